```python
import math
import jax, jax.numpy as jnp
from jax import lax
import numpy as np

D_MODEL = 4096
BATCH = 4
SEQ = 4096
DEPTH = 2

CTX_LEN = 256
GRID_W = 64
EPS = 1e-6
ROPE_BASE = 10000.0
S5_WIDTH = D_MODEL // 4
S5_GROUP = 16
S5_GROUPS = S5_WIDTH // S5_GROUP
S5_STATE = 64
S5_DT_MIN = 0.001
S5_DT_MAX = 0.1
MLSTM_WIDTH = D_MODEL - S5_WIDTH
MLSTM_DV = 512
MLSTM_DQK = MLSTM_DV // 2
MLSTM_HEADS = MLSTM_WIDTH // MLSTM_DV
MLSTM_CHUNK = 64
MLSTM_QK_W = MLSTM_HEADS * MLSTM_DQK
MLSTM_GATE_W = 2 * 2 * MLSTM_HEADS
AB_SPLITS = (MLSTM_QK_W, 2 * MLSTM_QK_W, 2 * MLSTM_QK_W + MLSTM_WIDTH,
             2 * MLSTM_QK_W + 2 * MLSTM_WIDTH, 2 * MLSTM_QK_W + 2 * MLSTM_WIDTH + MLSTM_GATE_W)
AB_IN_WIDTH = 2 * MLSTM_QK_W + 2 * MLSTM_WIDTH + MLSTM_GATE_W + S5_WIDTH
NA_HEADS = 32
NA_HEAD_DIM = D_MODEL // NA_HEADS
WIN_H = 8
WIN_W = 16
D_FF = 128 * ((8 * D_MODEL // 3 + 127) // 128)
CONV_W = 3

kernel_name = "hybrid_mlstm_s5_natten_dit"


def rms_norm(x, g):
    xf = x.astype(jnp.float32)
    y = xf * lax.rsqrt(jnp.mean(xf * xf, axis=-1, keepdims=True) + EPS)
    return (y * g.astype(jnp.float32)).astype(x.dtype)


def modulate(x, g, shift, scale):
    return rms_norm(x, g) * (1 + scale) + shift


def ada_chunks(cond, w, b):
    m = jax.nn.silu(cond) @ w + b
    return [t[:, None, :] for t in jnp.split(m, 6, axis=-1)]


def split_heads(t, n_heads):
    b, n, w = t.shape
    return t.reshape(b, n, n_heads, w // n_heads).transpose(0, 2, 1, 3)


def merge_heads(t):
    b, h, n, d = t.shape
    return t.transpose(0, 2, 1, 3).reshape(b, n, h * d)


def axial_rope(x):
    n, d = x.shape[2], x.shape[3]
    t = jnp.arange(n)
    half = d // 2
    inv = ROPE_BASE ** (-jnp.arange(0, half, 2, dtype=jnp.float32) / half)

    def rot(xa, pos):
        ang = pos.astype(jnp.float32)[:, None] * inv
        cos, sin = jnp.cos(ang), jnp.sin(ang)
        x1, x2 = jnp.split(xa.astype(jnp.float32), 2, axis=-1)
        return jnp.concatenate([x1 * cos - x2 * sin, x1 * sin + x2 * cos], axis=-1)

    out = jnp.concatenate([rot(x[..., :half], t // GRID_W), rot(x[..., half:], t % GRID_W)], axis=-1)
    return out.astype(x.dtype)


def mlstm_chunkwise(q, k, v, log_i, log_f, state):
    b_, h_, n_, _ = q.shape
    nc = n_ // MLSTM_CHUNK

    def chunks(t):
        t = t.reshape((b_, h_, nc, MLSTM_CHUNK) + t.shape[3:])
        return jnp.moveaxis(t, 2, 0)

    lower = jnp.tril(jnp.ones((MLSTM_CHUNK, MLSTM_CHUNK), dtype=bool))

    def step(carry, xs):
        c_st, n_st, m_st = carry
        qc, kc, vc, li, lf = xs
        cum_f = jnp.cumsum(lf, axis=-1)
        d = jnp.where(lower, cum_f[..., :, None] - cum_f[..., None, :] + li[..., None, :], -jnp.inf)
        carried = cum_f + m_st[..., None]
        m_loc = jnp.maximum(carried, jnp.max(d, axis=-1))
        w = jnp.exp(d - m_loc[..., None])
        w_state = jnp.exp(carried - m_loc)
        sc = jnp.einsum('bhld,bhsd->bhls', qc, kc) * w
        num = jnp.einsum('bhls,bhsv->bhlv', sc, vc) + w_state[..., None] * jnp.einsum('bhvd,bhld->bhlv', c_st, qc)
        den = jnp.sum(sc, axis=-1) + w_state * jnp.einsum('bhd,bhld->bhl', n_st, qc)
        h = num / jnp.maximum(jnp.abs(den), jnp.exp(-m_loc))[..., None]
        total_f = cum_f[..., -1]
        src = total_f[..., None] - cum_f + li
        m_new = jnp.maximum(total_f + m_st, jnp.max(src, axis=-1))
        decay = jnp.exp(total_f + m_st - m_new)
        w_src = jnp.exp(src - m_new[..., None])
        c_st = decay[..., None, None] * c_st + jnp.einsum('bhl,bhlv,bhld->bhvd', w_src, vc, kc)
        n_st = decay[..., None] * n_st + jnp.einsum('bhl,bhld->bhd', w_src, kc)
        return (c_st, n_st, m_new), h

    state, hs = lax.scan(step, state, tuple(chunks(t) for t in (q, k, v, log_i, log_f)))
    return jnp.moveaxis(hs, 0, 2).reshape(b_, h_, n_, v.shape[-1]), state


def mlstm_bidirectional(q_c, k_c, v_c, g_c, q_l, k_l, v_l, g_l, gate_b):
    f32 = jnp.float32
    q_c, k_c, v_c, q_l, k_l, v_l = (t.astype(f32) for t in (q_c, k_c, v_c, q_l, k_l, v_l))
    b_ = q_l.shape[0]

    def gates(g, direction):
        g = g.astype(f32).reshape(g.shape[0], g.shape[1], 2, 2, MLSTM_HEADS) + gate_b.astype(f32)
        li = jnp.transpose(g[:, :, direction, 0, :], (0, 2, 1))
        lf = jax.nn.log_sigmoid(jnp.transpose(g[:, :, direction, 1, :], (0, 2, 1)))
        return li, lf

    flip = lambda t: jnp.flip(t, axis=2)
    state0 = (jnp.zeros((b_, MLSTM_HEADS, MLSTM_DV, MLSTM_DQK), f32),
              jnp.zeros((b_, MLSTM_HEADS, MLSTM_DQK), f32),
              jnp.zeros((b_, MLSTM_HEADS), f32))
    li_c, lf_c = gates(g_c, 0)
    li_l, lf_l = gates(g_l, 0)
    hc_f, st = mlstm_chunkwise(q_c, k_c, v_c, li_c, lf_c, state0)
    hl_f, _ = mlstm_chunkwise(q_l, k_l, v_l, li_l, lf_l, st)
    li_c, lf_c = gates(g_c, 1)
    li_l, lf_l = gates(g_l, 1)
    hc_b, st = mlstm_chunkwise(flip(q_c), flip(k_c), flip(v_c), flip(li_c), flip(lf_c), state0)
    hl_b, _ = mlstm_chunkwise(flip(q_l), flip(k_l), flip(v_l), flip(li_l), flip(lf_l), st)
    return hc_f + flip(hc_b), hl_f + flip(hl_b)


def mlstm_readout(h, o, head_g):
    b_, _, n_, _ = h.shape
    hn = rms_norm(h.transpose(0, 2, 1, 3), head_g.reshape(MLSTM_HEADS, MLSTM_DV))
    return (jax.nn.sigmoid(o.astype(jnp.float32)) * hn.reshape(b_, n_, MLSTM_WIDTH)).astype(o.dtype)


def s5_discretise(a_re, a_im, log_dt):
    lam = lax.complex(a_re.astype(jnp.float32), a_im.astype(jnp.float32))
    a_bar = jnp.exp(lam * jnp.exp(log_dt.astype(jnp.float32))[:, None])
    return a_bar, (a_bar - 1) / lam


def s5_scan(bu, a_bar, init):
    bu = bu.at[:, 0].add(a_bar * init)
    a = jnp.broadcast_to(a_bar, (1, bu.shape[1]) + a_bar.shape)

    def op(e1, e2):
        return e1[0] * e2[0], e2[0] * e1[1] + e2[1]

    _, s = lax.associative_scan(op, (a, bu), axis=1)
    return s, s[:, -1]


def s5_bidirectional(u_c, u_l, a_re, a_im, log_dt, b_re, b_im, c_re, c_im, d_skip, glu_w, glu_b, need_ctx):
    f32 = jnp.float32
    bmat = lax.complex(b_re.astype(f32), b_im.astype(f32))
    cmat = lax.complex(c_re.astype(f32), c_im.astype(f32))

    def drive(u):
        ug = u.astype(f32).reshape(u.shape[0], u.shape[1], S5_GROUPS, S5_GROUP).astype(jnp.complex64)
        return jnp.einsum('bngh,gph->bngp', ug, bmat)

    bu_c, bu_l = drive(u_c), drive(u_l)
    init0 = jnp.zeros((u_l.shape[0], S5_GROUPS, S5_STATE), jnp.complex64)
    a_f, fac_f = s5_discretise(a_re[0], a_im[0], log_dt[0])
    s_cf, fin = s5_scan(bu_c * fac_f, a_f, init0)
    s_lf, _ = s5_scan(bu_l * fac_f, a_f, fin)
    a_b, fac_b = s5_discretise(a_re[1], a_im[1], log_dt[1])
    s_cb, fin = s5_scan(jnp.flip(bu_c, 1) * fac_b, a_b, init0)
    s_lb, _ = s5_scan(jnp.flip(bu_l, 1) * fac_b, a_b, fin)

    def readout(u, s):
        y = jnp.real(jnp.einsum('bngp,ghp->bngh', s, cmat)).reshape(u.shape) + d_skip.astype(f32) * u.astype(f32)
        g = jax.nn.gelu(y)
        return (g * jax.nn.sigmoid(g @ glu_w.astype(f32) + glu_b.astype(f32))).astype(u.dtype)

    y_l = readout(u_l, s_lf + jnp.flip(s_lb, 1))
    y_c = readout(u_c, s_cf + jnp.flip(s_cb, 1)) if need_ctx else None
    return y_c, y_l


def ab_mixer(h_c, h_l, w_in, gate_b, head_g, a_re, a_im, log_dt, b_re, b_im, c_re, c_im,
             d_skip, glu_w, glu_b, w_out, need_ctx):
    qc, kc, vc, oc, gc, uc = jnp.split(h_c @ w_in, AB_SPLITS, axis=-1)
    ql, kl, vl, ol, gl, ul = jnp.split(h_l @ w_in, AB_SPLITS, axis=-1)
    scale = MLSTM_DQK ** -0.5
    q_c = split_heads(qc, MLSTM_HEADS) * scale
    k_c = split_heads(kc, MLSTM_HEADS)
    q_l = axial_rope(split_heads(ql, MLSTM_HEADS)) * scale
    k_l = axial_rope(split_heads(kl, MLSTM_HEADS))
    hm_c, hm_l = mlstm_bidirectional(q_c, k_c, split_heads(vc, MLSTM_HEADS), gc,
                                     q_l, k_l, split_heads(vl, MLSTM_HEADS), gl, gate_b)
    ys_c, ys_l = s5_bidirectional(uc, ul, a_re, a_im, log_dt, b_re, b_im, c_re, c_im,
                                  d_skip, glu_w, glu_b, need_ctx)
    out_l = jnp.concatenate([mlstm_readout(hm_l, ol, head_g), ys_l], axis=-1) @ w_out
    out_c = (jnp.concatenate([mlstm_readout(hm_c, oc, head_g), ys_c], axis=-1) @ w_out) if need_ctx else None
    return out_c, out_l


def neighborhood_attention(q, k, v, k_ctx, v_ctx, rpb):
    b_, h_, n_, d = q.shape
    rows = n_ // GRID_W
    kh = min(WIN_H, rows)
    kw = WIN_W
    scale = d ** -0.5
    qg, kg, vg = (t.reshape(b_, h_, rows, GRID_W, d) for t in (q, k, v))
    col = jnp.arange(GRID_W)
    col_start = jnp.clip(col - kw // 2, 0, GRID_W - kw)
    col_mask = (col[None, :] >= col_start[:, None]) & (col[None, :] < col_start[:, None] + kw)
    ci = jnp.clip(col[None, :] - col[:, None], -(WIN_W - 1), WIN_W - 1) + (WIN_W - 1)

    def row_block(r):
        r_start = jnp.clip(r - kh // 2, 0, rows - kh)
        q_r = lax.dynamic_index_in_dim(qg, r, axis=2, keepdims=False)
        k_band = lax.dynamic_slice_in_dim(kg, r_start, kh, axis=2)
        v_band = lax.dynamic_slice_in_dim(vg, r_start, kh, axis=2)
        ri = r_start + jnp.arange(kh) - r + (WIN_H - 1)
        bias = rpb[:, ri[None, :, None], ci[:, None, :]].astype(jnp.float32)
        s_loc = jnp.einsum('bhqd,bhrkd->bhqrk', q_r, k_band).astype(jnp.float32) * scale + bias
        s_loc = jnp.where(col_mask[:, None, :], s_loc, -jnp.inf)
        s_ctx = jnp.einsum('bhqd,bhcd->bhqc', q_r, k_ctx).astype(jnp.float32) * scale
        p = jax.nn.softmax(jnp.concatenate([s_loc.reshape(b_, h_, GRID_W, kh * GRID_W), s_ctx], axis=-1), axis=-1)
        p = p.astype(v.dtype)
        p_loc = p[..., :kh * GRID_W].reshape(b_, h_, GRID_W, kh, GRID_W)
        return (jnp.einsum('bhqrk,bhrkd->bhqd', p_loc, v_band)
                + jnp.einsum('bhqc,bhcd->bhqd', p[..., kh * GRID_W:], v_ctx))

    out = lax.map(row_block, jnp.arange(rows))
    return jnp.moveaxis(out, 0, 2).reshape(b_, h_, n_, d)


def na_mixer(h_c, h_l, w_qkv, rpb, w_out, need_ctx):
    q_l, k_l, v_l = (split_heads(t, NA_HEADS) for t in jnp.split(h_l @ w_qkv, 3, axis=-1))
    k_c, v_c = (split_heads(t, NA_HEADS) for t in jnp.split(h_c @ w_qkv[:, D_MODEL:], 2, axis=-1))
    out_l = merge_heads(neighborhood_attention(q_l, k_l, v_l, k_c, v_c, rpb)) @ w_out
    out_c = None
    if need_ctx:
        q_c = split_heads(h_c @ w_qkv[:, :D_MODEL], NA_HEADS)
        s = jnp.einsum('bhqd,bhkd->bhqk', q_c, k_c).astype(jnp.float32) * NA_HEAD_DIM ** -0.5
        o_c = jnp.einsum('bhqk,bhkd->bhqd', jax.nn.softmax(s, axis=-1).astype(v_c.dtype), v_c)
        out_c = merge_heads(o_c) @ w_out
    return out_c, out_l


def conv_ffn(h, w_up, conv_w, conv_b, w_down):
    a, g = jnp.split(h @ w_up, 2, axis=-1)
    g = lax.conv_general_dilated(g, conv_w[:, None, :], window_strides=(1,),
                                 padding=((CONV_W // 2, CONV_W // 2),),
                                 dimension_numbers=('NWC', 'WIO', 'NWC'),
                                 feature_group_count=D_FF) + conv_b
    return (jax.nn.gelu(g) * a) @ w_down


def setup_inputs(seed: int = 0) -> dict:
    key = jax.random.key(seed)
    ks = iter(jax.random.split(key, 32))
    f32 = jnp.float32
    n_even = (DEPTH + 1) // 2
    n_odd = DEPTH // 2

    def nrm(shape, scale):
        return scale * jax.random.normal(next(ks), shape, f32)

    x = nrm((BATCH, SEQ, D_MODEL), 1.0)
    c = nrm((BATCH, D_MODEL), 1.0)
    ctx = nrm((BATCH, CTX_LEN, D_MODEL), 1.0)
    c_ctx = nrm((D_MODEL,), 1.0)
    mod_w = nrm((DEPTH, D_MODEL, 6 * D_MODEL), 0.5 * D_MODEL ** -0.5)
    mod_b = nrm((DEPTH, 6 * D_MODEL), 0.01)
    norm_mix_g = 1.0 + nrm((DEPTH, D_MODEL), 0.02)
    norm_ffn_g = 1.0 + nrm((DEPTH, D_MODEL), 0.02)
    ab_w_in = nrm((n_even, D_MODEL, AB_IN_WIDTH), D_MODEL ** -0.5)
    gate_base = jnp.stack([jnp.zeros((MLSTM_HEADS,), f32), jnp.linspace(3.0, 6.0, MLSTM_HEADS, dtype=f32)])
    mlstm_gate_b = nrm((n_even, 2, 2, MLSTM_HEADS), 0.1) + gate_base[None, None]
    mlstm_head_g = 1.0 + nrm((n_even, MLSTM_WIDTH), 0.02)
    s5_a_re = -0.5 + nrm((n_even, 2, S5_GROUPS, S5_STATE), 0.01)
    s5_a_im = math.pi * jnp.arange(S5_STATE, dtype=f32) + nrm((n_even, 2, S5_GROUPS, S5_STATE), 0.01)
    s5_log_dt = jax.random.uniform(next(ks), (n_even, 2, S5_GROUPS), f32,
                                   math.log(S5_DT_MIN), math.log(S5_DT_MAX))
    s5_b_re = nrm((n_even, S5_GROUPS, S5_STATE, S5_GROUP), (2 * S5_GROUP) ** -0.5)
    s5_b_im = nrm((n_even, S5_GROUPS, S5_STATE, S5_GROUP), (2 * S5_GROUP) ** -0.5)
    s5_c_re = nrm((n_even, S5_GROUPS, S5_GROUP, S5_STATE), S5_STATE ** -0.5)
    s5_c_im = nrm((n_even, S5_GROUPS, S5_GROUP, S5_STATE), S5_STATE ** -0.5)
    s5_d = nrm((n_even, S5_WIDTH), 0.5)
    s5_glu_w = nrm((n_even, S5_WIDTH, S5_WIDTH), S5_WIDTH ** -0.5)
    s5_glu_b = nrm((n_even, S5_WIDTH), 0.01)
    ab_w_out = nrm((n_even, D_MODEL, D_MODEL), D_MODEL ** -0.5)
    na_w_qkv = nrm((n_odd, D_MODEL, 3 * D_MODEL), D_MODEL ** -0.5)
    na_rpb = nrm((n_odd, NA_HEADS, 2 * WIN_H - 1, 2 * WIN_W - 1), 0.02)
    na_w_out = nrm((n_odd, D_MODEL, D_MODEL), D_MODEL ** -0.5)
    ffn_w_up = nrm((DEPTH, D_MODEL, 2 * D_FF), D_MODEL ** -0.5)
    ffn_conv_w = nrm((DEPTH, CONV_W, D_FF), CONV_W ** -0.5)
    ffn_conv_b = nrm((DEPTH, D_FF), 0.01)
    ffn_w_down = nrm((DEPTH, D_FF, D_MODEL), D_FF ** -0.5)
    final_norm_g = 1.0 + nrm((D_MODEL,), 0.02)
    return {"x": x, "c": c, "ctx": ctx, "c_ctx": c_ctx, "mod_w": mod_w, "mod_b": mod_b,
            "norm_mix_g": norm_mix_g, "norm_ffn_g": norm_ffn_g, "ab_w_in": ab_w_in,
            "mlstm_gate_b": mlstm_gate_b, "mlstm_head_g": mlstm_head_g,
            "s5_a_re": s5_a_re, "s5_a_im": s5_a_im, "s5_log_dt": s5_log_dt,
            "s5_b_re": s5_b_re, "s5_b_im": s5_b_im, "s5_c_re": s5_c_re, "s5_c_im": s5_c_im,
            "s5_d": s5_d, "s5_glu_w": s5_glu_w, "s5_glu_b": s5_glu_b, "ab_w_out": ab_w_out,
            "na_w_qkv": na_w_qkv, "na_rpb": na_rpb, "na_w_out": na_w_out,
            "ffn_w_up": ffn_w_up, "ffn_conv_w": ffn_conv_w, "ffn_conv_b": ffn_conv_b,
            "ffn_w_down": ffn_w_down, "final_norm_g": final_norm_g}


def reference(x, c, ctx, c_ctx, mod_w, mod_b, norm_mix_g, norm_ffn_g, ab_w_in, mlstm_gate_b,
              mlstm_head_g, s5_a_re, s5_a_im, s5_log_dt, s5_b_re, s5_b_im, s5_c_re, s5_c_im,
              s5_d, s5_glu_w, s5_glu_b, ab_w_out, na_w_qkv, na_rpb, na_w_out,
              ffn_w_up, ffn_conv_w, ffn_conv_b, ffn_w_down, final_norm_g):
    for i in range(DEPTH):
        need_ctx = i < DEPTH - 1
        sh_m, sc_m, gt_m, sh_f, sc_f, gt_f = ada_chunks(c, mod_w[i], mod_b[i])
        csh_m, csc_m, cgt_m, csh_f, csc_f, cgt_f = ada_chunks(c_ctx[None], mod_w[i], mod_b[i])
        h_l = modulate(x, norm_mix_g[i], sh_m, sc_m)
        h_c = modulate(ctx, norm_mix_g[i], csh_m, csc_m)
        if i % 2 == 0:
            e = i // 2
            mix_c, mix_l = ab_mixer(h_c, h_l, ab_w_in[e], mlstm_gate_b[e], mlstm_head_g[e],
                                    s5_a_re[e], s5_a_im[e], s5_log_dt[e], s5_b_re[e], s5_b_im[e],
                                    s5_c_re[e], s5_c_im[e], s5_d[e], s5_glu_w[e], s5_glu_b[e],
                                    ab_w_out[e], need_ctx)
        else:
            o = i // 2
            mix_c, mix_l = na_mixer(h_c, h_l, na_w_qkv[o], na_rpb[o], na_w_out[o], need_ctx)
        x = x + gt_m * mix_l
        x = x + gt_f * conv_ffn(modulate(x, norm_ffn_g[i], sh_f, sc_f),
                                ffn_w_up[i], ffn_conv_w[i], ffn_conv_b[i], ffn_w_down[i])
        if need_ctx:
            ctx = ctx + cgt_m * mix_c
            ctx = ctx + cgt_f * conv_ffn(modulate(ctx, norm_ffn_g[i], csh_f, csc_f),
                                         ffn_w_up[i], ffn_conv_w[i], ffn_conv_b[i], ffn_w_down[i])
    return rms_norm(x, final_norm_g)
```

```python
import functools
import math

import jax
import jax.numpy as jnp
from jax import lax
from jax.experimental import pallas as pl
from jax.experimental.pallas import tpu as pltpu

F32 = jnp.float32
BF16 = jnp.bfloat16

EPS = 1e-6
ROPE_BASE = 10000.0
GRID_W = 64
MLSTM_CHUNK = 256
S5_CHUNK = 16
NA_QROWS = 8
NA_KROWS = 16
LANE = 128
VMEM_LIMIT_BYTES = 58 * 1024 * 1024


def _cparams(*sem):
    return pltpu.CompilerParams(dimension_semantics=sem, vmem_limit_bytes=VMEM_LIMIT_BYTES)


def _pick(n, candidates):
    for c in candidates:
        if n % c == 0:
            return c
    raise ValueError(f"no tile in {candidates} divides {n}")


def _ada_kernel(c_ref, w_ref, b_ref, o_ref):
    s = jax.nn.silu(c_ref[...]).astype(BF16)
    o_ref[...] = jnp.dot(s, w_ref[...].astype(BF16), preferred_element_type=F32) + b_ref[...]


def _ada(cond8, mod_w, mod_b):
    depth, d, n6 = mod_w.shape
    tn = _pick(n6, (512, 256, 128))
    return pl.pallas_call(
        _ada_kernel,
        out_shape=jax.ShapeDtypeStruct((depth, 8, n6), F32),
        grid=(depth, n6 // tn),
        in_specs=[pl.BlockSpec((8, d), lambda l, j: (0, 0)),
                  pl.BlockSpec((None, d, tn), lambda l, j: (l, 0, j)),
                  pl.BlockSpec((None, 1, tn), lambda l, j: (l, 0, j))],
        out_specs=pl.BlockSpec((None, 8, tn), lambda l, j: (l, 0, j)),
        compiler_params=_cparams("parallel", "parallel"),
        name="ada_mod",
    )(cond8, mod_w, mod_b.reshape(depth, 1, n6))


def _modulate_kernel(x_ref, g_ref, sh_ref, sc_ref, o_ref):
    x = x_ref[...]
    y = x * lax.rsqrt(jnp.mean(x * x, axis=-1, keepdims=True) + EPS) * g_ref[...]
    o_ref[...] = (y * (1.0 + sc_ref[...]) + sh_ref[...]).astype(o_ref.dtype)


def _seg_fn(tm, n_lat_rows, seq, n_batch):
    n_lat_blk = n_lat_rows // tm

    def seg(i):
        return jnp.where(i < n_lat_blk, (i * tm) // seq, n_batch)
    return seg


def _modulate(xs, gain, mods, shift_idx, scale_idx, seg, tm):
    r, d = xs.shape
    return pl.pallas_call(
        _modulate_kernel,
        out_shape=jax.ShapeDtypeStruct((r, d), BF16),
        grid=(r // tm,),
        in_specs=[pl.BlockSpec((tm, d), lambda i: (i, 0)),
                  pl.BlockSpec((1, d), lambda i: (0, 0)),
                  pl.BlockSpec((None, None, 1, d), lambda i: (seg(i), shift_idx, 0, 0)),
                  pl.BlockSpec((None, None, 1, d), lambda i: (seg(i), scale_idx, 0, 0))],
        out_specs=pl.BlockSpec((tm, d), lambda i: (i, 0)),
        compiler_params=_cparams("parallel"),
        name="modulate",
    )(xs, gain.reshape(1, d), mods, mods)


def _rmsnorm_kernel(x_ref, g_ref, o_ref):
    x = x_ref[...]
    o_ref[...] = x * lax.rsqrt(jnp.mean(x * x, axis=-1, keepdims=True) + EPS) * g_ref[...]


def _rmsnorm(xs, gain, tm):
    r, d = xs.shape
    return pl.pallas_call(
        _rmsnorm_kernel,
        out_shape=jax.ShapeDtypeStruct((r, d), F32),
        grid=(r // tm,),
        in_specs=[pl.BlockSpec((tm, d), lambda i: (i, 0)),
                  pl.BlockSpec((1, d), lambda i: (0, 0))],
        out_specs=pl.BlockSpec((tm, d), lambda i: (i, 0)),
        compiler_params=_cparams("parallel"),
        name="final_norm",
    )(xs, gain.reshape(1, d))


def _mm_kernel(a_ref, b_ref, o_ref):
    o_ref[...] = jnp.dot(a_ref[...], b_ref[...], preferred_element_type=F32).astype(o_ref.dtype)


def _matmul(a, b, out_dtype, tm, tn, name):
    m, k = a.shape
    n = b.shape[1]
    return pl.pallas_call(
        _mm_kernel,
        out_shape=jax.ShapeDtypeStruct((m, n), out_dtype),
        grid=(m // tm, n // tn),
        in_specs=[pl.BlockSpec((tm, k), lambda i, j: (i, 0)),
                  pl.BlockSpec((k, tn), lambda i, j: (0, j))],
        out_specs=pl.BlockSpec((tm, tn), lambda i, j: (i, j)),
        compiler_params=_cparams("parallel", "parallel"),
        name=name,
    )(a, b)


def _mm_res_kernel(*refs, n_a):
    a_refs, b_refs = refs[:n_a], refs[n_a:2 * n_a]
    x_ref, gate_ref, o_ref = refs[2 * n_a:]
    acc = jnp.dot(a_refs[0][...], b_refs[0][...], preferred_element_type=F32)
    for a_ref, b_ref in zip(a_refs[1:], b_refs[1:]):
        acc = acc + jnp.dot(a_ref[...], b_ref[...], preferred_element_type=F32)
    o_ref[...] = x_ref[...] + gate_ref[...] * acc


def _matmul_residual(a_list, b_list, xs, mods, gate_idx, seg, tm, tn, name):
    m = a_list[0].shape[0]
    n = b_list[0].shape[1]
    in_specs = ([pl.BlockSpec((tm, a.shape[1]), lambda i, j: (i, 0)) for a in a_list]
                + [pl.BlockSpec((b.shape[0], tn), lambda i, j: (0, j)) for b in b_list]
                + [pl.BlockSpec((tm, tn), lambda i, j: (i, j)),
                   pl.BlockSpec((None, None, 1, tn), lambda i, j: (seg(i), gate_idx, 0, j))])
    return pl.pallas_call(
        functools.partial(_mm_res_kernel, n_a=len(a_list)),
        out_shape=jax.ShapeDtypeStruct((m, n), F32),
        grid=(m // tm, n // tn),
        in_specs=in_specs,
        out_specs=pl.BlockSpec((tm, tn), lambda i, j: (i, j)),
        compiler_params=_cparams("parallel", "parallel"),
        name=name,
    )(*a_list, *b_list, xs, mods)


def _rope(x, cos, sin):
    parts = []
    for j in range(x.shape[1] // LANE):
        sl = slice(j * LANE, (j + 1) * LANE)
        xs = x[:, sl]
        parts.append(xs * cos[:, sl] + pltpu.roll(xs, LANE // 2, axis=1) * sin[:, sl])
    return jnp.concatenate(parts, axis=1)


def _mlstm_kernel(gb_ref, q_ref, k_ref, v_ref, cos_ref, sin_ref, gcol_ref, grow_ref, *rest,
                  reverse, n_heads, scale):
    if reverse:
        o_ref, hf_ref, hg_ref, out_ref, ct_ref, m_ref = rest
    else:
        out_ref, ct_ref, m_ref = rest
    hh = pl.program_id(1)
    step = pl.program_id(2)
    d = 1 if reverse else 0
    L, dv = v_ref.shape

    @pl.when(step == 0)
    def _():
        ct_ref[...] = jnp.zeros_like(ct_ref)
        m_ref[...] = jnp.zeros_like(m_ref)

    b_i = gb_ref[(d * 2 + 0) * n_heads + hh]
    b_f = gb_ref[(d * 2 + 1) * n_heads + hh]
    li_col = gcol_ref[:, 0:1] + b_i
    lf_col = jax.nn.log_sigmoid(gcol_ref[:, 1:2] + b_f)
    li_row = grow_ref[0:1, :] + b_i
    lf_row = jax.nn.log_sigmoid(grow_ref[1:2, :] + b_f)

    row = lax.broadcasted_iota(jnp.int32, (L, L), 0)
    col = lax.broadcasted_iota(jnp.int32, (L, L), 1)
    incl = (col >= row) if reverse else (col <= row)
    incl_t = (row >= col) if reverse else (row <= col)
    cum_col = jnp.sum(jnp.where(incl, lf_row, 0.0), axis=1, keepdims=True)
    cum_row = jnp.sum(jnp.where(incl_t, lf_col, 0.0), axis=0, keepdims=True)
    total = jnp.sum(lf_row, axis=1, keepdims=True)
    m_prev = m_ref[0:1, 0:1]

    dmat = jnp.where(incl, cum_col - cum_row + li_row, -jnp.inf)
    carried = cum_col + m_prev
    m_loc = jnp.maximum(carried, jnp.max(dmat, axis=1, keepdims=True))
    w = jnp.exp(dmat - m_loc)
    w_state = jnp.exp(carried - m_loc)

    cos = cos_ref[...]
    sin = sin_ref[...]
    qb = (_rope(q_ref[...], cos, sin) * scale).astype(BF16)
    kb = _rope(k_ref[...], cos, sin).astype(BF16)
    s_qk = lax.dot_general(qb, kb, (((1,), (1,)), ((), ())), preferred_element_type=F32)
    sc = (s_qk * w).astype(BF16)
    vext = jnp.concatenate([v_ref[...], jnp.ones((L, LANE), F32)], axis=1)
    ct = ct_ref[...]
    res = (jnp.dot(sc, vext.astype(BF16), preferred_element_type=F32)
           + w_state * jnp.dot(qb, ct.astype(BF16), preferred_element_type=F32))
    den = res[:, dv:dv + 1]
    hval = res[:, :dv] / jnp.maximum(jnp.abs(den), jnp.exp(-m_loc))

    src = total - cum_col + li_col
    m_new = jnp.maximum(total + m_prev, jnp.max(src, axis=0, keepdims=True))
    decay = jnp.exp(total + m_prev - m_new)
    w_src = jnp.exp(src - m_new)
    upd = lax.dot_general(kb, (w_src * vext).astype(BF16), (((0,), (0,)), ((), ())),
                          preferred_element_type=F32)
    ct_ref[...] = decay * ct + upd
    m_ref[...] = jnp.broadcast_to(m_new, m_ref.shape)

    if reverse:
        hm = hf_ref[...] + hval
        hn = hm * lax.rsqrt(jnp.mean(hm * hm, axis=-1, keepdims=True) + EPS) * hg_ref[...]
        out_ref[...] = (jax.nn.sigmoid(o_ref[...]) * hn).astype(out_ref.dtype)
    else:
        out_ref[...] = hval


def _mlstm_direction(p1, gcol, grow, gate_b, cos_t, sin_t, head_g, hf, *, reverse, n_batch, seq,
                     n_heads, dqk, dv):
    r = p1.shape[0]
    L = MLSTM_CHUNK
    n_lat = seq // L
    lat_blocks = n_batch * n_lat
    d = 1 if reverse else 0

    def blk(b, s):
        lat = b * n_lat + ((n_lat - s) if reverse else (s - 1))
        return jnp.where(s == 0, lat_blocks + b, lat)

    def tab(s):
        return jnp.where(s == 0, n_lat, (n_lat - s) if reverse else (s - 1))

    in_specs = [
        pl.BlockSpec(memory_space=pltpu.SMEM),
        pl.BlockSpec((L, dqk), lambda b, h, s: (blk(b, s), h)),
        pl.BlockSpec((L, dqk), lambda b, h, s: (blk(b, s), n_heads + h)),
        pl.BlockSpec((L, dv), lambda b, h, s: (blk(b, s), n_heads + h)),
        pl.BlockSpec((L, dqk), lambda b, h, s: (tab(s), 0)),
        pl.BlockSpec((L, dqk), lambda b, h, s: (tab(s), 0)),
        pl.BlockSpec((None, None, L, 2), lambda b, h, s: (d, h, blk(b, s), 0)),
        pl.BlockSpec((None, None, 2, L), lambda b, h, s: (d, h, 0, blk(b, s))),
    ]
    args = [gate_b.reshape(-1), p1, p1, p1, cos_t, sin_t, gcol, grow]
    if reverse:
        in_specs += [
            pl.BlockSpec((L, dv), lambda b, h, s: (blk(b, s), 2 * n_heads + h)),
            pl.BlockSpec((L, dv), lambda b, h, s: (blk(b, s), h)),
            pl.BlockSpec((None, 1, dv), lambda b, h, s: (h, 0, 0)),
        ]
        args += [p1, hf, head_g.reshape(n_heads, 1, dv)]
    return pl.pallas_call(
        functools.partial(_mlstm_kernel, reverse=reverse, n_heads=n_heads, scale=dqk ** -0.5),
        out_shape=jax.ShapeDtypeStruct((r, n_heads * dv), BF16 if reverse else F32),
        grid=(n_batch, n_heads, n_lat + 1),
        in_specs=in_specs,
        out_specs=pl.BlockSpec((L, dv), lambda b, h, s: (blk(b, s), h)),
        scratch_shapes=[pltpu.VMEM((dqk, dv + LANE), F32), pltpu.VMEM((8, LANE), F32)],
        compiler_params=_cparams("parallel", "parallel", "arbitrary"),
        name="mlstm_bwd" if reverse else "mlstm_fwd",
    )(*args)


def _rope_tables(seq, ctx_len, dqk):
    half = dqk // 2
    inv = ROPE_BASE ** (-jnp.arange(0, half, 2, dtype=F32) / half)
    t = jnp.arange(seq)

    def one(pos):
        ang = pos.astype(F32)[:, None] * inv
        c, s = jnp.cos(ang), jnp.sin(ang)
        return jnp.concatenate([c, c], axis=-1), jnp.concatenate([-s, s], axis=-1)

    c_r, s_r = one(t // GRID_W)
    c_c, s_c = one(t % GRID_W)
    cos_t = jnp.concatenate([c_r, c_c], axis=-1)
    sin_t = jnp.concatenate([s_r, s_c], axis=-1)
    cos_t = jnp.concatenate([cos_t, jnp.ones((ctx_len, dqk), F32)], axis=0)
    sin_t = jnp.concatenate([sin_t, jnp.zeros((ctx_len, dqk), F32)], axis=0)
    return cos_t, sin_t


def _s5_tables(a_re, a_im, log_dt, b_re, b_im, c_re, c_im, d_skip):
    g_, p_, hsz = b_re.shape
    lc = S5_CHUNK
    bmat = lax.complex(b_re.astype(F32), b_im.astype(F32))
    cmat = lax.complex(c_re.astype(F32), c_im.astype(F32))
    tau = jnp.arange(lc + 1, dtype=F32)
    s_idx = jnp.arange(lc)
    m_tot = 0.0
    w_parts, v_parts, a_parts = [], [], []
    for d in range(2):
        lam = lax.complex(a_re[d].astype(F32), a_im[d].astype(F32))
        dt = jnp.exp(log_dt[d].astype(F32))[:, None]
        fac = (jnp.exp(lam * dt) - 1) / lam
        apow = jnp.exp(tau[:, None, None] * (lam * dt)[None])
        bf = bmat * fac[:, :, None]
        cb = cmat.transpose(0, 2, 1)[:, :, :, None] * bf[:, :, None, :]
        kt = jnp.real(jnp.einsum('tgp,gpij->tgij', apow[:lc], cb,
                                 precision=lax.Precision.HIGHEST))
        lag = (s_idx[None, :] - s_idx[:, None]) if d == 0 else (s_idx[:, None] - s_idx[None, :])
        ok = lag >= 0
        kk = kt[jnp.clip(lag, 0, lc - 1)]
        kk = jnp.where(ok[:, :, None, None, None], kk, 0.0)
        m_tot = m_tot + kk.transpose(2, 0, 4, 1, 3).reshape(g_, lc * hsz, lc * hsz)
        pw = apow[(lc - 1 - s_idx) if d == 0 else s_idx]
        wc = pw[:, :, :, None] * bf[None]
        wc = wc.transpose(1, 0, 3, 2).reshape(g_, lc * hsz, p_)
        pv = apow[(s_idx + 1) if d == 0 else (lc - s_idx)]
        vc = cmat[None] * pv[:, :, None, :]
        vc = vc.transpose(1, 3, 0, 2).reshape(g_, p_, lc * hsz)
        pad_w = jnp.zeros((g_, lc * hsz, LANE - p_), F32)
        pad_v = jnp.zeros((g_, LANE - p_, lc * hsz), F32)
        w_parts += [jnp.real(wc), pad_w, jnp.imag(wc), pad_w]
        v_parts += [jnp.real(vc), pad_v, -jnp.imag(vc), pad_v]
        a_chunk = apow[lc]
        pad_a = jnp.zeros((g_, LANE - p_), F32)
        a_parts += [jnp.concatenate([jnp.real(a_chunk), pad_a], axis=1),
                    jnp.concatenate([jnp.imag(a_chunk), pad_a], axis=1)]
    w_all = jnp.concatenate(w_parts, axis=2)
    v_all = jnp.concatenate(v_parts, axis=1)
    a_all = jnp.stack(a_parts + [jnp.zeros_like(a_parts[0])] * 4, axis=1)
    d_flat = jnp.tile(d_skip.astype(F32).reshape(g_, 1, hsz), (1, 1, lc))
    return m_tot, w_all, v_all, a_all, d_flat


def _s5_kernel(u_ref, m_ref, w_ref, v_ref, a_ref, d_ref, y_ref, z_ref, sp_ref, *, nb, n_lat, n_ctx):
    hi = lax.Precision.HIGHEST
    u = u_ref[...]
    z_ref[...] = jnp.dot(u, w_ref[...], preferred_element_type=F32, precision=hi)
    n_chunks = n_lat + n_ctx
    a = a_ref[...]
    coef = [(a[2 * d:2 * d + 1, :], a[2 * d + 1:2 * d + 2, :]) for d in range(2)]
    state = [(jnp.zeros((nb, LANE), F32), jnp.zeros((nb, LANE), F32)) for _ in range(2)]
    for t in range(n_chunks):
        chunk = ((t + n_lat) % n_chunks, n_chunks - 1 - t)
        for d in range(2):
            r0 = chunk[d] * nb
            c0 = d * 2 * LANE
            s_re, s_im = state[d]
            sp_ref[r0:r0 + nb, c0:c0 + LANE] = s_re
            sp_ref[r0:r0 + nb, c0 + LANE:c0 + 2 * LANE] = s_im
            a_r, a_i = coef[d]
            z_re = z_ref[r0:r0 + nb, c0:c0 + LANE]
            z_im = z_ref[r0:r0 + nb, c0 + LANE:c0 + 2 * LANE]
            state[d] = (a_r * s_re - a_i * s_im + z_re, a_r * s_im + a_i * s_re + z_im)
    y = (jnp.dot(u, m_ref[...], preferred_element_type=F32, precision=hi)
         + jnp.dot(sp_ref[...], v_ref[...], preferred_element_type=F32, precision=hi)
         + d_ref[...] * u)
    y_ref[...] = y


def _s5_scan(u_flat, tables, nb, n_lat, n_ctx):
    m_tot, w_all, v_all, a_all, d_flat = tables
    g_ = m_tot.shape[0]
    rows, width = u_flat.shape
    cw = width // g_
    return pl.pallas_call(
        functools.partial(_s5_kernel, nb=nb, n_lat=n_lat, n_ctx=n_ctx),
        out_shape=jax.ShapeDtypeStruct((rows, width), F32),
        grid=(g_,),
        in_specs=[pl.BlockSpec((rows, cw), lambda g: (0, g)),
                  pl.BlockSpec((None, cw, cw), lambda g: (g, 0, 0)),
                  pl.BlockSpec((None, cw, 4 * LANE), lambda g: (g, 0, 0)),
                  pl.BlockSpec((None, 4 * LANE, cw), lambda g: (g, 0, 0)),
                  pl.BlockSpec((None, 8, LANE), lambda g: (g, 0, 0)),
                  pl.BlockSpec((None, 1, cw), lambda g: (g, 0, 0))],
        out_specs=pl.BlockSpec((rows, cw), lambda g: (0, g)),
        scratch_shapes=[pltpu.VMEM((rows, 4 * LANE), F32), pltpu.VMEM((rows, 4 * LANE), F32)],
        compiler_params=_cparams("parallel"),
        name="s5_scan",
    )(u_flat, m_tot, w_all, v_all, a_all, d_flat)


def _glu_kernel(y_ref, w_ref, b_ref, o_ref):
    g = jax.nn.gelu(y_ref[...])
    z = jnp.dot(g.astype(BF16), w_ref[...], preferred_element_type=F32) + b_ref[...]
    o_ref[...] = (g * jax.nn.sigmoid(z)).astype(o_ref.dtype)


def _glu(y, glu_w, glu_b, tm):
    r, w = y.shape
    return pl.pallas_call(
        _glu_kernel,
        out_shape=jax.ShapeDtypeStruct((r, w), BF16),
        grid=(r // tm,),
        in_specs=[pl.BlockSpec((tm, w), lambda i: (i, 0)),
                  pl.BlockSpec((w, w), lambda i: (0, 0)),
                  pl.BlockSpec((1, w), lambda i: (0, 0))],
        out_specs=pl.BlockSpec((tm, w), lambda i: (i, 0)),
        compiler_params=_cparams("parallel"),
        name="s5_glu",
    )(y, glu_w.astype(BF16), glu_b.astype(F32).reshape(1, w))


def _na_bias(rpb, rows):
    n_heads, nri, nci = rpb.shape
    win_h, win_w = (nri + 1) // 2, (nci + 1) // 2
    kh = min(win_h, rows)
    n_blk = rows // NA_QROWS
    col = jnp.arange(GRID_W)
    col_start = jnp.clip(col - win_w // 2, 0, GRID_W - win_w)
    col_ok = (col[None, :] >= col_start[:, None]) & (col[None, :] < col_start[:, None] + win_w)
    ci = jnp.clip(col[None, :] - col[:, None], -(win_w - 1), win_w - 1) + (win_w - 1)
    out = []
    for i in (0, min(1, n_blk - 1), n_blk - 1):
        ks = min(max(i * NA_QROWS - kh // 2, 0), rows - NA_KROWS)
        qrow = i * NA_QROWS + jnp.arange(NA_QROWS)
        krow = ks + jnp.arange(NA_KROWS)
        r_start = jnp.clip(qrow - kh // 2, 0, rows - kh)
        row_ok = (krow[None, :] >= r_start[:, None]) & (krow[None, :] < r_start[:, None] + kh)
        ri = jnp.clip(krow[None, :] - qrow[:, None] + (win_h - 1), 0, nri - 1)
        bias = rpb.astype(F32)[:, ri[:, None, :, None], ci[None, :, None, :]]
        ok = row_ok[:, None, :, None] & col_ok[None, :, None, :]
        bias = jnp.where(ok[None], bias, -jnp.inf)
        out.append(bias.reshape(n_heads, NA_QROWS * GRID_W, NA_KROWS * GRID_W))
    return jnp.stack(out, axis=1)


def _na_kernel(q_ref, k_ref, v_ref, kc_ref, vc_ref, bias_ref, o_ref, *, rows, kh, scale):
    n_blk = rows // NA_QROWS
    tq = NA_QROWS * GRID_W
    tk = NA_KROWS * GRID_W
    kc = kc_ref[...]
    vc = vc_ref[...]
    nt = (((1,), (1,)), ((), ()))

    def body(i, carry):
        q0 = pl.multiple_of(i * tq, tq)
        ks = jnp.clip(i * NA_QROWS - kh // 2, 0, rows - NA_KROWS)
        k0 = pl.multiple_of(ks * GRID_W, 4 * GRID_W)
        variant = jnp.where(i == 0, 0, jnp.where(i == n_blk - 1, 2, 1))
        qb = q_ref[pl.ds(q0, tq), :]
        kb = k_ref[pl.ds(k0, tk), :]
        vb = v_ref[pl.ds(k0, tk), :]
        s_loc = lax.dot_general(qb, kb, nt, preferred_element_type=F32) * scale + bias_ref[variant]
        s_ctx = lax.dot_general(qb, kc, nt, preferred_element_type=F32) * scale
        m = jnp.maximum(jnp.max(s_loc, axis=1, keepdims=True), jnp.max(s_ctx, axis=1, keepdims=True))
        p_loc = jnp.exp(s_loc - m)
        p_ctx = jnp.exp(s_ctx - m)
        denom = jnp.sum(p_loc, axis=1, keepdims=True) + jnp.sum(p_ctx, axis=1, keepdims=True)
        acc = (jnp.dot(p_loc.astype(BF16), vb, preferred_element_type=F32)
               + jnp.dot(p_ctx.astype(BF16), vc, preferred_element_type=F32))
        o_ref[pl.ds(q0, tq), :] = (acc / denom).astype(o_ref.dtype)
        return carry

    lax.fori_loop(0, n_blk, body, 0)


def _na_attention(qkv, bias, *, n_batch, seq, ctx_len, n_heads, hd, kh):
    rows = seq // GRID_W
    lat_ctx_blk = (n_batch * seq) // ctx_len
    tq, tk = NA_QROWS * GRID_W, NA_KROWS * GRID_W
    return pl.pallas_call(
        functools.partial(_na_kernel, rows=rows, kh=kh, scale=hd ** -0.5),
        out_shape=jax.ShapeDtypeStruct((n_batch * seq, n_heads * hd), BF16),
        grid=(n_heads, n_batch),
        in_specs=[pl.BlockSpec((seq, hd), lambda h, b: (b, h)),
                  pl.BlockSpec((seq, hd), lambda h, b: (b, n_heads + h)),
                  pl.BlockSpec((seq, hd), lambda h, b: (b, 2 * n_heads + h)),
                  pl.BlockSpec((ctx_len, hd), lambda h, b: (lat_ctx_blk + b, n_heads + h)),
                  pl.BlockSpec((ctx_len, hd), lambda h, b: (lat_ctx_blk + b, 2 * n_heads + h)),
                  pl.BlockSpec((None, 3, tq, tk), lambda h, b: (h, 0, 0, 0))],
        out_specs=pl.BlockSpec((seq, hd), lambda h, b: (b, h)),
        compiler_params=_cparams("parallel", "parallel"),
        name="na_attention",
    )(qkv, qkv, qkv, qkv, qkv, bias)


def _ffn_kernel(h_ref, halo_ref, wup_ref, cw_ref, cb_ref, wd_ref, x_ref, gate_ref, o_ref, *,
                tm, tf, n_lat_rows, seq, ctx_len):
    i = pl.program_id(0)
    j = pl.program_id(1)

    @pl.when(j == 0)
    def _():
        o_ref[...] = jnp.zeros_like(o_ref)

    ag = jnp.dot(h_ref[...], wup_ref[...], preferred_element_type=F32)
    a, g = ag[:, :tf], ag[:, tf:]
    g_halo = jnp.dot(halo_ref[...], wup_ref[:, tf:], preferred_element_type=F32)
    local = lax.broadcasted_iota(jnp.int32, (tm, 1), 0)
    grow = i * tm + local
    in_lat = grow < n_lat_rows
    pos = jnp.where(in_lat, grow % seq, (grow - n_lat_rows) % ctx_len)
    last = jnp.where(in_lat, seq - 1, ctx_len - 1)
    g_prev = jnp.where(local == 0, g_halo[0:1, :], pltpu.roll(g, 1, axis=0))
    g_prev = jnp.where(pos == 0, 0.0, g_prev)
    g_next = jnp.where(local == tm - 1, g_halo[1:2, :], pltpu.roll(g, tm - 1, axis=0))
    g_next = jnp.where(pos == last, 0.0, g_next)
    cw = cw_ref[...]
    gc = cw[0:1, :] * g_prev + cw[1:2, :] * g + cw[2:3, :] * g_next + cb_ref[...]
    act = (jax.nn.gelu(gc) * a).astype(BF16)
    o_ref[...] += jnp.dot(act, wd_ref[...], preferred_element_type=F32)

    @pl.when(j == pl.num_programs(1) - 1)
    def _():
        o_ref[...] = x_ref[...] + gate_ref[...] * o_ref[...]


def _conv_ffn(hs, xs, w_up_t, conv_w, conv_b, w_down, mods, gate_idx, seg, *, tm, tf, n_lat_rows,
              seq, ctx_len):
    r, d = hs.shape
    dff = w_down.shape[0]
    n_blk = r // tm
    prev_rows = hs[jnp.maximum(jnp.arange(n_blk) * tm - 1, 0)]
    next_rows = hs[jnp.minimum((jnp.arange(n_blk) + 1) * tm, r - 1)]
    halo = jnp.concatenate([prev_rows[:, None], next_rows[:, None],
                            jnp.zeros((n_blk, 14, d), hs.dtype)], axis=1)
    cw8 = jnp.concatenate([conv_w.astype(F32), jnp.zeros((5, dff), F32)], axis=0)
    return pl.pallas_call(
        functools.partial(_ffn_kernel, tm=tm, tf=tf, n_lat_rows=n_lat_rows, seq=seq, ctx_len=ctx_len),
        out_shape=jax.ShapeDtypeStruct((r, d), F32),
        grid=(n_blk, dff // tf),
        in_specs=[pl.BlockSpec((tm, d), lambda i, j: (i, 0), pipeline_mode=pl.Buffered(1)),
                  pl.BlockSpec((None, 16, d), lambda i, j: (i, 0, 0)),
                  pl.BlockSpec((d, 2 * tf), lambda i, j: (0, j)),
                  pl.BlockSpec((8, tf), lambda i, j: (0, j)),
                  pl.BlockSpec((1, tf), lambda i, j: (0, j)),
                  pl.BlockSpec((tf, d), lambda i, j: (j, 0)),
                  pl.BlockSpec((tm, d), lambda i, j: (i, 0), pipeline_mode=pl.Buffered(1)),
                  pl.BlockSpec((None, None, 1, d), lambda i, j: (seg(i), gate_idx, 0, 0))],
        out_specs=pl.BlockSpec((tm, d), lambda i, j: (i, 0)),
        compiler_params=_cparams("parallel", "arbitrary"),
        name="conv_ffn",
    )(hs, halo, w_up_t, cw8, conv_b.astype(F32).reshape(1, dff), w_down, xs, mods)


def _tile_up_weights(w_up, tf):
    d, two_dff = w_up.shape
    nj = two_dff // (2 * tf)
    return w_up.reshape(d, 2, nj, tf).transpose(0, 2, 1, 3).reshape(d, two_dff).astype(BF16)


def kernel(x, c, ctx, c_ctx, mod_w, mod_b, norm_mix_g, norm_ffn_g, ab_w_in, mlstm_gate_b, mlstm_head_g, s5_a_re, s5_a_im, s5_log_dt, s5_b_re, s5_b_im, s5_c_re, s5_c_im, s5_d, s5_glu_w, s5_glu_b, ab_w_out, na_w_qkv, na_rpb, na_w_out, ffn_w_up, ffn_conv_w, ffn_conv_b, ffn_w_down, final_norm_g):
    n_batch, seq, d = x.shape
    ctx_len = ctx.shape[1]
    depth = mod_w.shape[0]
    assert depth == 2 and mod_w.shape[2] == 6 * d
    n_lat_rows = n_batch * seq
    n_ctx_rows = n_batch * ctx_len
    assert seq % MLSTM_CHUNK == 0 and ctx_len == MLSTM_CHUNK and seq % GRID_W == 0

    n_heads = mlstm_gate_b.shape[-1]
    s5w = s5_d.shape[-1]
    mw = d - s5w
    dv = mw // n_heads
    dqk = dv // 2
    qkw = n_heads * dqk
    n_groups, p_state, s5_group = s5_b_re.shape[1:]
    assert S5_CHUNK * s5_group == 2 * LANE and p_state <= LANE and dqk % (2 * LANE) == 0
    na_heads = na_rpb.shape[1]
    hd = d // na_heads
    rows = seq // GRID_W
    kh = min((na_rpb.shape[2] + 1) // 2, rows)
    assert rows % NA_QROWS == 0 and rows >= NA_KROWS and hd == LANE
    dff = ffn_w_down.shape[1]

    tm_big = _pick(math.gcd(n_lat_rows, n_ctx_rows), (1024, 512, 256))
    tm_ffn = _pick(math.gcd(n_lat_rows, n_ctx_rows), (512, 256))
    tf = _pick(dff, (256, 128))
    seg_big = _seg_fn(tm_big, n_lat_rows, seq, n_batch)
    seg_ffn = _seg_fn(tm_ffn, n_lat_rows, seq, n_batch)
    seg_256 = _seg_fn(256, n_lat_rows, seq, n_batch)

    cond8 = jnp.concatenate([c, c_ctx[None], jnp.zeros((8 - n_batch - 1, d), F32)], axis=0)
    mods = _ada(cond8, mod_w, mod_b).reshape(depth, 8, 6, 1, d)

    xs = jnp.concatenate([x.reshape(n_lat_rows, d), ctx.reshape(n_ctx_rows, d)], axis=0)

    h = _modulate(xs, norm_mix_g[0], mods[0], 0, 1, seg_256, 256)
    w_in = ab_w_in[0]
    n_gate = 4 * n_heads
    w1 = w_in[:, :2 * qkw + 2 * mw].astype(BF16)
    w2 = jnp.concatenate([w_in[:, 2 * qkw + 2 * mw + n_gate:], w_in[:, 2 * qkw + 2 * mw:2 * qkw + 2 * mw + n_gate],
                          jnp.zeros((d, LANE - n_gate), F32)], axis=1).astype(BF16)
    p1 = _matmul(h, w1, F32, tm_big, _pick(w1.shape[1], (512, 256, 128)), "ab_in_proj")
    p2 = _matmul(h, w2, F32, tm_ffn, w2.shape[1], "ab_in_proj_s5")
    u = p2[:, :s5w]
    gates = p2[:, s5w:s5w + n_gate].reshape(-1, 2, 2, n_heads)
    gcol = gates.transpose(1, 3, 0, 2)
    grow = gates.transpose(1, 3, 2, 0)

    cos_t, sin_t = _rope_tables(seq, ctx_len, dqk)
    common = dict(n_batch=n_batch, seq=seq, n_heads=n_heads, dqk=dqk, dv=dv)
    hf = _mlstm_direction(p1, gcol, grow, mlstm_gate_b[0], cos_t, sin_t, None, None, reverse=False, **common)
    mix_a = _mlstm_direction(p1, gcol, grow, mlstm_gate_b[0], cos_t, sin_t, mlstm_head_g[0], hf,
                             reverse=True, **common)

    n_lat_ch, n_ctx_ch = seq // S5_CHUNK, ctx_len // S5_CHUNK
    u_lat = u[:n_lat_rows].reshape(n_batch, n_lat_ch, S5_CHUNK, n_groups, s5_group)
    u_ctx = u[n_lat_rows:].reshape(n_batch, n_ctx_ch, S5_CHUNK, n_groups, s5_group)
    u_flat = jnp.concatenate([u_lat, u_ctx], axis=1).transpose(1, 0, 3, 2, 4)
    u_flat = u_flat.reshape((n_lat_ch + n_ctx_ch) * n_batch, n_groups * S5_CHUNK * s5_group)
    tables = _s5_tables(s5_a_re[0], s5_a_im[0], s5_log_dt[0], s5_b_re[0], s5_b_im[0], s5_c_re[0],
                        s5_c_im[0], s5_d[0])
    y_flat = _s5_scan(u_flat, tables, n_batch, n_lat_ch, n_ctx_ch)
    y5 = y_flat.reshape(n_lat_ch + n_ctx_ch, n_batch, n_groups, S5_CHUNK, s5_group).transpose(1, 0, 3, 2, 4)
    y_rows = jnp.concatenate([y5[:, :n_lat_ch].reshape(n_lat_rows, s5w),
                              y5[:, n_lat_ch:].reshape(n_ctx_rows, s5w)], axis=0)
    mix_b = _glu(y_rows, s5_glu_w[0], s5_glu_b[0], tm_ffn)

    w_out = ab_w_out[0].astype(BF16)
    xs = _matmul_residual([mix_a, mix_b], [w_out[:mw], w_out[mw:]], xs, mods[0], 2, seg_big,
                          tm_big, 512, "ab_out_proj")
    h = _modulate(xs, norm_ffn_g[0], mods[0], 3, 4, seg_256, 256)
    xs = _conv_ffn(h, xs, _tile_up_weights(ffn_w_up[0], tf), ffn_conv_w[0], ffn_conv_b[0],
                   ffn_w_down[0].astype(BF16), mods[0], 5, seg_ffn, tm=tm_ffn, tf=tf,
                   n_lat_rows=n_lat_rows, seq=seq, ctx_len=ctx_len)

    h = _modulate(xs, norm_mix_g[1], mods[1], 0, 1, seg_256, 256)
    qkv = _matmul(h, na_w_qkv[0].astype(BF16), BF16, tm_big, 512, "na_qkv_proj")
    bias = _na_bias(na_rpb[0], rows)
    o = _na_attention(qkv, bias, n_batch=n_batch, seq=seq, ctx_len=ctx_len, n_heads=na_heads, hd=hd, kh=kh)
    xl = _matmul_residual([o], [na_w_out[0].astype(BF16)], xs, mods[1], 2, seg_big, tm_big, 512,
                          "na_out_proj")
    h = _modulate(xl, norm_ffn_g[1], mods[1], 3, 4, seg_256, 256)
    xl = _conv_ffn(h, xl, _tile_up_weights(ffn_w_up[1], tf), ffn_conv_w[1], ffn_conv_b[1],
                   ffn_w_down[1].astype(BF16), mods[1], 5, seg_ffn, tm=tm_ffn, tf=tf,
                   n_lat_rows=n_lat_rows, seq=seq, ctx_len=ctx_len)
    return _rmsnorm(xl, final_norm_g, 256).reshape(n_batch, seq, d)
```

```python
import functools
import math

import jax
import jax.numpy as jnp
import numpy as np
from jax import lax
from jax.experimental import pallas as pl
from jax.experimental.pallas import tpu as pltpu

F32 = jnp.float32
BF16 = jnp.bfloat16

EPS = 1e-6
ROPE_BASE = 10000.0
GRID_W = 64
MLSTM_CHUNK = 256
S5_CHUNK = 16
NA_QROWS = 8
NA_KROWS = 16
LANE = 128
VMEM_LIMIT_BYTES = 58 * 1024 * 1024


def _cparams(*sem):
    return pltpu.CompilerParams(dimension_semantics=sem, vmem_limit_bytes=VMEM_LIMIT_BYTES)


def _pick(n, candidates):
    for c in candidates:
        if n % c == 0:
            return c
    raise ValueError(f"no tile in {candidates} divides {n}")


def _ada_kernel(c_ref, w_ref, b_ref, o_ref):
    s = jax.nn.silu(c_ref[...]).astype(BF16)
    o_ref[...] = jnp.dot(s, w_ref[...].astype(BF16), preferred_element_type=F32) + b_ref[...]


def _ada(cond8, mod_w, mod_b):
    depth, d, n6 = mod_w.shape
    tn = _pick(n6, (512, 256, 128))
    return pl.pallas_call(
        _ada_kernel,
        out_shape=jax.ShapeDtypeStruct((depth, 8, n6), F32),
        grid=(depth, n6 // tn),
        in_specs=[pl.BlockSpec((8, d), lambda l, j: (0, 0)),
                  pl.BlockSpec((None, d, tn), lambda l, j: (l, 0, j)),
                  pl.BlockSpec((None, 1, tn), lambda l, j: (l, 0, j))],
        out_specs=pl.BlockSpec((None, 8, tn), lambda l, j: (l, 0, j)),
        compiler_params=_cparams("parallel", "parallel"),
        name="ada_mod",
    )(cond8, mod_w, mod_b.reshape(depth, 1, n6))


def _modulate_kernel(x_ref, g_ref, sh_ref, sc_ref, o_ref):
    x = x_ref[...]
    y = x * lax.rsqrt(jnp.mean(x * x, axis=-1, keepdims=True) + EPS) * g_ref[...]
    o_ref[...] = (y * (1.0 + sc_ref[...]) + sh_ref[...]).astype(o_ref.dtype)


def _seg_fn(tm, n_lat_rows, seq, n_batch):
    n_lat_blk = n_lat_rows // tm

    def seg(i):
        return jnp.where(i < n_lat_blk, (i * tm) // seq, n_batch)
    return seg


def _modulate(xs, gain, mods, shift_idx, scale_idx, seg, tm):
    r, d = xs.shape
    return pl.pallas_call(
        _modulate_kernel,
        out_shape=jax.ShapeDtypeStruct((r, d), BF16),
        grid=(r // tm,),
        in_specs=[pl.BlockSpec((tm, d), lambda i: (i, 0)),
                  pl.BlockSpec((1, d), lambda i: (0, 0)),
                  pl.BlockSpec((None, None, 1, d), lambda i: (seg(i), shift_idx, 0, 0)),
                  pl.BlockSpec((None, None, 1, d), lambda i: (seg(i), scale_idx, 0, 0))],
        out_specs=pl.BlockSpec((tm, d), lambda i: (i, 0)),
        compiler_params=_cparams("parallel"),
        name="modulate",
    )(xs, gain.reshape(1, d), mods, mods)


def _rmsnorm_kernel(x_ref, g_ref, o_ref):
    x = x_ref[...]
    o_ref[...] = x * lax.rsqrt(jnp.mean(x * x, axis=-1, keepdims=True) + EPS) * g_ref[...]


def _rmsnorm(xs, gain, tm):
    r, d = xs.shape
    return pl.pallas_call(
        _rmsnorm_kernel,
        out_shape=jax.ShapeDtypeStruct((r, d), F32),
        grid=(r // tm,),
        in_specs=[pl.BlockSpec((tm, d), lambda i: (i, 0)),
                  pl.BlockSpec((1, d), lambda i: (0, 0))],
        out_specs=pl.BlockSpec((tm, d), lambda i: (i, 0)),
        compiler_params=_cparams("parallel"),
        name="final_norm",
    )(xs, gain.reshape(1, d))


def _mm_kernel(a_ref, b_ref, o_ref):
    o_ref[...] = jnp.dot(a_ref[...], b_ref[...], preferred_element_type=F32).astype(o_ref.dtype)


def _matmul(a, b, out_dtype, tm, tn, name):
    m, k = a.shape
    n = b.shape[1]
    return pl.pallas_call(
        _mm_kernel,
        out_shape=jax.ShapeDtypeStruct((m, n), out_dtype),
        grid=(m // tm, n // tn),
        in_specs=[pl.BlockSpec((tm, k), lambda i, j: (i, 0)),
                  pl.BlockSpec((k, tn), lambda i, j: (0, j))],
        out_specs=pl.BlockSpec((tm, tn), lambda i, j: (i, j)),
        compiler_params=_cparams("parallel", "parallel"),
        name=name,
    )(a, b)


def _mm_res_kernel(*refs, n_a):
    a_refs, b_refs = refs[:n_a], refs[n_a:2 * n_a]
    x_ref, gate_ref, o_ref = refs[2 * n_a:]
    acc = jnp.dot(a_refs[0][...], b_refs[0][...], preferred_element_type=F32)
    for a_ref, b_ref in zip(a_refs[1:], b_refs[1:]):
        acc = acc + jnp.dot(a_ref[...], b_ref[...], preferred_element_type=F32)
    o_ref[...] = x_ref[...] + gate_ref[...] * acc


def _matmul_residual(a_list, b_list, xs, mods, gate_idx, seg, tm, tn, name):
    m = a_list[0].shape[0]
    n = b_list[0].shape[1]
    in_specs = ([pl.BlockSpec((tm, a.shape[1]), lambda i, j: (i, 0)) for a in a_list]
                + [pl.BlockSpec((b.shape[0], tn), lambda i, j: (0, j)) for b in b_list]
                + [pl.BlockSpec((tm, tn), lambda i, j: (i, j)),
                   pl.BlockSpec((None, None, 1, tn), lambda i, j: (seg(i), gate_idx, 0, j))])
    return pl.pallas_call(
        functools.partial(_mm_res_kernel, n_a=len(a_list)),
        out_shape=jax.ShapeDtypeStruct((m, n), F32),
        grid=(m // tm, n // tn),
        in_specs=in_specs,
        out_specs=pl.BlockSpec((tm, tn), lambda i, j: (i, j)),
        compiler_params=_cparams("parallel", "parallel"),
        name=name,
    )(*a_list, *b_list, xs, mods)


def _rope(x, cos, sin):
    parts = []
    for j in range(x.shape[1] // LANE):
        sl = slice(j * LANE, (j + 1) * LANE)
        xs = x[:, sl]
        parts.append(xs * cos[:, sl] + pltpu.roll(xs, LANE // 2, axis=1) * sin[:, sl])
    return jnp.concatenate(parts, axis=1)


def _mlstm_kernel(gb_ref, q_ref, k_ref, v_ref, cos_ref, sin_ref, gcol_ref, grow_ref, *rest,
                  reverse, n_heads, scale):
    if reverse:
        o_ref, hf_ref, hg_ref, out_ref, ct_ref, m_ref = rest
    else:
        out_ref, ct_ref, m_ref = rest
    hh = pl.program_id(1)
    step = pl.program_id(2)
    d = 1 if reverse else 0
    L, dv = v_ref.shape

    @pl.when(step == 0)
    def _():
        ct_ref[...] = jnp.zeros_like(ct_ref)
        m_ref[...] = jnp.zeros_like(m_ref)

    b_i = gb_ref[(d * 2 + 0) * n_heads + hh]
    b_f = gb_ref[(d * 2 + 1) * n_heads + hh]
    li_col = gcol_ref[:, 0:1] + b_i
    lf_col = jax.nn.log_sigmoid(gcol_ref[:, 1:2] + b_f)
    li_row = grow_ref[0:1, :] + b_i
    lf_row = jax.nn.log_sigmoid(grow_ref[1:2, :] + b_f)

    row = lax.broadcasted_iota(jnp.int32, (L, L), 0)
    col = lax.broadcasted_iota(jnp.int32, (L, L), 1)
    incl = (col >= row) if reverse else (col <= row)
    incl_t = (row >= col) if reverse else (row <= col)
    cum_col = jnp.sum(jnp.where(incl, lf_row, 0.0), axis=1, keepdims=True)
    cum_row = jnp.sum(jnp.where(incl_t, lf_col, 0.0), axis=0, keepdims=True)
    total = jnp.sum(lf_row, axis=1, keepdims=True)
    m_prev = m_ref[0:1, 0:1]

    dmat = jnp.where(incl, cum_col - cum_row + li_row, -jnp.inf)
    carried = cum_col + m_prev
    m_loc = jnp.maximum(carried, jnp.max(dmat, axis=1, keepdims=True))
    w = jnp.exp(dmat - m_loc)
    w_state = jnp.exp(carried - m_loc)

    cos = cos_ref[...]
    sin = sin_ref[...]
    qb = (_rope(q_ref[...], cos, sin) * scale).astype(BF16)
    kb = _rope(k_ref[...], cos, sin).astype(BF16)
    s_qk = lax.dot_general(qb, kb, (((1,), (1,)), ((), ())), preferred_element_type=F32)
    sc = (s_qk * w).astype(BF16)
    vext = jnp.concatenate([v_ref[...], jnp.ones((L, LANE), F32)], axis=1)
    ct = ct_ref[...]
    res = (jnp.dot(sc, vext.astype(BF16), preferred_element_type=F32)
           + w_state * jnp.dot(qb, ct.astype(BF16), preferred_element_type=F32))
    den = res[:, dv:dv + 1]
    hval = res[:, :dv] / jnp.maximum(jnp.abs(den), jnp.exp(-m_loc))

    src = total - cum_col + li_col
    m_new = jnp.maximum(total + m_prev, jnp.max(src, axis=0, keepdims=True))
    decay = jnp.exp(total + m_prev - m_new)
    w_src = jnp.exp(src - m_new)
    upd = lax.dot_general(kb, (w_src * vext).astype(BF16), (((0,), (0,)), ((), ())),
                          preferred_element_type=F32)
    ct_ref[...] = decay * ct + upd
    m_ref[...] = jnp.broadcast_to(m_new, m_ref.shape)

    if reverse:
        hm = hf_ref[...] + hval
        hn = hm * lax.rsqrt(jnp.mean(hm * hm, axis=-1, keepdims=True) + EPS) * hg_ref[...]
        out_ref[...] = (jax.nn.sigmoid(o_ref[...]) * hn).astype(out_ref.dtype)
    else:
        out_ref[...] = hval


def _mlstm_direction(p1, gcol, grow, gate_b, cos_t, sin_t, head_g, hf, *, reverse, n_batch, seq,
                     n_heads, dqk, dv):
    r = p1.shape[0]
    L = MLSTM_CHUNK
    n_lat = seq // L
    lat_blocks = n_batch * n_lat
    d = 1 if reverse else 0

    def blk(b, s):
        lat = b * n_lat + ((n_lat - s) if reverse else (s - 1))
        return jnp.where(s == 0, lat_blocks + b, lat)

    def tab(s):
        return jnp.where(s == 0, n_lat, (n_lat - s) if reverse else (s - 1))

    in_specs = [
        pl.BlockSpec(memory_space=pltpu.SMEM),
        pl.BlockSpec((L, dqk), lambda b, h, s: (blk(b, s), h)),
        pl.BlockSpec((L, dqk), lambda b, h, s: (blk(b, s), n_heads + h)),
        pl.BlockSpec((L, dv), lambda b, h, s: (blk(b, s), n_heads + h)),
        pl.BlockSpec((L, dqk), lambda b, h, s: (tab(s), 0)),
        pl.BlockSpec((L, dqk), lambda b, h, s: (tab(s), 0)),
        pl.BlockSpec((None, None, L, 2), lambda b, h, s: (d, h, blk(b, s), 0)),
        pl.BlockSpec((None, None, 2, L), lambda b, h, s: (d, h, 0, blk(b, s))),
    ]
    args = [gate_b.reshape(-1), p1, p1, p1, cos_t, sin_t, gcol, grow]
    if reverse:
        in_specs += [
            pl.BlockSpec((L, dv), lambda b, h, s: (blk(b, s), 2 * n_heads + h)),
            pl.BlockSpec((L, dv), lambda b, h, s: (blk(b, s), h)),
            pl.BlockSpec((None, 1, dv), lambda b, h, s: (h, 0, 0)),
        ]
        args += [p1, hf, head_g.reshape(n_heads, 1, dv)]
    return pl.pallas_call(
        functools.partial(_mlstm_kernel, reverse=reverse, n_heads=n_heads, scale=dqk ** -0.5),
        out_shape=jax.ShapeDtypeStruct((r, n_heads * dv), BF16 if reverse else F32),
        grid=(n_batch, n_heads, n_lat + 1),
        in_specs=in_specs,
        out_specs=pl.BlockSpec((L, dv), lambda b, h, s: (blk(b, s), h)),
        scratch_shapes=[pltpu.VMEM((dqk, dv + LANE), F32), pltpu.VMEM((8, LANE), F32)],
        compiler_params=_cparams("parallel", "parallel", "arbitrary"),
        name="mlstm_bwd" if reverse else "mlstm_fwd",
    )(*args)


def _rope_tables(seq, ctx_len, dqk):
    half = dqk // 2
    inv = ROPE_BASE ** (-jnp.arange(0, half, 2, dtype=F32) / half)
    t = jnp.arange(seq)

    def one(pos):
        ang = pos.astype(F32)[:, None] * inv
        c, s = jnp.cos(ang), jnp.sin(ang)
        return jnp.concatenate([c, c], axis=-1), jnp.concatenate([-s, s], axis=-1)

    c_r, s_r = one(t // GRID_W)
    c_c, s_c = one(t % GRID_W)
    cos_t = jnp.concatenate([c_r, c_c], axis=-1)
    sin_t = jnp.concatenate([s_r, s_c], axis=-1)
    cos_t = jnp.concatenate([cos_t, jnp.ones((ctx_len, dqk), F32)], axis=0)
    sin_t = jnp.concatenate([sin_t, jnp.zeros((ctx_len, dqk), F32)], axis=0)
    return cos_t, sin_t


def _s5_tables(a_re, a_im, log_dt, b_re, b_im, c_re, c_im, d_skip):
    g_, p_, hsz = b_re.shape
    lc = S5_CHUNK
    bmat = lax.complex(b_re.astype(F32), b_im.astype(F32))
    cmat = lax.complex(c_re.astype(F32), c_im.astype(F32))
    tau = jnp.arange(lc + 1, dtype=F32)
    s_idx = jnp.arange(lc)
    m_tot = 0.0
    w_parts, v_parts, a_parts = [], [], []
    for d in range(2):
        lam = lax.complex(a_re[d].astype(F32), a_im[d].astype(F32))
        dt = jnp.exp(log_dt[d].astype(F32))[:, None]
        fac = (jnp.exp(lam * dt) - 1) / lam
        apow = jnp.exp(tau[:, None, None] * (lam * dt)[None])
        bf = bmat * fac[:, :, None]
        cb = cmat.transpose(0, 2, 1)[:, :, :, None] * bf[:, :, None, :]
        kt = jnp.real(jnp.einsum('tgp,gpij->tgij', apow[:lc], cb,
                                 precision=lax.Precision.HIGHEST))
        lag = (s_idx[None, :] - s_idx[:, None]) if d == 0 else (s_idx[:, None] - s_idx[None, :])
        ok = lag >= 0
        kk = kt[jnp.clip(lag, 0, lc - 1)]
        kk = jnp.where(ok[:, :, None, None, None], kk, 0.0)
        m_tot = m_tot + kk.transpose(2, 0, 4, 1, 3).reshape(g_, lc * hsz, lc * hsz)
        pw = apow[(lc - 1 - s_idx) if d == 0 else s_idx]
        wc = pw[:, :, :, None] * bf[None]
        wc = wc.transpose(1, 0, 3, 2).reshape(g_, lc * hsz, p_)
        pv = apow[(s_idx + 1) if d == 0 else (lc - s_idx)]
        vc = cmat[None] * pv[:, :, None, :]
        vc = vc.transpose(1, 3, 0, 2).reshape(g_, p_, lc * hsz)
        pad_w = jnp.zeros((g_, lc * hsz, LANE - p_), F32)
        pad_v = jnp.zeros((g_, LANE - p_, lc * hsz), F32)
        w_parts += [jnp.real(wc), pad_w, jnp.imag(wc), pad_w]
        v_parts += [jnp.real(vc), pad_v, -jnp.imag(vc), pad_v]
        a_chunk = apow[lc]
        pad_a = jnp.zeros((g_, LANE - p_), F32)
        a_parts += [jnp.concatenate([jnp.real(a_chunk), pad_a], axis=1),
                    jnp.concatenate([jnp.imag(a_chunk), pad_a], axis=1)]
    w_all = jnp.concatenate(w_parts, axis=2)
    v_all = jnp.concatenate(v_parts, axis=1)
    a_all = jnp.stack(a_parts + [jnp.zeros_like(a_parts[0])] * 4, axis=1)
    d_flat = jnp.tile(d_skip.astype(F32).reshape(g_, 1, hsz), (1, 1, lc))
    return m_tot, w_all, v_all, a_all, d_flat


def _s5_kernel(u_ref, m_ref, w_ref, v_ref, a_ref, d_ref, y_ref, z_ref, sp_ref, *, nb, n_lat, n_ctx):
    hi = lax.Precision.HIGHEST
    u = u_ref[...]
    z_ref[...] = jnp.dot(u, w_ref[...], preferred_element_type=F32, precision=hi)
    n_chunks = n_lat + n_ctx
    a = a_ref[...]
    coef = [(a[2 * d:2 * d + 1, :], a[2 * d + 1:2 * d + 2, :]) for d in range(2)]
    state = [(jnp.zeros((nb, LANE), F32), jnp.zeros((nb, LANE), F32)) for _ in range(2)]
    for t in range(n_chunks):
        chunk = ((t + n_lat) % n_chunks, n_chunks - 1 - t)
        for d in range(2):
            r0 = chunk[d] * nb
            c0 = d * 2 * LANE
            s_re, s_im = state[d]
            sp_ref[r0:r0 + nb, c0:c0 + LANE] = s_re
            sp_ref[r0:r0 + nb, c0 + LANE:c0 + 2 * LANE] = s_im
            a_r, a_i = coef[d]
            z_re = z_ref[r0:r0 + nb, c0:c0 + LANE]
            z_im = z_ref[r0:r0 + nb, c0 + LANE:c0 + 2 * LANE]
            state[d] = (a_r * s_re - a_i * s_im + z_re, a_r * s_im + a_i * s_re + z_im)
    y = (jnp.dot(u, m_ref[...], preferred_element_type=F32, precision=hi)
         + jnp.dot(sp_ref[...], v_ref[...], preferred_element_type=F32, precision=hi)
         + d_ref[...] * u)
    y_ref[...] = y


def _s5_scan(u_flat, tables, nb, n_lat, n_ctx):
    m_tot, w_all, v_all, a_all, d_flat = tables
    g_ = m_tot.shape[0]
    rows, width = u_flat.shape
    cw = width // g_
    return pl.pallas_call(
        functools.partial(_s5_kernel, nb=nb, n_lat=n_lat, n_ctx=n_ctx),
        out_shape=jax.ShapeDtypeStruct((rows, width), F32),
        grid=(g_,),
        in_specs=[pl.BlockSpec((rows, cw), lambda g: (0, g)),
                  pl.BlockSpec((None, cw, cw), lambda g: (g, 0, 0)),
                  pl.BlockSpec((None, cw, 4 * LANE), lambda g: (g, 0, 0)),
                  pl.BlockSpec((None, 4 * LANE, cw), lambda g: (g, 0, 0)),
                  pl.BlockSpec((None, 8, LANE), lambda g: (g, 0, 0)),
                  pl.BlockSpec((None, 1, cw), lambda g: (g, 0, 0))],
        out_specs=pl.BlockSpec((rows, cw), lambda g: (0, g)),
        scratch_shapes=[pltpu.VMEM((rows, 4 * LANE), F32), pltpu.VMEM((rows, 4 * LANE), F32)],
        compiler_params=_cparams("parallel"),
        name="s5_scan",
    )(u_flat, m_tot, w_all, v_all, a_all, d_flat)


def _glu_kernel(y_ref, w_ref, b_ref, o_ref):
    g = jax.nn.gelu(y_ref[...])
    z = jnp.dot(g.astype(BF16), w_ref[...], preferred_element_type=F32) + b_ref[...]
    o_ref[...] = (g * jax.nn.sigmoid(z)).astype(o_ref.dtype)


def _glu(y, glu_w, glu_b, tm):
    r, w = y.shape
    return pl.pallas_call(
        _glu_kernel,
        out_shape=jax.ShapeDtypeStruct((r, w), BF16),
        grid=(r // tm,),
        in_specs=[pl.BlockSpec((tm, w), lambda i: (i, 0)),
                  pl.BlockSpec((w, w), lambda i: (0, 0)),
                  pl.BlockSpec((1, w), lambda i: (0, 0))],
        out_specs=pl.BlockSpec((tm, w), lambda i: (i, 0)),
        compiler_params=_cparams("parallel"),
        name="s5_glu",
    )(y, glu_w.astype(BF16), glu_b.astype(F32).reshape(1, w))


def _na_variants(rows, kh):
    n_blk = rows // NA_QROWS
    out = []
    for i in (0, min(1, n_blk - 1), n_blk - 1):
        out.append((i, min(max(i * NA_QROWS - kh // 2, 0), rows - NA_KROWS)))
    return out


def _na_tables(rpb, rows):
    n_heads, nri, nci = rpb.shape
    win_h, win_w = (nri + 1) // 2, (nci + 1) // 2
    kh = min(win_h, rows)
    col = np.arange(GRID_W)
    col_start = np.clip(col - win_w // 2, 0, GRID_W - win_w)
    col_ok = (col[None, :] >= col_start[:, None]) & (col[None, :] < col_start[:, None] + win_w)
    ci = np.clip(col[None, :] - col[:, None], -(win_w - 1), win_w - 1) + (win_w - 1)
    onehot = (ci[None] == np.arange(nci)[:, None, None]).astype(np.float32)
    t1 = jnp.einsum('hrc,cqk->hrqk', rpb.astype(F32), onehot, precision=lax.Precision.HIGHEST)
    t1 = jnp.where(col_ok[None, None], t1, -jnp.inf)
    dummy = jnp.full((n_heads, 1, GRID_W, GRID_W), -jnp.inf, F32)
    t1 = jnp.concatenate([dummy, t1, dummy], axis=1)
    t2 = jnp.concatenate([t1[:, :nri + 1], t1[:, 1:nri + 2]], axis=-1)
    rmask = np.zeros((3, NA_QROWS, NA_KROWS // 2, 1, 2 * GRID_W), np.float32)
    for v, (i, ks) in enumerate(_na_variants(rows, kh)):
        for a in range(NA_QROWS):
            r_start = min(max(i * NA_QROWS + a - kh // 2, 0), rows - kh)
            for b in range(NA_KROWS):
                if not (r_start <= ks + b < r_start + kh):
                    rmask[v, a, b // 2, 0, (b % 2) * GRID_W:(b % 2 + 1) * GRID_W] = -np.inf
    return t2, jnp.asarray(rmask)


def _na_kernel(q_ref, k_ref, v_ref, kc_ref, vc_ref, t2_ref, rm_ref, o_ref, bias_ref, *, rows, kh, win_h,
               scale):
    n_blk = rows // NA_QROWS
    tq = NA_QROWS * GRID_W
    tk = NA_KROWS * GRID_W
    n_r = t2_ref.shape[0]

    @pl.when(pl.program_id(1) == 0)
    def _():
        for v, (i, ks) in enumerate(_na_variants(rows, kh)):
            for a in range(NA_QROWS):
                for pair in range(NA_KROWS // 2):
                    r = (ks + 2 * pair) - (i * NA_QROWS + a) + (win_h - 1)
                    tile = t2_ref[min(max(r + 1, 0), n_r - 1)] + rm_ref[v, a, pair]
                    bias_ref[v, a * GRID_W:(a + 1) * GRID_W, pair * 2 * GRID_W:(pair + 1) * 2 * GRID_W] = tile

    kc = kc_ref[...]
    vc = vc_ref[...]
    nt = (((1,), (1,)), ((), ()))

    def body(i, carry):
        q0 = pl.multiple_of(i * tq, tq)
        ks = jnp.clip(i * NA_QROWS - kh // 2, 0, rows - NA_KROWS)
        k0 = pl.multiple_of(ks * GRID_W, 4 * GRID_W)
        variant = jnp.where(i == 0, 0, jnp.where(i == n_blk - 1, 2, 1))
        qb = q_ref[pl.ds(q0, tq), :]
        kb = k_ref[pl.ds(k0, tk), :]
        vb = v_ref[pl.ds(k0, tk), :]
        s_loc = lax.dot_general(qb, kb, nt, preferred_element_type=F32) * scale + bias_ref[variant]
        s_ctx = lax.dot_general(qb, kc, nt, preferred_element_type=F32) * scale
        m = jnp.maximum(jnp.max(s_loc, axis=1, keepdims=True), jnp.max(s_ctx, axis=1, keepdims=True))
        p_loc = jnp.exp(s_loc - m)
        p_ctx = jnp.exp(s_ctx - m)
        denom = jnp.sum(p_loc, axis=1, keepdims=True) + jnp.sum(p_ctx, axis=1, keepdims=True)
        acc = (jnp.dot(p_loc.astype(BF16), vb, preferred_element_type=F32)
               + jnp.dot(p_ctx.astype(BF16), vc, preferred_element_type=F32))
        o_ref[pl.ds(q0, tq), :] = (acc / denom).astype(o_ref.dtype)
        return carry

    lax.fori_loop(0, n_blk, body, 0)


def _na_attention(qkv, t2, rmask, *, n_batch, seq, ctx_len, n_heads, hd, kh, win_h):
    rows = seq // GRID_W
    lat_ctx_blk = (n_batch * seq) // ctx_len
    tq, tk = NA_QROWS * GRID_W, NA_KROWS * GRID_W
    return pl.pallas_call(
        functools.partial(_na_kernel, rows=rows, kh=kh, win_h=win_h, scale=hd ** -0.5),
        out_shape=jax.ShapeDtypeStruct((n_batch * seq, n_heads * hd), BF16),
        grid=(n_heads, n_batch),
        in_specs=[pl.BlockSpec((seq, hd), lambda h, b: (b, h)),
                  pl.BlockSpec((seq, hd), lambda h, b: (b, n_heads + h)),
                  pl.BlockSpec((seq, hd), lambda h, b: (b, 2 * n_heads + h)),
                  pl.BlockSpec((ctx_len, hd), lambda h, b: (lat_ctx_blk + b, n_heads + h)),
                  pl.BlockSpec((ctx_len, hd), lambda h, b: (lat_ctx_blk + b, 2 * n_heads + h)),
                  pl.BlockSpec((None,) + t2.shape[1:], lambda h, b: (h, 0, 0, 0)),
                  pl.BlockSpec(rmask.shape, lambda h, b: (0, 0, 0, 0, 0))],
        out_specs=pl.BlockSpec((seq, hd), lambda h, b: (b, h)),
        scratch_shapes=[pltpu.VMEM((3, tq, tk), F32)],
        compiler_params=_cparams("arbitrary", "arbitrary"),
        name="na_attention",
    )(qkv, qkv, qkv, qkv, qkv, t2, rmask)


def _ffn_kernel(hx_ref, wg_ref, wa_ref, cw_ref, cb_ref, wd_ref, x_ref, gate_ref, o_ref, *,
                tm, n_lat_rows, seq, ctx_len):
    i = pl.program_id(0)
    j = pl.program_id(1)

    @pl.when(j == 0)
    def _():
        o_ref[...] = jnp.zeros_like(o_ref)

    g_all = jnp.dot(hx_ref[...], wg_ref[...], preferred_element_type=F32)
    g = g_all[:tm]
    local = lax.broadcasted_iota(jnp.int32, (tm, 1), 0)
    grow = i * tm + local
    in_lat = grow < n_lat_rows
    pos = jnp.where(in_lat, grow % seq, (grow - n_lat_rows) % ctx_len)
    last = jnp.where(in_lat, seq - 1, ctx_len - 1)
    g_prev = jnp.where(local == 0, g_all[tm:tm + 1, :], pltpu.roll(g, 1, axis=0))
    g_prev = jnp.where(pos == 0, 0.0, g_prev)
    g_next = jnp.where(local == tm - 1, g_all[tm + 1:tm + 2, :], pltpu.roll(g, tm - 1, axis=0))
    g_next = jnp.where(pos == last, 0.0, g_next)
    cw = cw_ref[...]
    gl = jax.nn.gelu(cw[0:1, :] * g_prev + cw[1:2, :] * g + cw[2:3, :] * g_next + cb_ref[...])
    a = jnp.dot(hx_ref[:tm, :], wa_ref[...], preferred_element_type=F32)
    o_ref[...] += jnp.dot((gl * a).astype(BF16), wd_ref[...], preferred_element_type=F32)

    @pl.when(j == pl.num_programs(1) - 1)
    def _():
        o_ref[...] = x_ref[...] + gate_ref[...] * o_ref[...]


def _conv_ffn(hs, xs, weights, mods, gate_idx, seg, *, tm, tf, n_lat_rows, seq, ctx_len):
    w_up, cw8, cb, w_down = weights
    r, d = hs.shape
    dffp = w_down.shape[0]
    n_blk, nj = r // tm, dffp // tf
    prev_rows = hs[jnp.maximum(jnp.arange(n_blk) * tm - 1, 0)]
    next_rows = hs[jnp.minimum((jnp.arange(n_blk) + 1) * tm, r - 1)]
    hx = jnp.concatenate([hs.reshape(n_blk, tm, d), prev_rows[:, None], next_rows[:, None],
                          jnp.zeros((n_blk, 14, d), hs.dtype)], axis=1)
    once = pl.Buffered(1)
    return pl.pallas_call(
        functools.partial(_ffn_kernel, tm=tm, n_lat_rows=n_lat_rows, seq=seq, ctx_len=ctx_len),
        out_shape=jax.ShapeDtypeStruct((r, d), F32),
        grid=(n_blk, nj),
        in_specs=[pl.BlockSpec((None, tm + 16, d), lambda i, j: (i, 0, 0), pipeline_mode=once),
                  pl.BlockSpec((d, tf), lambda i, j: (0, nj + j)),
                  pl.BlockSpec((d, tf), lambda i, j: (0, j)),
                  pl.BlockSpec((8, tf), lambda i, j: (0, j)),
                  pl.BlockSpec((1, tf), lambda i, j: (0, j)),
                  pl.BlockSpec((tf, d), lambda i, j: (j, 0)),
                  pl.BlockSpec((tm, d), lambda i, j: (i, 0), pipeline_mode=once),
                  pl.BlockSpec((None, None, 1, d), lambda i, j: (seg(i), gate_idx, 0, 0))],
        out_specs=pl.BlockSpec((tm, d), lambda i, j: (i, 0), pipeline_mode=once),
        compiler_params=_cparams("parallel", "arbitrary"),
        name="conv_ffn",
    )(hx, w_up, w_up, cw8, cb, w_down, xs, mods)


def _pad_ffn_weights(w_up, conv_w, conv_b, w_down, tf):
    d = w_up.shape[0]
    dff = w_down.shape[0]
    pad = -dff % tf
    w_up_p = jnp.pad(w_up.reshape(d, 2, dff), ((0, 0), (0, 0), (0, pad))).reshape(d, 2 * (dff + pad))
    cw8 = jnp.pad(conv_w.astype(F32), ((0, 8 - conv_w.shape[0]), (0, pad)))
    cb = jnp.pad(conv_b.astype(F32), (0, pad)).reshape(1, dff + pad)
    w_down_p = jnp.pad(w_down, ((0, pad), (0, 0)))
    return w_up_p.astype(BF16), cw8, cb, w_down_p.astype(BF16)


def kernel(x, c, ctx, c_ctx, mod_w, mod_b, norm_mix_g, norm_ffn_g, ab_w_in, mlstm_gate_b, mlstm_head_g, s5_a_re, s5_a_im, s5_log_dt, s5_b_re, s5_b_im, s5_c_re, s5_c_im, s5_d, s5_glu_w, s5_glu_b, ab_w_out, na_w_qkv, na_rpb, na_w_out, ffn_w_up, ffn_conv_w, ffn_conv_b, ffn_w_down, final_norm_g):
    n_batch, seq, d = x.shape
    ctx_len = ctx.shape[1]
    depth = mod_w.shape[0]
    assert depth == 2 and mod_w.shape[2] == 6 * d
    n_lat_rows = n_batch * seq
    n_ctx_rows = n_batch * ctx_len
    assert seq % MLSTM_CHUNK == 0 and ctx_len == MLSTM_CHUNK and seq % GRID_W == 0

    n_heads = mlstm_gate_b.shape[-1]
    s5w = s5_d.shape[-1]
    mw = d - s5w
    dv = mw // n_heads
    dqk = dv // 2
    qkw = n_heads * dqk
    n_groups, p_state, s5_group = s5_b_re.shape[1:]
    assert S5_CHUNK * s5_group == 2 * LANE and p_state <= LANE and dqk % (2 * LANE) == 0
    na_heads = na_rpb.shape[1]
    hd = d // na_heads
    rows = seq // GRID_W
    kh = min((na_rpb.shape[2] + 1) // 2, rows)
    assert rows % NA_QROWS == 0 and rows >= NA_KROWS and hd == LANE
    dff = ffn_w_down.shape[1]

    tm_big = _pick(math.gcd(n_lat_rows, n_ctx_rows), (1024, 512, 256))
    tm_ffn = _pick(math.gcd(n_lat_rows, n_ctx_rows), (512, 256))
    tf = 512
    seg_big = _seg_fn(tm_big, n_lat_rows, seq, n_batch)
    seg_ffn = _seg_fn(tm_ffn, n_lat_rows, seq, n_batch)
    seg_256 = _seg_fn(256, n_lat_rows, seq, n_batch)

    cond8 = jnp.concatenate([c, c_ctx[None], jnp.zeros((8 - n_batch - 1, d), F32)], axis=0)
    mods = _ada(cond8, mod_w, mod_b).reshape(depth, 8, 6, 1, d)

    xs = jnp.concatenate([x.reshape(n_lat_rows, d), ctx.reshape(n_ctx_rows, d)], axis=0)

    h = _modulate(xs, norm_mix_g[0], mods[0], 0, 1, seg_256, 256)
    w_in = ab_w_in[0]
    n_gate = 4 * n_heads
    w1 = w_in[:, :2 * qkw + 2 * mw].astype(BF16)
    w2 = jnp.concatenate([w_in[:, 2 * qkw + 2 * mw + n_gate:], w_in[:, 2 * qkw + 2 * mw:2 * qkw + 2 * mw + n_gate],
                          jnp.zeros((d, LANE - n_gate), F32)], axis=1).astype(BF16)
    p1 = _matmul(h, w1, F32, tm_big, _pick(w1.shape[1], (512, 256, 128)), "ab_in_proj")
    p2 = _matmul(h, w2, F32, tm_ffn, w2.shape[1], "ab_in_proj_s5")
    u = p2[:, :s5w]
    gates = p2[:, s5w:s5w + n_gate].reshape(-1, 2, 2, n_heads)
    gcol = gates.transpose(1, 3, 0, 2)
    grow = gates.transpose(1, 3, 2, 0)

    cos_t, sin_t = _rope_tables(seq, ctx_len, dqk)
    common = dict(n_batch=n_batch, seq=seq, n_heads=n_heads, dqk=dqk, dv=dv)
    hf = _mlstm_direction(p1, gcol, grow, mlstm_gate_b[0], cos_t, sin_t, None, None, reverse=False, **common)
    mix_a = _mlstm_direction(p1, gcol, grow, mlstm_gate_b[0], cos_t, sin_t, mlstm_head_g[0], hf,
                             reverse=True, **common)

    n_lat_ch, n_ctx_ch = seq // S5_CHUNK, ctx_len // S5_CHUNK
    u_lat = u[:n_lat_rows].reshape(n_batch, n_lat_ch, S5_CHUNK, n_groups, s5_group)
    u_ctx = u[n_lat_rows:].reshape(n_batch, n_ctx_ch, S5_CHUNK, n_groups, s5_group)
    u_flat = jnp.concatenate([u_lat, u_ctx], axis=1).transpose(1, 0, 3, 2, 4)
    u_flat = u_flat.reshape((n_lat_ch + n_ctx_ch) * n_batch, n_groups * S5_CHUNK * s5_group)
    tables = _s5_tables(s5_a_re[0], s5_a_im[0], s5_log_dt[0], s5_b_re[0], s5_b_im[0], s5_c_re[0],
                        s5_c_im[0], s5_d[0])
    y_flat = _s5_scan(u_flat, tables, n_batch, n_lat_ch, n_ctx_ch)
    y5 = y_flat.reshape(n_lat_ch + n_ctx_ch, n_batch, n_groups, S5_CHUNK, s5_group).transpose(1, 0, 3, 2, 4)
    y_rows = jnp.concatenate([y5[:, :n_lat_ch].reshape(n_lat_rows, s5w),
                              y5[:, n_lat_ch:].reshape(n_ctx_rows, s5w)], axis=0)
    mix_b = _glu(y_rows, s5_glu_w[0], s5_glu_b[0], tm_ffn)

    w_out = ab_w_out[0].astype(BF16)
    xs = _matmul_residual([mix_a, mix_b], [w_out[:mw], w_out[mw:]], xs, mods[0], 2, seg_big,
                          tm_big, 512, "ab_out_proj")
    h = _modulate(xs, norm_ffn_g[0], mods[0], 3, 4, seg_256, 256)
    ffn_kw = dict(tm=tm_ffn, tf=tf, n_lat_rows=n_lat_rows, seq=seq, ctx_len=ctx_len)
    xs = _conv_ffn(h, xs, _pad_ffn_weights(ffn_w_up[0], ffn_conv_w[0], ffn_conv_b[0], ffn_w_down[0], tf),
                   mods[0], 5, seg_ffn, **ffn_kw)

    h = _modulate(xs, norm_mix_g[1], mods[1], 0, 1, seg_256, 256)
    qkv = _matmul(h, na_w_qkv[0].astype(BF16), BF16, tm_big, 512, "na_qkv_proj")
    t2, rmask = _na_tables(na_rpb[0], rows)
    o = _na_attention(qkv, t2, rmask, n_batch=n_batch, seq=seq, ctx_len=ctx_len, n_heads=na_heads, hd=hd,
                      kh=kh, win_h=(na_rpb.shape[2] + 1) // 2)
    xl = _matmul_residual([o], [na_w_out[0].astype(BF16)], xs, mods[1], 2, seg_big, tm_big, 512,
                          "na_out_proj")
    h = _modulate(xl, norm_ffn_g[1], mods[1], 3, 4, seg_256, 256)
    xl = _conv_ffn(h, xl, _pad_ffn_weights(ffn_w_up[1], ffn_conv_w[1], ffn_conv_b[1], ffn_w_down[1], tf),
                   mods[1], 5, seg_ffn, **ffn_kw)
    return _rmsnorm(xl, final_norm_g, 256).reshape(n_batch, seq, d)
```

```python
import functools
import math

import jax
import jax.numpy as jnp
import numpy as np
from jax import lax
from jax.experimental import pallas as pl
from jax.experimental.pallas import tpu as pltpu

F32 = jnp.float32
BF16 = jnp.bfloat16

EPS = 1e-6
ROPE_BASE = 10000.0
GRID_W = 64
MLSTM_CHUNK = 256
S5_CHUNK = 16
NA_QROWS = 8
NA_KROWS = 16
LANE = 128
VMEM_LIMIT_BYTES = 58 * 1024 * 1024


def _cparams(*sem):
    return pltpu.CompilerParams(dimension_semantics=sem, vmem_limit_bytes=VMEM_LIMIT_BYTES)


def _pick(n, candidates):
    for c in candidates:
        if n % c == 0:
            return c
    raise ValueError(f"no tile in {candidates} divides {n}")


def _ada_kernel(c_ref, w_ref, b_ref, o_ref):
    s = jax.nn.silu(c_ref[...]).astype(BF16)
    o_ref[...] = jnp.dot(s, w_ref[...].astype(BF16), preferred_element_type=F32) + b_ref[...]


def _ada(cond8, mod_w, mod_b):
    depth, d, n6 = mod_w.shape
    tn = _pick(n6, (512, 256, 128))
    return pl.pallas_call(
        _ada_kernel,
        out_shape=jax.ShapeDtypeStruct((depth, 8, n6), F32),
        grid=(depth, n6 // tn),
        in_specs=[pl.BlockSpec((8, d), lambda l, j: (0, 0)),
                  pl.BlockSpec((None, d, tn), lambda l, j: (l, 0, j)),
                  pl.BlockSpec((None, 1, tn), lambda l, j: (l, 0, j))],
        out_specs=pl.BlockSpec((None, 8, tn), lambda l, j: (l, 0, j)),
        compiler_params=_cparams("parallel", "parallel"),
        name="ada_mod",
    )(cond8, mod_w, mod_b.reshape(depth, 1, n6))


def _modulate_kernel(x_ref, g_ref, sh_ref, sc_ref, o_ref):
    x = x_ref[...]
    y = x * lax.rsqrt(jnp.mean(x * x, axis=-1, keepdims=True) + EPS) * g_ref[...]
    o_ref[...] = (y * (1.0 + sc_ref[...]) + sh_ref[...]).astype(o_ref.dtype)


def _seg_fn(tm, n_lat_rows, seq, n_batch):
    n_lat_blk = n_lat_rows // tm

    def seg(i):
        return jnp.where(i < n_lat_blk, (i * tm) // seq, n_batch)
    return seg


def _modulate(xs, gain, mods, shift_idx, scale_idx, seg, tm):
    r, d = xs.shape
    return pl.pallas_call(
        _modulate_kernel,
        out_shape=jax.ShapeDtypeStruct((r, d), BF16),
        grid=(r // tm,),
        in_specs=[pl.BlockSpec((tm, d), lambda i: (i, 0)),
                  pl.BlockSpec((1, d), lambda i: (0, 0)),
                  pl.BlockSpec((None, None, 1, d), lambda i: (seg(i), shift_idx, 0, 0)),
                  pl.BlockSpec((None, None, 1, d), lambda i: (seg(i), scale_idx, 0, 0))],
        out_specs=pl.BlockSpec((tm, d), lambda i: (i, 0)),
        compiler_params=_cparams("parallel"),
        name="modulate",
    )(xs, gain.reshape(1, d), mods, mods)


def _rmsnorm_kernel(x_ref, g_ref, o_ref):
    x = x_ref[...]
    o_ref[...] = x * lax.rsqrt(jnp.mean(x * x, axis=-1, keepdims=True) + EPS) * g_ref[...]


def _rmsnorm(xs, gain, tm):
    r, d = xs.shape
    return pl.pallas_call(
        _rmsnorm_kernel,
        out_shape=jax.ShapeDtypeStruct((r, d), F32),
        grid=(r // tm,),
        in_specs=[pl.BlockSpec((tm, d), lambda i: (i, 0)),
                  pl.BlockSpec((1, d), lambda i: (0, 0))],
        out_specs=pl.BlockSpec((tm, d), lambda i: (i, 0)),
        compiler_params=_cparams("parallel"),
        name="final_norm",
    )(xs, gain.reshape(1, d))


def _mm_kernel(a_ref, b_ref, o_ref):
    o_ref[...] = jnp.dot(a_ref[...], b_ref[...], preferred_element_type=F32).astype(o_ref.dtype)


def _matmul(a, b, out_dtype, tm, tn, name):
    m, k = a.shape
    n = b.shape[1]
    return pl.pallas_call(
        _mm_kernel,
        out_shape=jax.ShapeDtypeStruct((m, n), out_dtype),
        grid=(m // tm, n // tn),
        in_specs=[pl.BlockSpec((tm, k), lambda i, j: (i, 0)),
                  pl.BlockSpec((k, tn), lambda i, j: (0, j))],
        out_specs=pl.BlockSpec((tm, tn), lambda i, j: (i, j)),
        compiler_params=_cparams("parallel", "parallel"),
        name=name,
    )(a, b)


def _mm_res_kernel(*refs, n_a):
    a_refs, b_refs = refs[:n_a], refs[n_a:2 * n_a]
    x_ref, gate_ref, o_ref = refs[2 * n_a:]
    acc = jnp.dot(a_refs[0][...], b_refs[0][...], preferred_element_type=F32)
    for a_ref, b_ref in zip(a_refs[1:], b_refs[1:]):
        acc = acc + jnp.dot(a_ref[...], b_ref[...], preferred_element_type=F32)
    o_ref[...] = x_ref[...] + gate_ref[...] * acc


def _matmul_residual(a_list, b_list, xs, mods, gate_idx, seg, tm, tn, name):
    m = a_list[0].shape[0]
    n = b_list[0].shape[1]
    in_specs = ([pl.BlockSpec((tm, a.shape[1]), lambda i, j: (i, 0)) for a in a_list]
                + [pl.BlockSpec((b.shape[0], tn), lambda i, j: (0, j)) for b in b_list]
                + [pl.BlockSpec((tm, tn), lambda i, j: (i, j)),
                   pl.BlockSpec((None, None, 1, tn), lambda i, j: (seg(i), gate_idx, 0, j))])
    return pl.pallas_call(
        functools.partial(_mm_res_kernel, n_a=len(a_list)),
        out_shape=jax.ShapeDtypeStruct((m, n), F32),
        grid=(m // tm, n // tn),
        in_specs=in_specs,
        out_specs=pl.BlockSpec((tm, tn), lambda i, j: (i, j)),
        compiler_params=_cparams("parallel", "parallel"),
        name=name,
    )(*a_list, *b_list, xs, mods)


def _rope(x, cos, sin):
    parts = []
    for j in range(x.shape[1] // LANE):
        sl = slice(j * LANE, (j + 1) * LANE)
        xs = x[:, sl]
        parts.append(xs * cos[:, sl] + pltpu.roll(xs, LANE // 2, axis=1) * sin[:, sl])
    return jnp.concatenate(parts, axis=1)


def _mlstm_kernel(gb_ref, q_ref, k_ref, v_ref, cos_ref, sin_ref, gcol_ref, grow_ref, *rest,
                  reverse, n_heads, scale):
    if reverse:
        o_ref, hf_ref, hg_ref, out_ref, ct_ref, m_ref = rest
    else:
        out_ref, ct_ref, m_ref = rest
    hh = pl.program_id(1)
    step = pl.program_id(2)
    d = 1 if reverse else 0
    L, dv = v_ref.shape

    @pl.when(step == 0)
    def _():
        ct_ref[...] = jnp.zeros_like(ct_ref)
        m_ref[...] = jnp.zeros_like(m_ref)

    b_i = gb_ref[(d * 2 + 0) * n_heads + hh]
    b_f = gb_ref[(d * 2 + 1) * n_heads + hh]
    li_col = gcol_ref[:, 0:1] + b_i
    lf_col = jax.nn.log_sigmoid(gcol_ref[:, 1:2] + b_f)
    li_row = grow_ref[0:1, :] + b_i
    lf_row = jax.nn.log_sigmoid(grow_ref[1:2, :] + b_f)

    row = lax.broadcasted_iota(jnp.int32, (L, L), 0)
    col = lax.broadcasted_iota(jnp.int32, (L, L), 1)
    incl = (col >= row) if reverse else (col <= row)
    incl_t = (row >= col) if reverse else (row <= col)
    cum_col = jnp.sum(jnp.where(incl, lf_row, 0.0), axis=1, keepdims=True)
    cum_row = jnp.sum(jnp.where(incl_t, lf_col, 0.0), axis=0, keepdims=True)
    total = jnp.sum(lf_row, axis=1, keepdims=True)
    m_prev = m_ref[0:1, 0:1]

    dmat = jnp.where(incl, cum_col - cum_row + li_row, -jnp.inf)
    carried = cum_col + m_prev
    m_loc = jnp.maximum(carried, jnp.max(dmat, axis=1, keepdims=True))
    w = jnp.exp(dmat - m_loc)
    w_state = jnp.exp(carried - m_loc)

    n_lat = pl.num_programs(2) - 1
    tab = jnp.where(step == 0, n_lat, (n_lat - step) if reverse else (step - 1))
    t0 = pl.multiple_of(tab * L, L)
    cos = cos_ref[pl.ds(t0, L), :]
    sin = sin_ref[pl.ds(t0, L), :]
    qb = (_rope(q_ref[...].astype(F32), cos, sin) * scale).astype(BF16)
    kb = _rope(k_ref[...].astype(F32), cos, sin).astype(BF16)
    s_qk = lax.dot_general(qb, kb, (((1,), (1,)), ((), ())), preferred_element_type=F32)
    sc = (s_qk * w).astype(BF16)
    vext = jnp.concatenate([v_ref[...], jnp.ones((L, LANE), v_ref.dtype)], axis=1)
    ct = ct_ref[...]
    res = (jnp.dot(sc, vext, preferred_element_type=F32)
           + w_state * jnp.dot(qb, ct.astype(BF16), preferred_element_type=F32))
    den = res[:, dv:dv + 1]
    hval = res[:, :dv] / jnp.maximum(jnp.abs(den), jnp.exp(-m_loc))

    src = total - cum_col + li_col
    m_new = jnp.maximum(total + m_prev, jnp.max(src, axis=0, keepdims=True))
    decay = jnp.exp(total + m_prev - m_new)
    w_src = jnp.exp(src - m_new)
    upd = lax.dot_general(kb, (w_src * vext.astype(F32)).astype(BF16), (((0,), (0,)), ((), ())),
                          preferred_element_type=F32)
    ct_ref[...] = decay * ct + upd
    m_ref[...] = jnp.broadcast_to(m_new, m_ref.shape)

    if reverse:
        hm = hf_ref[...] + hval
        hn = hm * lax.rsqrt(jnp.mean(hm * hm, axis=-1, keepdims=True) + EPS) * hg_ref[...]
        out_ref[...] = (jax.nn.sigmoid(o_ref[...].astype(F32)) * hn).astype(out_ref.dtype)
    else:
        out_ref[...] = hval


def _mlstm_direction(p1, gcol, grow, gate_b, cos_t, sin_t, head_g, hf, *, reverse, n_batch, seq,
                     n_heads, dqk, dv):
    r = p1.shape[0]
    L = MLSTM_CHUNK
    n_lat = seq // L
    lat_blocks = n_batch * n_lat
    d = 1 if reverse else 0

    def blk(b, s):
        lat = b * n_lat + ((n_lat - s) if reverse else (s - 1))
        return jnp.where(s == 0, lat_blocks + b, lat)

    in_specs = [
        pl.BlockSpec(memory_space=pltpu.SMEM),
        pl.BlockSpec((L, dqk), lambda b, h, s: (blk(b, s), h)),
        pl.BlockSpec((L, dqk), lambda b, h, s: (blk(b, s), n_heads + h)),
        pl.BlockSpec((L, dv), lambda b, h, s: (blk(b, s), n_heads + h)),
        pl.BlockSpec(cos_t.shape, lambda b, h, s: (0, 0), pipeline_mode=pl.Buffered(1)),
        pl.BlockSpec(sin_t.shape, lambda b, h, s: (0, 0), pipeline_mode=pl.Buffered(1)),
        pl.BlockSpec((None, None, L, 2), lambda b, h, s: (d, h, blk(b, s), 0)),
        pl.BlockSpec((None, None, 2, L), lambda b, h, s: (d, h, 0, blk(b, s))),
    ]
    args = [gate_b.reshape(-1), p1, p1, p1, cos_t, sin_t, gcol, grow]
    if reverse:
        in_specs += [
            pl.BlockSpec((L, dv), lambda b, h, s: (blk(b, s), 2 * n_heads + h)),
            pl.BlockSpec((L, dv), lambda b, h, s: (blk(b, s), h)),
            pl.BlockSpec((None, 1, dv), lambda b, h, s: (h, 0, 0)),
        ]
        args += [p1, hf, head_g.reshape(n_heads, 1, dv)]
    return pl.pallas_call(
        functools.partial(_mlstm_kernel, reverse=reverse, n_heads=n_heads, scale=dqk ** -0.5),
        out_shape=jax.ShapeDtypeStruct((r, n_heads * dv), BF16 if reverse else F32),
        grid=(n_batch, n_heads, n_lat + 1),
        in_specs=in_specs,
        out_specs=pl.BlockSpec((L, dv), lambda b, h, s: (blk(b, s), h)),
        scratch_shapes=[pltpu.VMEM((dqk, dv + LANE), F32), pltpu.VMEM((8, LANE), F32)],
        compiler_params=_cparams("parallel", "parallel", "arbitrary"),
        name="mlstm_bwd" if reverse else "mlstm_fwd",
    )(*args)


def _rope_tables(seq, ctx_len, dqk):
    half = dqk // 2
    inv = ROPE_BASE ** (-jnp.arange(0, half, 2, dtype=F32) / half)
    t = jnp.arange(seq)

    def one(pos):
        ang = pos.astype(F32)[:, None] * inv
        c, s = jnp.cos(ang), jnp.sin(ang)
        return jnp.concatenate([c, c], axis=-1), jnp.concatenate([-s, s], axis=-1)

    c_r, s_r = one(t // GRID_W)
    c_c, s_c = one(t % GRID_W)
    cos_t = jnp.concatenate([c_r, c_c], axis=-1)
    sin_t = jnp.concatenate([s_r, s_c], axis=-1)
    cos_t = jnp.concatenate([cos_t, jnp.ones((ctx_len, dqk), F32)], axis=0)
    sin_t = jnp.concatenate([sin_t, jnp.zeros((ctx_len, dqk), F32)], axis=0)
    return cos_t, sin_t


def _s5_tables(a_re, a_im, log_dt, b_re, b_im, c_re, c_im):
    g_, p_, hsz = b_re.shape
    lc = S5_CHUNK
    bmat = lax.complex(b_re.astype(F32), b_im.astype(F32))
    cmat = lax.complex(c_re.astype(F32), c_im.astype(F32))
    tau = jnp.arange(lc + 1, dtype=F32)
    s_idx = jnp.arange(lc)
    m_tot = 0.0
    w_parts, v_parts, a_parts = [], [], []
    for d in range(2):
        lam = lax.complex(a_re[d].astype(F32), a_im[d].astype(F32))
        dt = jnp.exp(log_dt[d].astype(F32))[:, None]
        fac = (jnp.exp(lam * dt) - 1) / lam
        apow = jnp.exp(tau[:, None, None] * (lam * dt)[None])
        bf = bmat * fac[:, :, None]
        cb = cmat.transpose(0, 2, 1)[:, :, :, None] * bf[:, :, None, :]
        kt = jnp.real(jnp.einsum('tgp,gpij->tgij', apow[:lc], cb,
                                 precision=lax.Precision.HIGHEST))
        lag = (s_idx[None, :] - s_idx[:, None]) if d == 0 else (s_idx[:, None] - s_idx[None, :])
        ok = lag >= 0
        kk = kt[jnp.clip(lag, 0, lc - 1)]
        kk = jnp.where(ok[:, :, None, None, None], kk, 0.0)
        m_tot = m_tot + kk.transpose(2, 0, 4, 1, 3).reshape(g_, lc * hsz, lc * hsz)
        pw = apow[(lc - 1 - s_idx) if d == 0 else s_idx]
        wc = pw[:, :, :, None] * bf[None]
        wc = wc.transpose(1, 0, 3, 2).reshape(g_, lc * hsz, p_)
        pv = apow[(s_idx + 1) if d == 0 else (lc - s_idx)]
        vc = cmat[None] * pv[:, :, None, :]
        vc = vc.transpose(1, 3, 0, 2).reshape(g_, p_, lc * hsz)
        pad_w = jnp.zeros((g_, lc * hsz, LANE - p_), F32)
        pad_v = jnp.zeros((g_, LANE - p_, lc * hsz), F32)
        w_parts += [jnp.real(wc), pad_w, jnp.imag(wc), pad_w]
        v_parts += [jnp.real(vc), pad_v, -jnp.imag(vc), pad_v]
        a_chunk = apow[lc]
        pad_a = jnp.zeros((g_, LANE - p_), F32)
        a_parts += [jnp.concatenate([jnp.real(a_chunk), pad_a], axis=1),
                    jnp.concatenate([jnp.imag(a_chunk), pad_a], axis=1)]
    w_all = jnp.concatenate(w_parts, axis=2)
    v_all = jnp.concatenate(v_parts, axis=1)
    a_all = jnp.stack(a_parts + [jnp.zeros_like(a_parts[0])] * 4, axis=1)
    return m_tot.astype(BF16), w_all.astype(BF16), v_all.astype(BF16), a_all


def _s5_kernel(*refs, nb, n_lat, n_ctx, hsz):
    lc = S5_CHUNK
    x_refs = refs[:lc]
    m_ref, w_ref, v_ref, a_ref, o_ref = refs[lc:lc + 5]
    z_refs = refs[lc + 5:lc + 9]
    sp_refs = refs[lc + 9:lc + 13]
    k = pl.program_id(0) % (LANE // hsz)
    cw = lc * hsz
    ri = lax.broadcasted_iota(jnp.int32, (lc * LANE, cw), 0)
    ci = lax.broadcasted_iota(jnp.int32, (lc * LANE, cw), 1)
    sel = jnp.where(ri == (ci // hsz) * LANE + k * hsz + ci % hsz, 1.0, 0.0).astype(BF16)
    ri_t = lax.broadcasted_iota(jnp.int32, (cw, lc * LANE), 0)
    ci_t = lax.broadcasted_iota(jnp.int32, (cw, lc * LANE), 1)
    sel_t = jnp.where(ci_t == (ri_t // hsz) * LANE + k * hsz + ri_t % hsz, 1.0, 0.0).astype(BF16)

    x = jnp.concatenate([r[...] for r in x_refs], axis=1).astype(BF16)
    ub = jnp.dot(x, sel, preferred_element_type=F32).astype(BF16)
    z = jnp.dot(ub, w_ref[...], preferred_element_type=F32)
    for q in range(4):
        z_refs[q][...] = z[:, q * LANE:(q + 1) * LANE]
    n_chunks = n_lat + n_ctx

    def rows_of(chunk):
        if chunk < n_lat:
            return pl.ds(chunk, nb, stride=n_lat)
        return pl.ds(nb * n_lat + chunk - n_lat, nb, stride=n_ctx)

    a = a_ref[...]
    coef = [(a[2 * d:2 * d + 1, :], a[2 * d + 1:2 * d + 2, :]) for d in range(2)]
    state = [(jnp.zeros((nb, LANE), F32), jnp.zeros((nb, LANE), F32)) for _ in range(2)]
    for t in range(n_chunks):
        chunk = ((t + n_lat) % n_chunks, n_chunks - 1 - t)
        for d in range(2):
            rs = rows_of(chunk[d])
            a_r, a_i = coef[d]
            s_re, s_im = state[d]
            sp_refs[2 * d][rs, :] = s_re
            sp_refs[2 * d + 1][rs, :] = s_im
            z_re = z_refs[2 * d][rs, :]
            z_im = z_refs[2 * d + 1][rs, :]
            state[d] = (a_r * s_re - a_i * s_im + z_re, a_r * s_im + a_i * s_re + z_im)
    sp = jnp.concatenate([r[...] for r in sp_refs], axis=1)
    y = (jnp.dot(ub, m_ref[...], preferred_element_type=F32)
         + jnp.dot(sp.astype(BF16), v_ref[...], preferred_element_type=F32))
    y_hi = y.astype(BF16)
    y_lo = (y - y_hi.astype(F32)).astype(BF16)
    back = (jnp.dot(y_hi, sel_t, preferred_element_type=F32)
            + jnp.dot(y_lo, sel_t, preferred_element_type=F32))

    @pl.when(k == 0)
    def _():
        o_ref[...] = jnp.zeros_like(o_ref)

    for s in range(lc):
        o_ref[s] += back[:, s * LANE:(s + 1) * LANE]


def _s5_scan(p2v, tables, *, nb, n_lat, n_ctx, n_groups, hsz, row_w):
    m_tot, w_all, v_all, a_all = tables
    rows = p2v.shape[0]
    lc = S5_CHUNK
    cw = lc * hsz
    gpt = LANE // hsz
    tiles_per_tok = row_w // LANE

    def x_spec(s):
        return pl.BlockSpec((rows, LANE), lambda g: (0, s * tiles_per_tok + g // gpt))

    return pl.pallas_call(
        functools.partial(_s5_kernel, nb=nb, n_lat=n_lat, n_ctx=n_ctx, hsz=hsz),
        out_shape=jax.ShapeDtypeStruct((lc, rows, n_groups * hsz), F32),
        grid=(n_groups,),
        in_specs=[x_spec(s) for s in range(lc)] + [
            pl.BlockSpec((None, cw, cw), lambda g: (g, 0, 0)),
            pl.BlockSpec((None, cw, 4 * LANE), lambda g: (g, 0, 0)),
            pl.BlockSpec((None, 4 * LANE, cw), lambda g: (g, 0, 0)),
            pl.BlockSpec((None, 8, LANE), lambda g: (g, 0, 0))],
        out_specs=pl.BlockSpec((lc, rows, LANE), lambda g: (0, 0, g // gpt)),
        scratch_shapes=[pltpu.VMEM((rows, LANE), F32)] * 8,
        compiler_params=_cparams("arbitrary"),
        name="s5_scan",
    )(*([p2v] * lc), m_tot, w_all, v_all, a_all)


def _glu_kernel(y_ref, p_ref, d_ref, w_ref, b_ref, o_ref):
    width = y_ref.shape[1]
    g = jax.nn.gelu(y_ref[...] + d_ref[...] * p_ref[:, :width])
    z = jnp.dot(g.astype(BF16), w_ref[...], preferred_element_type=F32) + b_ref[...]
    o_ref[...] = (g * jax.nn.sigmoid(z)).astype(o_ref.dtype)


def _glu(y, p2, d_skip, glu_w, glu_b, tm):
    r, w = y.shape
    return pl.pallas_call(
        _glu_kernel,
        out_shape=jax.ShapeDtypeStruct((r, w), BF16),
        grid=(r // tm,),
        in_specs=[pl.BlockSpec((tm, w), lambda i: (i, 0)),
                  pl.BlockSpec((tm, p2.shape[1]), lambda i: (i, 0)),
                  pl.BlockSpec((1, w), lambda i: (0, 0)),
                  pl.BlockSpec((w, w), lambda i: (0, 0)),
                  pl.BlockSpec((1, w), lambda i: (0, 0))],
        out_specs=pl.BlockSpec((tm, w), lambda i: (i, 0)),
        compiler_params=_cparams("parallel"),
        name="s5_glu",
    )(y, p2, d_skip.astype(F32).reshape(1, w), glu_w.astype(BF16), glu_b.astype(F32).reshape(1, w))


def _na_variants(rows, kh):
    n_blk = rows // NA_QROWS
    out = []
    for i in (0, min(1, n_blk - 1), n_blk - 1):
        out.append((i, min(max(i * NA_QROWS - kh // 2, 0), rows - NA_KROWS)))
    return out


def _na_tables(rpb, rows):
    n_heads, nri, nci = rpb.shape
    win_h, win_w = (nri + 1) // 2, (nci + 1) // 2
    kh = min(win_h, rows)
    col = np.arange(GRID_W)
    col_start = np.clip(col - win_w // 2, 0, GRID_W - win_w)
    col_ok = (col[None, :] >= col_start[:, None]) & (col[None, :] < col_start[:, None] + win_w)
    ci = np.clip(col[None, :] - col[:, None], -(win_w - 1), win_w - 1) + (win_w - 1)
    onehot = (ci[None] == np.arange(nci)[:, None, None]).astype(np.float32)
    t1 = jnp.einsum('hrc,cqk->hrqk', rpb.astype(F32), onehot, precision=lax.Precision.HIGHEST)
    t1 = jnp.where(col_ok[None, None], t1, -jnp.inf)
    dummy = jnp.full((n_heads, 1, GRID_W, GRID_W), -jnp.inf, F32)
    t1 = jnp.concatenate([dummy, t1, dummy], axis=1)
    t2 = jnp.concatenate([t1[:, :nri + 1], t1[:, 1:nri + 2]], axis=-1)
    rmask = np.zeros((3, NA_QROWS, NA_KROWS // 2, 1, 2 * GRID_W), np.float32)
    for v, (i, ks) in enumerate(_na_variants(rows, kh)):
        for a in range(NA_QROWS):
            r_start = min(max(i * NA_QROWS + a - kh // 2, 0), rows - kh)
            for b in range(NA_KROWS):
                if not (r_start <= ks + b < r_start + kh):
                    rmask[v, a, b // 2, 0, (b % 2) * GRID_W:(b % 2 + 1) * GRID_W] = -np.inf
    return t2, jnp.asarray(rmask)


def _na_kernel(q_ref, k_ref, v_ref, kc_ref, vc_ref, t2_ref, rm_ref, o_ref, bias_ref, *, rows, kh, win_h,
               scale):
    n_blk = rows // NA_QROWS
    tq = NA_QROWS * GRID_W
    tk = NA_KROWS * GRID_W
    n_r = t2_ref.shape[0]

    @pl.when(pl.program_id(1) == 0)
    def _():
        for v, (i, ks) in enumerate(_na_variants(rows, kh)):
            for a in range(NA_QROWS):
                for pair in range(NA_KROWS // 2):
                    r = (ks + 2 * pair) - (i * NA_QROWS + a) + (win_h - 1)
                    tile = t2_ref[min(max(r + 1, 0), n_r - 1)] + rm_ref[v, a, pair]
                    bias_ref[v, a * GRID_W:(a + 1) * GRID_W, pair * 2 * GRID_W:(pair + 1) * 2 * GRID_W] = tile

    kc = kc_ref[...]
    vc = vc_ref[...]
    nt = (((1,), (1,)), ((), ()))

    def body(i, carry):
        q0 = pl.multiple_of(i * tq, tq)
        ks = jnp.clip(i * NA_QROWS - kh // 2, 0, rows - NA_KROWS)
        k0 = pl.multiple_of(ks * GRID_W, 4 * GRID_W)
        variant = jnp.where(i == 0, 0, jnp.where(i == n_blk - 1, 2, 1))
        qb = q_ref[pl.ds(q0, tq), :]
        kb = k_ref[pl.ds(k0, tk), :]
        vb = v_ref[pl.ds(k0, tk), :]
        s_loc = lax.dot_general(qb, kb, nt, preferred_element_type=F32) * scale + bias_ref[variant]
        s_ctx = lax.dot_general(qb, kc, nt, preferred_element_type=F32) * scale
        m = jnp.maximum(jnp.max(s_loc, axis=1, keepdims=True), jnp.max(s_ctx, axis=1, keepdims=True))
        p_loc = jnp.exp(s_loc - m)
        p_ctx = jnp.exp(s_ctx - m)
        denom = jnp.sum(p_loc, axis=1, keepdims=True) + jnp.sum(p_ctx, axis=1, keepdims=True)
        acc = (jnp.dot(p_loc.astype(BF16), vb, preferred_element_type=F32)
               + jnp.dot(p_ctx.astype(BF16), vc, preferred_element_type=F32))
        o_ref[pl.ds(q0, tq), :] = (acc / denom).astype(o_ref.dtype)
        return carry

    lax.fori_loop(0, n_blk, body, 0, unroll=2)


def _na_attention(qkv, t2, rmask, *, n_batch, seq, ctx_len, n_heads, hd, kh, win_h):
    rows = seq // GRID_W
    lat_ctx_blk = (n_batch * seq) // ctx_len
    tq, tk = NA_QROWS * GRID_W, NA_KROWS * GRID_W
    return pl.pallas_call(
        functools.partial(_na_kernel, rows=rows, kh=kh, win_h=win_h, scale=hd ** -0.5),
        out_shape=jax.ShapeDtypeStruct((n_batch * seq, n_heads * hd), BF16),
        grid=(n_heads, n_batch),
        in_specs=[pl.BlockSpec((seq, hd), lambda h, b: (b, h)),
                  pl.BlockSpec((seq, hd), lambda h, b: (b, n_heads + h)),
                  pl.BlockSpec((seq, hd), lambda h, b: (b, 2 * n_heads + h)),
                  pl.BlockSpec((ctx_len, hd), lambda h, b: (lat_ctx_blk + b, n_heads + h)),
                  pl.BlockSpec((ctx_len, hd), lambda h, b: (lat_ctx_blk + b, 2 * n_heads + h)),
                  pl.BlockSpec((None,) + t2.shape[1:], lambda h, b: (h, 0, 0, 0)),
                  pl.BlockSpec(rmask.shape, lambda h, b: (0, 0, 0, 0, 0))],
        out_specs=pl.BlockSpec((seq, hd), lambda h, b: (b, h)),
        scratch_shapes=[pltpu.VMEM((3, tq, tk), F32)],
        compiler_params=_cparams("arbitrary", "arbitrary"),
        name="na_attention",
    )(qkv, qkv, qkv, qkv, qkv, t2, rmask)


def _ffn_kernel(hx_ref, wg_ref, wa_ref, cw_ref, cb_ref, wd_ref, x_ref, gate_ref, o_ref, *,
                tm, n_lat_rows, seq, ctx_len):
    i = pl.program_id(0)
    j = pl.program_id(1)

    @pl.when(j == 0)
    def _():
        o_ref[...] = jnp.zeros_like(o_ref)

    g_all = jnp.dot(hx_ref[...], wg_ref[...], preferred_element_type=F32)
    a = jnp.dot(hx_ref[:tm, :], wa_ref[...], preferred_element_type=F32)
    g = g_all[:tm]
    local = lax.broadcasted_iota(jnp.int32, (tm, 1), 0)
    grow = i * tm + local
    in_lat = grow < n_lat_rows
    pos = jnp.where(in_lat, grow % seq, (grow - n_lat_rows) % ctx_len)
    last = jnp.where(in_lat, seq - 1, ctx_len - 1)
    g_prev = jnp.where(local == 0, g_all[tm:tm + 1, :], pltpu.roll(g, 1, axis=0))
    g_prev = jnp.where(pos == 0, 0.0, g_prev)
    g_next = jnp.where(local == tm - 1, g_all[tm + 1:tm + 2, :], pltpu.roll(g, tm - 1, axis=0))
    g_next = jnp.where(pos == last, 0.0, g_next)
    cw = cw_ref[...]
    gl = jax.nn.gelu(cw[0:1, :] * g_prev + cw[1:2, :] * g + cw[2:3, :] * g_next + cb_ref[...])
    o_ref[...] += jnp.dot((gl * a).astype(BF16), wd_ref[...], preferred_element_type=F32)

    @pl.when(j == pl.num_programs(1) - 1)
    def _():
        o_ref[...] = x_ref[...] + gate_ref[...] * o_ref[...]


def _conv_ffn(hs, xs, weights, mods, gate_idx, seg, *, tm, tf, n_lat_rows, seq, ctx_len):
    w_g, w_a, cw8, cb, w_down = weights
    r, d = hs.shape
    dffp = w_down.shape[0]
    n_blk, nj = r // tm, dffp // tf
    prev_rows = hs[jnp.maximum(jnp.arange(n_blk) * tm - 1, 0)]
    next_rows = hs[jnp.minimum((jnp.arange(n_blk) + 1) * tm, r - 1)]
    hx = jnp.concatenate([hs.reshape(n_blk, tm, d), prev_rows[:, None], next_rows[:, None],
                          jnp.zeros((n_blk, 14, d), hs.dtype)], axis=1)
    once = pl.Buffered(1)
    return pl.pallas_call(
        functools.partial(_ffn_kernel, tm=tm, n_lat_rows=n_lat_rows, seq=seq, ctx_len=ctx_len),
        out_shape=jax.ShapeDtypeStruct((r, d), F32),
        grid=(n_blk, nj),
        in_specs=[pl.BlockSpec((None, tm + 16, d), lambda i, j: (i, 0, 0), pipeline_mode=once),
                  pl.BlockSpec((d, tf), lambda i, j: (0, j)),
                  pl.BlockSpec((d, tf), lambda i, j: (0, j)),
                  pl.BlockSpec((8, tf), lambda i, j: (0, j)),
                  pl.BlockSpec((1, tf), lambda i, j: (0, j)),
                  pl.BlockSpec((tf, d), lambda i, j: (j, 0)),
                  pl.BlockSpec((tm, d), lambda i, j: (i, 0), pipeline_mode=once),
                  pl.BlockSpec((None, None, 1, d), lambda i, j: (seg(i), gate_idx, 0, 0))],
        out_specs=pl.BlockSpec((tm, d), lambda i, j: (i, 0), pipeline_mode=once),
        compiler_params=_cparams("parallel", "arbitrary"),
        name="conv_ffn",
    )(hx, w_g, w_a, cw8, cb, w_down, xs, mods)


def _pad_ffn_weights(w_up, conv_w, conv_b, w_down, tf):
    dff = w_down.shape[0]
    pad = -dff % tf
    w_a = jnp.pad(w_up[:, :dff].astype(BF16), ((0, 0), (0, pad)))
    w_g = jnp.pad(w_up[:, dff:].astype(BF16), ((0, 0), (0, pad)))
    cw8 = jnp.pad(conv_w.astype(F32), ((0, 8 - conv_w.shape[0]), (0, pad)))
    cb = jnp.pad(conv_b.astype(F32), (0, pad)).reshape(1, dff + pad)
    w_down_p = jnp.pad(w_down.astype(BF16), ((0, pad), (0, 0)))
    return w_g, w_a, cw8, cb, w_down_p


def kernel(x, c, ctx, c_ctx, mod_w, mod_b, norm_mix_g, norm_ffn_g, ab_w_in, mlstm_gate_b, mlstm_head_g, s5_a_re, s5_a_im, s5_log_dt, s5_b_re, s5_b_im, s5_c_re, s5_c_im, s5_d, s5_glu_w, s5_glu_b, ab_w_out, na_w_qkv, na_rpb, na_w_out, ffn_w_up, ffn_conv_w, ffn_conv_b, ffn_w_down, final_norm_g):
    n_batch, seq, d = x.shape
    ctx_len = ctx.shape[1]
    depth = mod_w.shape[0]
    assert depth == 2 and mod_w.shape[2] == 6 * d
    n_lat_rows = n_batch * seq
    n_ctx_rows = n_batch * ctx_len
    assert seq % MLSTM_CHUNK == 0 and ctx_len == MLSTM_CHUNK and seq % GRID_W == 0

    n_heads = mlstm_gate_b.shape[-1]
    s5w = s5_d.shape[-1]
    mw = d - s5w
    dv = mw // n_heads
    dqk = dv // 2
    qkw = n_heads * dqk
    n_groups, p_state, s5_group = s5_b_re.shape[1:]
    assert S5_CHUNK * s5_group == 2 * LANE and p_state <= LANE and dqk % (2 * LANE) == 0
    na_heads = na_rpb.shape[1]
    hd = d // na_heads
    rows = seq // GRID_W
    kh = min((na_rpb.shape[2] + 1) // 2, rows)
    assert rows % NA_QROWS == 0 and rows >= NA_KROWS and hd == LANE
    dff = ffn_w_down.shape[1]

    tm_big = _pick(math.gcd(n_lat_rows, n_ctx_rows), (1024, 512, 256))
    tm_ffn = _pick(math.gcd(n_lat_rows, n_ctx_rows), (512, 256))
    tf = 512
    seg_big = _seg_fn(tm_big, n_lat_rows, seq, n_batch)
    seg_ffn = _seg_fn(tm_ffn, n_lat_rows, seq, n_batch)
    seg_256 = _seg_fn(256, n_lat_rows, seq, n_batch)

    cond8 = jnp.concatenate([c, c_ctx[None], jnp.zeros((8 - n_batch - 1, d), F32)], axis=0)
    mods = _ada(cond8, mod_w, mod_b).reshape(depth, 8, 6, 1, d)

    xs = jnp.concatenate([x.reshape(n_lat_rows, d), ctx.reshape(n_ctx_rows, d)], axis=0)

    h = _modulate(xs, norm_mix_g[0], mods[0], 0, 1, seg_256, 256)
    w_in = ab_w_in[0]
    n_gate = 4 * n_heads
    w1 = w_in[:, :2 * qkw + 2 * mw].astype(BF16)
    w2 = jnp.concatenate([w_in[:, 2 * qkw + 2 * mw + n_gate:], w_in[:, 2 * qkw + 2 * mw:2 * qkw + 2 * mw + n_gate],
                          jnp.zeros((d, LANE - n_gate), F32)], axis=1).astype(BF16)
    p1 = _matmul(h, w1, BF16, tm_big, _pick(w1.shape[1], (512, 256, 128)), "ab_in_proj")
    p2 = _matmul(h, w2, F32, tm_ffn, w2.shape[1], "ab_in_proj_s5")
    gates = p2[:, s5w:s5w + n_gate].reshape(-1, 2, 2, n_heads)
    gcol = gates.transpose(1, 3, 0, 2)
    grow = gates.transpose(1, 3, 2, 0)

    cos_t, sin_t = _rope_tables(seq, ctx_len, dqk)
    common = dict(n_batch=n_batch, seq=seq, n_heads=n_heads, dqk=dqk, dv=dv)
    hf = _mlstm_direction(p1, gcol, grow, mlstm_gate_b[0], cos_t, sin_t, None, None, reverse=False, **common)
    mix_a = _mlstm_direction(p1, gcol, grow, mlstm_gate_b[0], cos_t, sin_t, mlstm_head_g[0], hf,
                             reverse=True, **common)

    n_lat_ch, n_ctx_ch = seq // S5_CHUNK, ctx_len // S5_CHUNK
    row_w = p2.shape[1]
    tables = _s5_tables(s5_a_re[0], s5_a_im[0], s5_log_dt[0], s5_b_re[0], s5_b_im[0], s5_c_re[0], s5_c_im[0])
    y16 = _s5_scan(p2.reshape(-1, S5_CHUNK * row_w), tables, nb=n_batch, n_lat=n_lat_ch, n_ctx=n_ctx_ch,
                   n_groups=n_groups, hsz=s5_group, row_w=row_w)
    y_rows = y16.transpose(1, 0, 2).reshape(-1, s5w)
    mix_b = _glu(y_rows, p2, s5_d[0], s5_glu_w[0], s5_glu_b[0], tm_ffn)

    w_out = ab_w_out[0].astype(BF16)
    xs = _matmul_residual([mix_a, mix_b], [w_out[:mw], w_out[mw:]], xs, mods[0], 2, seg_big,
                          tm_big, 512, "ab_out_proj")
    h = _modulate(xs, norm_ffn_g[0], mods[0], 3, 4, seg_256, 256)
    ffn_kw = dict(tm=tm_ffn, tf=tf, n_lat_rows=n_lat_rows, seq=seq, ctx_len=ctx_len)
    xs = _conv_ffn(h, xs, _pad_ffn_weights(ffn_w_up[0], ffn_conv_w[0], ffn_conv_b[0], ffn_w_down[0], tf),
                   mods[0], 5, seg_ffn, **ffn_kw)

    h = _modulate(xs, norm_mix_g[1], mods[1], 0, 1, seg_256, 256)
    qkv = _matmul(h, na_w_qkv[0].astype(BF16), BF16, tm_big, 512, "na_qkv_proj")
    t2, rmask = _na_tables(na_rpb[0], rows)
    o = _na_attention(qkv, t2, rmask, n_batch=n_batch, seq=seq, ctx_len=ctx_len, n_heads=na_heads, hd=hd,
                      kh=kh, win_h=(na_rpb.shape[2] + 1) // 2)
    xl = _matmul_residual([o], [na_w_out[0].astype(BF16)], xs, mods[1], 2, seg_big, tm_big, 512,
                          "na_out_proj")
    h = _modulate(xl, norm_ffn_g[1], mods[1], 3, 4, seg_256, 256)
    xl = _conv_ffn(h, xl, _pad_ffn_weights(ffn_w_up[1], ffn_conv_w[1], ffn_conv_b[1], ffn_w_down[1], tf),
                   mods[1], 5, seg_ffn, **ffn_kw)
    return _rmsnorm(xl, final_norm_g, 256).reshape(n_batch, seq, d)
```

```python
import functools
import math

import jax
import jax.numpy as jnp
import numpy as np
from jax import lax
from jax.experimental import pallas as pl
from jax.experimental.pallas import tpu as pltpu

F32 = jnp.float32
BF16 = jnp.bfloat16

EPS = 1e-6
ROPE_BASE = 10000.0
GRID_W = 64
MLSTM_CHUNK = 256
S5_CHUNK = 16
NA_QROWS = 8
NA_KROWS = 16
NA_UNROLL = 2
LANE = 128
VMEM_LIMIT_BYTES = 58 * 1024 * 1024


def _cparams(*sem):
    return pltpu.CompilerParams(dimension_semantics=sem, vmem_limit_bytes=VMEM_LIMIT_BYTES)


def _pick(n, candidates):
    for c in candidates:
        if n % c == 0:
            return c
    raise ValueError(f"no tile in {candidates} divides {n}")


def _ada_kernel(c_ref, w_ref, b_ref, o_ref):
    s = jax.nn.silu(c_ref[...]).astype(BF16)
    o_ref[...] = jnp.dot(s, w_ref[...].astype(BF16), preferred_element_type=F32) + b_ref[...]


def _ada(cond8, mod_w, mod_b):
    depth, d, n6 = mod_w.shape
    tn = _pick(n6, (512, 256, 128))
    return pl.pallas_call(
        _ada_kernel,
        out_shape=jax.ShapeDtypeStruct((depth, 8, n6), F32),
        grid=(depth, n6 // tn),
        in_specs=[pl.BlockSpec((8, d), lambda l, j: (0, 0)),
                  pl.BlockSpec((None, d, tn), lambda l, j: (l, 0, j)),
                  pl.BlockSpec((None, 1, tn), lambda l, j: (l, 0, j))],
        out_specs=pl.BlockSpec((None, 8, tn), lambda l, j: (l, 0, j)),
        compiler_params=_cparams("parallel", "parallel"),
        name="ada_mod",
    )(cond8, mod_w, mod_b.reshape(depth, 1, n6))


def _modulate_kernel(x_ref, g_ref, sh_ref, sc_ref, o_ref):
    x = x_ref[...]
    y = x * lax.rsqrt(jnp.mean(x * x, axis=-1, keepdims=True) + EPS) * g_ref[...]
    o_ref[...] = (y * (1.0 + sc_ref[...]) + sh_ref[...]).astype(o_ref.dtype)


def _seg_fn(tm, n_lat_rows, seq, n_batch):
    n_lat_blk = n_lat_rows // tm

    def seg(i):
        return jnp.where(i < n_lat_blk, (i * tm) // seq, n_batch)
    return seg


def _modulate(xs, gain, mods, shift_idx, scale_idx, seg, tm):
    r, d = xs.shape
    return pl.pallas_call(
        _modulate_kernel,
        out_shape=jax.ShapeDtypeStruct((r, d), BF16),
        grid=(r // tm,),
        in_specs=[pl.BlockSpec((tm, d), lambda i: (i, 0)),
                  pl.BlockSpec((1, d), lambda i: (0, 0)),
                  pl.BlockSpec((None, None, 1, d), lambda i: (seg(i), shift_idx, 0, 0)),
                  pl.BlockSpec((None, None, 1, d), lambda i: (seg(i), scale_idx, 0, 0))],
        out_specs=pl.BlockSpec((tm, d), lambda i: (i, 0)),
        compiler_params=_cparams("parallel"),
        name="modulate",
    )(xs, gain.reshape(1, d), mods, mods)


def _modulate_halo_kernel(x_ref, xp_ref, xn_ref, g_ref, sh_ref, sc_ref, o_ref):
    x = jnp.concatenate([x_ref[...], xp_ref[...], xn_ref[...]], axis=0)
    y = x * lax.rsqrt(jnp.mean(x * x, axis=-1, keepdims=True) + EPS) * g_ref[...]
    o_ref[...] = (y * (1.0 + sc_ref[...]) + sh_ref[...]).astype(o_ref.dtype)


def _modulate_halo(xs, gain, mods, shift_idx, scale_idx, seg, tm):
    r, d = xs.shape
    n_blk, per8 = r // tm, tm // 8
    return pl.pallas_call(
        _modulate_halo_kernel,
        out_shape=jax.ShapeDtypeStruct((n_blk, tm + 16, d), BF16),
        grid=(n_blk,),
        in_specs=[pl.BlockSpec((tm, d), lambda i: (i, 0)),
                  pl.BlockSpec((8, d), lambda i: (jnp.maximum(i * per8 - 1, 0), 0)),
                  pl.BlockSpec((8, d), lambda i: (jnp.minimum((i + 1) * per8, r // 8 - 1), 0)),
                  pl.BlockSpec((1, d), lambda i: (0, 0)),
                  pl.BlockSpec((None, None, 1, d), lambda i: (seg(i), shift_idx, 0, 0)),
                  pl.BlockSpec((None, None, 1, d), lambda i: (seg(i), scale_idx, 0, 0))],
        out_specs=pl.BlockSpec((None, tm + 16, d), lambda i: (i, 0, 0)),
        compiler_params=_cparams("parallel"),
        name="modulate_halo",
    )(xs, xs, xs, gain.reshape(1, d), mods, mods)


def _rmsnorm_kernel(x_ref, g_ref, o_ref):
    x = x_ref[...]
    o_ref[...] = x * lax.rsqrt(jnp.mean(x * x, axis=-1, keepdims=True) + EPS) * g_ref[...]


def _rmsnorm(xs, gain, tm):
    r, d = xs.shape
    return pl.pallas_call(
        _rmsnorm_kernel,
        out_shape=jax.ShapeDtypeStruct((r, d), F32),
        grid=(r // tm,),
        in_specs=[pl.BlockSpec((tm, d), lambda i: (i, 0)),
                  pl.BlockSpec((1, d), lambda i: (0, 0))],
        out_specs=pl.BlockSpec((tm, d), lambda i: (i, 0)),
        compiler_params=_cparams("parallel"),
        name="final_norm",
    )(xs, gain.reshape(1, d))


def _mm_kernel(a_ref, b_ref, o_ref):
    res = jnp.dot(a_ref[...], b_ref[...], preferred_element_type=F32).astype(o_ref.dtype)
    if len(o_ref.shape) == 3:
        w = o_ref.shape[2]
        for t in range(o_ref.shape[0]):
            o_ref[t] = res[:, t * w:(t + 1) * w]
    else:
        o_ref[...] = res


def _matmul(a, b, out_dtype, tm, tn, name, unit=None):
    m, k = a.shape
    n = b.shape[1]
    if unit is None:
        out_shape = jax.ShapeDtypeStruct((m, n), out_dtype)
        out_spec = pl.BlockSpec((tm, tn), lambda i, j: (i, j))
    else:
        out_shape = jax.ShapeDtypeStruct((n // unit, m, unit), out_dtype)
        out_spec = pl.BlockSpec((tn // unit, tm, unit), lambda i, j: (j, i, 0))
    return pl.pallas_call(
        _mm_kernel,
        out_shape=out_shape,
        grid=(m // tm, n // tn),
        in_specs=[pl.BlockSpec((tm, k), lambda i, j: (i, 0)),
                  pl.BlockSpec((k, tn), lambda i, j: (0, j))],
        out_specs=out_spec,
        compiler_params=_cparams("parallel", "parallel"),
        name=name,
    )(a, b)


def _rows_by_unit(ref):
    if len(ref.shape) == 2:
        return ref[...]
    return jnp.concatenate([ref[t] for t in range(ref.shape[0])], axis=1)


def _mm_res_kernel(*refs, n_a):
    a_refs, b_refs = refs[:n_a], refs[n_a:2 * n_a]
    x_ref, gate_ref, o_ref = refs[2 * n_a:]
    acc = jnp.dot(_rows_by_unit(a_refs[0]), b_refs[0][...], preferred_element_type=F32)
    for a_ref, b_ref in zip(a_refs[1:], b_refs[1:]):
        acc = acc + jnp.dot(_rows_by_unit(a_ref), b_ref[...], preferred_element_type=F32)
    o_ref[...] = x_ref[...] + gate_ref[...] * acc


def _matmul_residual(a_list, b_list, xs, mods, gate_idx, seg, tm, tn, name):
    m = a_list[0].shape[-2]
    n = b_list[0].shape[1]

    def a_spec(a):
        if a.ndim == 2:
            return pl.BlockSpec((tm, a.shape[1]), lambda i, j: (i, 0))
        return pl.BlockSpec((a.shape[0], tm, a.shape[2]), lambda i, j: (0, i, 0))

    in_specs = ([a_spec(a) for a in a_list]
                + [pl.BlockSpec((b.shape[0], tn), lambda i, j: (0, j)) for b in b_list]
                + [pl.BlockSpec((tm, tn), lambda i, j: (i, j)),
                   pl.BlockSpec((None, None, 1, tn), lambda i, j: (seg(i), gate_idx, 0, j))])
    return pl.pallas_call(
        functools.partial(_mm_res_kernel, n_a=len(a_list)),
        out_shape=jax.ShapeDtypeStruct((m, n), F32),
        grid=(m // tm, n // tn),
        in_specs=in_specs,
        out_specs=pl.BlockSpec((tm, tn), lambda i, j: (i, j)),
        compiler_params=_cparams("parallel", "parallel"),
        name=name,
    )(*a_list, *b_list, xs, mods)


def _rope(x, cos, sin):
    parts = []
    for j in range(x.shape[1] // LANE):
        sl = slice(j * LANE, (j + 1) * LANE)
        xs = x[:, sl]
        parts.append(xs * cos[:, sl] + pltpu.roll(xs, LANE // 2, axis=1) * sin[:, sl])
    return jnp.concatenate(parts, axis=1)


def _mlstm_kernel(gb_ref, q_ref, k_ref, v_ref, cos_ref, sin_ref, gcol_ref, grow_ref, *rest,
                  reverse, n_heads, scale):
    if reverse:
        o_ref, hf_ref, hg_ref, out_ref, ct_ref, m_ref = rest
    else:
        out_ref, ct_ref, m_ref = rest
    step = pl.program_id(1)
    d = 1 if reverse else 0
    L = q_ref.shape[0]
    dqk = q_ref.shape[1] // n_heads
    dv = v_ref.shape[1] // n_heads

    @pl.when(step == 0)
    def _():
        ct_ref[...] = jnp.zeros_like(ct_ref)
        m_ref[...] = jnp.zeros_like(m_ref)

    row = lax.broadcasted_iota(jnp.int32, (L, L), 0)
    col = lax.broadcasted_iota(jnp.int32, (L, L), 1)
    incl = (col >= row) if reverse else (col <= row)
    incl_t = (row >= col) if reverse else (row <= col)
    n_lat = pl.num_programs(1) - 1
    tab = jnp.where(step == 0, n_lat, (n_lat - step) if reverse else (step - 1))
    t0 = pl.multiple_of(tab * L, L)
    cos = cos_ref[pl.ds(t0, L), :]
    sin = sin_ref[pl.ds(t0, L), :]
    ones = jnp.ones((L, LANE), v_ref.dtype)

    for hh in range(n_heads):
        gi = (d * 2 + 0) * n_heads + hh
        gf = (d * 2 + 1) * n_heads + hh
        b_i = gb_ref[gi]
        b_f = gb_ref[gf]
        li_col = gcol_ref[:, gi:gi + 1] + b_i
        lf_col = jax.nn.log_sigmoid(gcol_ref[:, gf:gf + 1] + b_f)
        li_row = grow_ref[gi:gi + 1, :] + b_i
        lf_row = jax.nn.log_sigmoid(grow_ref[gf:gf + 1, :] + b_f)
        cum_col = jnp.sum(jnp.where(incl, lf_row, 0.0), axis=1, keepdims=True)
        cum_row = jnp.sum(jnp.where(incl_t, lf_col, 0.0), axis=0, keepdims=True)
        total = jnp.sum(lf_row, axis=1, keepdims=True)
        m_prev = m_ref[hh, 0:1, 0:1]

        dmat = jnp.where(incl, cum_col - cum_row + li_row, -jnp.inf)
        carried = cum_col + m_prev
        m_loc = jnp.maximum(carried, jnp.max(dmat, axis=1, keepdims=True))
        w = jnp.exp(dmat - m_loc)
        w_state = jnp.exp(carried - m_loc)

        qs = slice(hh * dqk, (hh + 1) * dqk)
        vs = slice(hh * dv, (hh + 1) * dv)
        qb = (_rope(q_ref[:, qs].astype(F32), cos, sin) * scale).astype(BF16)
        kb = _rope(k_ref[:, qs].astype(F32), cos, sin).astype(BF16)
        s_qk = lax.dot_general(qb, kb, (((1,), (1,)), ((), ())), preferred_element_type=F32)
        sc = (s_qk * w).astype(BF16)
        vext = jnp.concatenate([v_ref[:, vs], ones], axis=1)
        ct = ct_ref[hh]
        res = (jnp.dot(sc, vext, preferred_element_type=F32)
               + w_state * jnp.dot(qb, ct.astype(BF16), preferred_element_type=F32))
        den = res[:, dv:dv + 1]
        hval = res[:, :dv] / jnp.maximum(jnp.abs(den), jnp.exp(-m_loc))

        src = total - cum_col + li_col
        m_new = jnp.maximum(total + m_prev, jnp.max(src, axis=0, keepdims=True))
        decay = jnp.exp(total + m_prev - m_new)
        w_src = jnp.exp(src - m_new)
        upd = lax.dot_general(kb, (w_src * vext.astype(F32)).astype(BF16), (((0,), (0,)), ((), ())),
                              preferred_element_type=F32)
        ct_ref[hh] = decay * ct + upd
        m_ref[hh] = jnp.broadcast_to(m_new, m_ref.shape[1:])

        if reverse:
            hm = hf_ref[:, vs] + hval
            hn = hm * lax.rsqrt(jnp.mean(hm * hm, axis=-1, keepdims=True) + EPS) * hg_ref[:, vs]
            out_ref[:, vs] = (jax.nn.sigmoid(o_ref[:, vs].astype(F32)) * hn).astype(out_ref.dtype)
        else:
            out_ref[:, vs] = hval


def _mlstm_direction(p1, p2, grow, gate_b, cos_t, sin_t, head_g, hf, *, reverse, n_batch, seq,
                     n_heads, dqk, dv):
    r = p1.shape[0]
    L = MLSTM_CHUNK
    n_lat = seq // L
    lat_blocks = n_batch * n_lat
    qkw, mw = n_heads * dqk, n_heads * dv
    gate_tile = p2.shape[1] // LANE - 1

    def blk(b, s):
        lat = b * n_lat + ((n_lat - s) if reverse else (s - 1))
        return jnp.where(s == 0, lat_blocks + b, lat)

    once = pl.Buffered(1)
    in_specs = [
        pl.BlockSpec(memory_space=pltpu.SMEM),
        pl.BlockSpec((L, qkw), lambda b, s: (blk(b, s), 0)),
        pl.BlockSpec((L, qkw), lambda b, s: (blk(b, s), 1)),
        pl.BlockSpec((L, mw), lambda b, s: (blk(b, s), (2 * qkw) // mw)),
        pl.BlockSpec(cos_t.shape, lambda b, s: (0, 0), pipeline_mode=once),
        pl.BlockSpec(sin_t.shape, lambda b, s: (0, 0), pipeline_mode=once),
        pl.BlockSpec((L, LANE), lambda b, s: (blk(b, s), gate_tile)),
        pl.BlockSpec((grow.shape[0], L), lambda b, s: (0, blk(b, s))),
    ]
    args = [gate_b.reshape(-1), p1, p1, p1, cos_t, sin_t, p2, grow]
    if reverse:
        in_specs += [
            pl.BlockSpec((L, mw), lambda b, s: (blk(b, s), (2 * qkw) // mw + 1)),
            pl.BlockSpec((L, mw), lambda b, s: (blk(b, s), 0)),
            pl.BlockSpec((1, mw), lambda b, s: (0, 0)),
        ]
        args += [p1, hf, head_g.reshape(1, mw)]
    return pl.pallas_call(
        functools.partial(_mlstm_kernel, reverse=reverse, n_heads=n_heads, scale=dqk ** -0.5),
        out_shape=jax.ShapeDtypeStruct((r, mw), BF16 if reverse else F32),
        grid=(n_batch, n_lat + 1),
        in_specs=in_specs,
        out_specs=pl.BlockSpec((L, mw), lambda b, s: (blk(b, s), 0)),
        scratch_shapes=[pltpu.VMEM((n_heads, dqk, dv + LANE), F32), pltpu.VMEM((n_heads, 8, LANE), F32)],
        compiler_params=_cparams("parallel", "arbitrary"),
        name="mlstm_bwd" if reverse else "mlstm_fwd",
    )(*args)


def _rope_tables(seq, ctx_len, dqk):
    half = dqk // 2
    inv = ROPE_BASE ** (-jnp.arange(0, half, 2, dtype=F32) / half)
    t = jnp.arange(seq)

    def one(pos):
        ang = pos.astype(F32)[:, None] * inv
        c, s = jnp.cos(ang), jnp.sin(ang)
        return jnp.concatenate([c, c], axis=-1), jnp.concatenate([-s, s], axis=-1)

    c_r, s_r = one(t // GRID_W)
    c_c, s_c = one(t % GRID_W)
    cos_t = jnp.concatenate([c_r, c_c], axis=-1)
    sin_t = jnp.concatenate([s_r, s_c], axis=-1)
    cos_t = jnp.concatenate([cos_t, jnp.ones((ctx_len, dqk), F32)], axis=0)
    sin_t = jnp.concatenate([sin_t, jnp.zeros((ctx_len, dqk), F32)], axis=0)
    return cos_t, sin_t


def _s5_tables(a_re, a_im, log_dt, b_re, b_im, c_re, c_im):
    g_, p_, hsz = b_re.shape
    lc = S5_CHUNK
    bmat = lax.complex(b_re.astype(F32), b_im.astype(F32))
    cmat = lax.complex(c_re.astype(F32), c_im.astype(F32))
    tau = jnp.arange(lc + 1, dtype=F32)
    s_idx = jnp.arange(lc)
    m_tot = 0.0
    w_parts, v_parts, a_parts = [], [], []
    for d in range(2):
        lam = lax.complex(a_re[d].astype(F32), a_im[d].astype(F32))
        dt = jnp.exp(log_dt[d].astype(F32))[:, None]
        fac = (jnp.exp(lam * dt) - 1) / lam
        apow = jnp.exp(tau[:, None, None] * (lam * dt)[None])
        bf = bmat * fac[:, :, None]
        cb = cmat.transpose(0, 2, 1)[:, :, :, None] * bf[:, :, None, :]
        kt = jnp.real(jnp.einsum('tgp,gpij->tgij', apow[:lc], cb,
                                 precision=lax.Precision.HIGHEST))
        lag = (s_idx[None, :] - s_idx[:, None]) if d == 0 else (s_idx[:, None] - s_idx[None, :])
        ok = lag >= 0
        kk = kt[jnp.clip(lag, 0, lc - 1)]
        kk = jnp.where(ok[:, :, None, None, None], kk, 0.0)
        m_tot = m_tot + kk.transpose(2, 0, 4, 1, 3).reshape(g_, lc * hsz, lc * hsz)
        pw = apow[(lc - 1 - s_idx) if d == 0 else s_idx]
        wc = pw[:, :, :, None] * bf[None]
        wc = wc.transpose(1, 0, 3, 2).reshape(g_, lc * hsz, p_)
        pv = apow[(s_idx + 1) if d == 0 else (lc - s_idx)]
        vc = cmat[None] * pv[:, :, None, :]
        vc = vc.transpose(1, 3, 0, 2).reshape(g_, p_, lc * hsz)
        pad_w = jnp.zeros((g_, lc * hsz, LANE - p_), F32)
        pad_v = jnp.zeros((g_, LANE - p_, lc * hsz), F32)
        w_parts += [jnp.real(wc), pad_w, jnp.imag(wc), pad_w]
        v_parts += [jnp.real(vc), pad_v, -jnp.imag(vc), pad_v]
        a_chunk = apow[lc]
        pad_a = jnp.zeros((g_, LANE - p_), F32)
        a_parts += [jnp.concatenate([jnp.real(a_chunk), pad_a], axis=1),
                    jnp.concatenate([jnp.imag(a_chunk), pad_a], axis=1)]
    w_all = jnp.concatenate(w_parts, axis=2)
    v_all = jnp.concatenate(v_parts, axis=1)
    a_all = jnp.stack(a_parts + [jnp.zeros_like(a_parts[0])] * 4, axis=1)
    return m_tot.astype(BF16), w_all.astype(BF16), v_all.astype(BF16), a_all


def _s5_kernel(*refs, nb, n_lat, n_ctx, hsz):
    lc = S5_CHUNK
    x_refs = refs[:lc]
    m_ref, w_ref, v_ref, a_ref, o_ref = refs[lc:lc + 5]
    z_refs = refs[lc + 5:lc + 9]
    sp_refs = refs[lc + 9:lc + 13]
    k = pl.program_id(0) % (LANE // hsz)
    cw = lc * hsz
    ri = lax.broadcasted_iota(jnp.int32, (lc * LANE, cw), 0)
    ci = lax.broadcasted_iota(jnp.int32, (lc * LANE, cw), 1)
    sel = jnp.where(ri == (ci // hsz) * LANE + k * hsz + ci % hsz, 1.0, 0.0).astype(BF16)
    ri_t = lax.broadcasted_iota(jnp.int32, (cw, lc * LANE), 0)
    ci_t = lax.broadcasted_iota(jnp.int32, (cw, lc * LANE), 1)
    sel_t = jnp.where(ci_t == (ri_t // hsz) * LANE + k * hsz + ri_t % hsz, 1.0, 0.0).astype(BF16)

    x = jnp.concatenate([r[...] for r in x_refs], axis=1).astype(BF16)
    ub = jnp.dot(x, sel, preferred_element_type=F32).astype(BF16)
    z = jnp.dot(ub, w_ref[...], preferred_element_type=F32)
    for q in range(4):
        z_refs[q][...] = z[:, q * LANE:(q + 1) * LANE]
    n_chunks = n_lat + n_ctx

    def rows_of(chunk):
        if chunk < n_lat:
            return pl.ds(chunk, nb, stride=n_lat)
        return pl.ds(nb * n_lat + chunk - n_lat, nb, stride=n_ctx)

    a = a_ref[...]
    coef = [(a[2 * d:2 * d + 1, :], a[2 * d + 1:2 * d + 2, :]) for d in range(2)]
    state = [(jnp.zeros((nb, LANE), F32), jnp.zeros((nb, LANE), F32)) for _ in range(2)]
    for t in range(n_chunks):
        chunk = ((t + n_lat) % n_chunks, n_chunks - 1 - t)
        for d in range(2):
            rs = rows_of(chunk[d])
            a_r, a_i = coef[d]
            s_re, s_im = state[d]
            sp_refs[2 * d][rs, :] = s_re
            sp_refs[2 * d + 1][rs, :] = s_im
            z_re = z_refs[2 * d][rs, :]
            z_im = z_refs[2 * d + 1][rs, :]
            state[d] = (a_r * s_re - a_i * s_im + z_re, a_r * s_im + a_i * s_re + z_im)
    sp = jnp.concatenate([r[...] for r in sp_refs], axis=1)
    y = (jnp.dot(ub, m_ref[...], preferred_element_type=F32)
         + jnp.dot(sp.astype(BF16), v_ref[...], preferred_element_type=F32))
    y_hi = y.astype(BF16)
    y_lo = (y - y_hi.astype(F32)).astype(BF16)
    back = (jnp.dot(y_hi, sel_t, preferred_element_type=F32)
            + jnp.dot(y_lo, sel_t, preferred_element_type=F32))

    @pl.when(k == 0)
    def _():
        o_ref[...] = jnp.zeros_like(o_ref)

    for s in range(lc):
        o_ref[s] += back[:, s * LANE:(s + 1) * LANE]


def _s5_scan(p2v, tables, *, nb, n_lat, n_ctx, n_groups, hsz, row_w):
    m_tot, w_all, v_all, a_all = tables
    rows = p2v.shape[0]
    lc = S5_CHUNK
    cw = lc * hsz
    gpt = LANE // hsz
    tiles_per_tok = row_w // LANE

    def x_spec(s):
        return pl.BlockSpec((rows, LANE), lambda g: (0, s * tiles_per_tok + g // gpt))

    return pl.pallas_call(
        functools.partial(_s5_kernel, nb=nb, n_lat=n_lat, n_ctx=n_ctx, hsz=hsz),
        out_shape=jax.ShapeDtypeStruct((lc, rows, n_groups * hsz), F32),
        grid=(n_groups,),
        in_specs=[x_spec(s) for s in range(lc)] + [
            pl.BlockSpec((None, cw, cw), lambda g: (g, 0, 0)),
            pl.BlockSpec((None, cw, 4 * LANE), lambda g: (g, 0, 0)),
            pl.BlockSpec((None, 4 * LANE, cw), lambda g: (g, 0, 0)),
            pl.BlockSpec((None, 8, LANE), lambda g: (g, 0, 0))],
        out_specs=pl.BlockSpec((lc, rows, LANE), lambda g: (0, 0, g // gpt)),
        scratch_shapes=[pltpu.VMEM((rows, LANE), F32)] * 8,
        compiler_params=_cparams("arbitrary"),
        name="s5_scan",
    )(*([p2v] * lc), m_tot, w_all, v_all, a_all)


def _glu_kernel(y_ref, p_ref, d_ref, w_ref, b_ref, o_ref):
    width = y_ref.shape[1]
    g = jax.nn.gelu(y_ref[...] + d_ref[...] * p_ref[:, :width])
    z = jnp.dot(g.astype(BF16), w_ref[...], preferred_element_type=F32) + b_ref[...]
    o_ref[...] = (g * jax.nn.sigmoid(z)).astype(o_ref.dtype)


def _glu(y, p2, d_skip, glu_w, glu_b, tm):
    r, w = y.shape
    return pl.pallas_call(
        _glu_kernel,
        out_shape=jax.ShapeDtypeStruct((r, w), BF16),
        grid=(r // tm,),
        in_specs=[pl.BlockSpec((tm, w), lambda i: (i, 0)),
                  pl.BlockSpec((tm, p2.shape[1]), lambda i: (i, 0)),
                  pl.BlockSpec((1, w), lambda i: (0, 0)),
                  pl.BlockSpec((w, w), lambda i: (0, 0)),
                  pl.BlockSpec((1, w), lambda i: (0, 0))],
        out_specs=pl.BlockSpec((tm, w), lambda i: (i, 0)),
        compiler_params=_cparams("parallel"),
        name="s5_glu",
    )(y, p2, d_skip.astype(F32).reshape(1, w), glu_w.astype(BF16), glu_b.astype(F32).reshape(1, w))


def _na_variants(rows, kh):
    n_blk = rows // NA_QROWS
    out = []
    for i in (0, min(1, n_blk - 1), n_blk - 1):
        out.append((i, min(max(i * NA_QROWS - kh // 2, 0), rows - NA_KROWS)))
    return out


def _na_tables(rpb, rows):
    n_heads, nri, nci = rpb.shape
    win_h, win_w = (nri + 1) // 2, (nci + 1) // 2
    kh = min(win_h, rows)
    col = np.arange(GRID_W)
    col_start = np.clip(col - win_w // 2, 0, GRID_W - win_w)
    col_ok = (col[None, :] >= col_start[:, None]) & (col[None, :] < col_start[:, None] + win_w)
    ci = np.clip(col[None, :] - col[:, None], -(win_w - 1), win_w - 1) + (win_w - 1)
    onehot = (ci[None] == np.arange(nci)[:, None, None]).astype(np.float32)
    t1 = jnp.einsum('hrc,cqk->hrqk', rpb.astype(F32), onehot, precision=lax.Precision.HIGHEST)
    t1 = jnp.where(col_ok[None, None], t1, -jnp.inf)
    dummy = jnp.full((n_heads, 1, GRID_W, GRID_W), -jnp.inf, F32)
    t1 = jnp.concatenate([dummy, t1, dummy], axis=1)
    t2 = jnp.concatenate([t1[:, :nri + 1], t1[:, 1:nri + 2]], axis=-1)
    rmask = np.zeros((3, NA_QROWS, NA_KROWS // 2, 1, 2 * GRID_W), np.float32)
    for v, (i, ks) in enumerate(_na_variants(rows, kh)):
        for a in range(NA_QROWS):
            r_start = min(max(i * NA_QROWS + a - kh // 2, 0), rows - kh)
            for b in range(NA_KROWS):
                if not (r_start <= ks + b < r_start + kh):
                    rmask[v, a, b // 2, 0, (b % 2) * GRID_W:(b % 2 + 1) * GRID_W] = -np.inf
    return t2, jnp.asarray(rmask)


def _na_kernel(q_ref, k_ref, v_ref, kc_ref, vc_ref, t2_ref, rm_ref, o_ref, bias_ref, *, rows, kh, win_h,
               scale):
    n_blk = rows // NA_QROWS
    tq = NA_QROWS * GRID_W
    tk = NA_KROWS * GRID_W
    n_r = t2_ref.shape[0]

    @pl.when(pl.program_id(1) == 0)
    def _():
        for v, (i, ks) in enumerate(_na_variants(rows, kh)):
            for a in range(NA_QROWS):
                for pair in range(NA_KROWS // 2):
                    r = (ks + 2 * pair) - (i * NA_QROWS + a) + (win_h - 1)
                    tile = t2_ref[min(max(r + 1, 0), n_r - 1)] + rm_ref[v, a, pair]
                    bias_ref[v, a * GRID_W:(a + 1) * GRID_W, pair * 2 * GRID_W:(pair + 1) * 2 * GRID_W] = tile

    kc = kc_ref[...]
    vc = vc_ref[...]
    nt = (((1,), (1,)), ((), ()))

    def body(i, carry):
        q0 = pl.multiple_of(i * tq, tq)
        ks = jnp.clip(i * NA_QROWS - kh // 2, 0, rows - NA_KROWS)
        k0 = pl.multiple_of(ks * GRID_W, 4 * GRID_W)
        variant = jnp.where(i == 0, 0, jnp.where(i == n_blk - 1, 2, 1))
        qb = q_ref[pl.ds(q0, tq), :]
        kb = k_ref[pl.ds(k0, tk), :]
        vb = v_ref[pl.ds(k0, tk), :]
        s_loc = lax.dot_general(qb, kb, nt, preferred_element_type=F32) * scale + bias_ref[variant]
        s_ctx = lax.dot_general(qb, kc, nt, preferred_element_type=F32) * scale
        m = jnp.maximum(jnp.max(s_loc, axis=1, keepdims=True), jnp.max(s_ctx, axis=1, keepdims=True))
        p_loc = jnp.exp(s_loc - m)
        p_ctx = jnp.exp(s_ctx - m)
        denom = jnp.sum(p_loc, axis=1, keepdims=True) + jnp.sum(p_ctx, axis=1, keepdims=True)
        acc = (jnp.dot(p_loc.astype(BF16), vb, preferred_element_type=F32)
               + jnp.dot(p_ctx.astype(BF16), vc, preferred_element_type=F32))
        o_ref[pl.ds(q0, tq), :] = (acc / denom).astype(o_ref.dtype)
        return carry

    lax.fori_loop(0, n_blk, body, 0, unroll=NA_UNROLL)


def _na_attention(qkv, t2, rmask, *, n_batch, seq, ctx_len, n_heads, hd, kh, win_h):
    rows = seq // GRID_W
    lat_ctx_blk = (n_batch * seq) // ctx_len
    tq, tk = NA_QROWS * GRID_W, NA_KROWS * GRID_W
    return pl.pallas_call(
        functools.partial(_na_kernel, rows=rows, kh=kh, win_h=win_h, scale=hd ** -0.5),
        out_shape=jax.ShapeDtypeStruct((n_heads, n_batch * seq, hd), BF16),
        grid=(n_heads, n_batch),
        in_specs=[pl.BlockSpec((None, seq, hd), lambda h, b: (h, b, 0)),
                  pl.BlockSpec((None, seq, hd), lambda h, b: (n_heads + h, b, 0)),
                  pl.BlockSpec((None, seq, hd), lambda h, b: (2 * n_heads + h, b, 0)),
                  pl.BlockSpec((None, ctx_len, hd), lambda h, b: (n_heads + h, lat_ctx_blk + b, 0)),
                  pl.BlockSpec((None, ctx_len, hd), lambda h, b: (2 * n_heads + h, lat_ctx_blk + b, 0)),
                  pl.BlockSpec((None,) + t2.shape[1:], lambda h, b: (h, 0, 0, 0)),
                  pl.BlockSpec(rmask.shape, lambda h, b: (0, 0, 0, 0, 0))],
        out_specs=pl.BlockSpec((None, seq, hd), lambda h, b: (h, b, 0)),
        scratch_shapes=[pltpu.VMEM((3, tq, tk), F32)],
        compiler_params=_cparams("arbitrary", "arbitrary"),
        name="na_attention",
    )(qkv, qkv, qkv, qkv, qkv, t2, rmask)


def _ffn_kernel(hx_ref, wg_ref, wa_ref, cw_ref, cb_ref, wd_ref, x_ref, gate_ref, o_ref, *,
                tm, n_lat_rows, seq, ctx_len):
    i = pl.program_id(0)
    j = pl.program_id(1)

    @pl.when(j == 0)
    def _():
        o_ref[...] = jnp.zeros_like(o_ref)

    g_all = jnp.dot(hx_ref[...], wg_ref[...], preferred_element_type=F32)
    a = jnp.dot(hx_ref[:tm, :], wa_ref[...], preferred_element_type=F32)
    g = g_all[:tm]
    local = lax.broadcasted_iota(jnp.int32, (tm, 1), 0)
    grow = i * tm + local
    in_lat = grow < n_lat_rows
    pos = jnp.where(in_lat, grow % seq, (grow - n_lat_rows) % ctx_len)
    last = jnp.where(in_lat, seq - 1, ctx_len - 1)
    g_prev = jnp.where(local == 0, g_all[tm + 7:tm + 8, :], pltpu.roll(g, 1, axis=0))
    g_prev = jnp.where(pos == 0, 0.0, g_prev)
    g_next = jnp.where(local == tm - 1, g_all[tm + 8:tm + 9, :], pltpu.roll(g, tm - 1, axis=0))
    g_next = jnp.where(pos == last, 0.0, g_next)
    cw = cw_ref[...]
    gl = jax.nn.gelu(cw[0:1, :] * g_prev + cw[1:2, :] * g + cw[2:3, :] * g_next + cb_ref[...])
    o_ref[...] += jnp.dot((gl * a).astype(BF16), wd_ref[...], preferred_element_type=F32)

    @pl.when(j == pl.num_programs(1) - 1)
    def _():
        o_ref[...] = x_ref[...] + gate_ref[...] * o_ref[...]


def _conv_ffn(hx, xs, weights, mods, gate_idx, seg, *, tm, tf, n_lat_rows, seq, ctx_len):
    w_g, w_a, cw8, cb, w_down = weights
    n_blk, _, d = hx.shape
    r = n_blk * tm
    dffp = w_down.shape[0]
    nj = dffp // tf
    once = pl.Buffered(1)
    return pl.pallas_call(
        functools.partial(_ffn_kernel, tm=tm, n_lat_rows=n_lat_rows, seq=seq, ctx_len=ctx_len),
        out_shape=jax.ShapeDtypeStruct((r, d), F32),
        grid=(n_blk, nj),
        in_specs=[pl.BlockSpec((None, tm + 16, d), lambda i, j: (i, 0, 0), pipeline_mode=once),
                  pl.BlockSpec((d, tf), lambda i, j: (0, j)),
                  pl.BlockSpec((d, tf), lambda i, j: (0, j)),
                  pl.BlockSpec((8, tf), lambda i, j: (0, j)),
                  pl.BlockSpec((1, tf), lambda i, j: (0, j)),
                  pl.BlockSpec((tf, d), lambda i, j: (j, 0)),
                  pl.BlockSpec((tm, d), lambda i, j: (i, 0), pipeline_mode=once),
                  pl.BlockSpec((None, None, 1, d), lambda i, j: (seg(i), gate_idx, 0, 0))],
        out_specs=pl.BlockSpec((tm, d), lambda i, j: (i, 0), pipeline_mode=once),
        compiler_params=_cparams("parallel", "arbitrary"),
        name="conv_ffn",
    )(hx, w_g, w_a, cw8, cb, w_down, xs, mods)


def _pad_ffn_weights(w_up, conv_w, conv_b, w_down, tf):
    d = w_up.shape[0]
    dff = w_down.shape[0]
    pad = -dff % tf
    zc = jnp.zeros((d, pad), BF16)
    w_a = jnp.concatenate([w_up[:, :dff].astype(BF16), zc], axis=1)
    w_g = jnp.concatenate([w_up[:, dff:].astype(BF16), zc], axis=1)
    cw8 = jnp.pad(conv_w.astype(F32), ((0, 8 - conv_w.shape[0]), (0, pad)))
    cb = jnp.pad(conv_b.astype(F32), (0, pad)).reshape(1, dff + pad)
    w_down_p = jnp.concatenate([w_down.astype(BF16), jnp.zeros((pad, d), BF16)], axis=0)
    return w_g, w_a, cw8, cb, w_down_p


def kernel(x, c, ctx, c_ctx, mod_w, mod_b, norm_mix_g, norm_ffn_g, ab_w_in, mlstm_gate_b, mlstm_head_g, s5_a_re, s5_a_im, s5_log_dt, s5_b_re, s5_b_im, s5_c_re, s5_c_im, s5_d, s5_glu_w, s5_glu_b, ab_w_out, na_w_qkv, na_rpb, na_w_out, ffn_w_up, ffn_conv_w, ffn_conv_b, ffn_w_down, final_norm_g):
    n_batch, seq, d = x.shape
    ctx_len = ctx.shape[1]
    depth = mod_w.shape[0]
    assert depth == 2 and mod_w.shape[2] == 6 * d
    n_lat_rows = n_batch * seq
    n_ctx_rows = n_batch * ctx_len
    assert seq % MLSTM_CHUNK == 0 and ctx_len == MLSTM_CHUNK and seq % GRID_W == 0

    n_heads = mlstm_gate_b.shape[-1]
    s5w = s5_d.shape[-1]
    mw = d - s5w
    dv = mw // n_heads
    dqk = dv // 2
    qkw = n_heads * dqk
    n_groups, p_state, s5_group = s5_b_re.shape[1:]
    assert S5_CHUNK * s5_group == 2 * LANE and p_state <= LANE and dqk % (2 * LANE) == 0
    na_heads = na_rpb.shape[1]
    hd = d // na_heads
    rows = seq // GRID_W
    kh = min((na_rpb.shape[2] + 1) // 2, rows)
    assert rows % NA_QROWS == 0 and rows >= NA_KROWS and hd == LANE
    dff = ffn_w_down.shape[1]

    tm_big = _pick(math.gcd(n_lat_rows, n_ctx_rows), (1024, 512, 256))
    tm_ffn = _pick(math.gcd(n_lat_rows, n_ctx_rows), (512, 256))
    tf = 512
    seg_big = _seg_fn(tm_big, n_lat_rows, seq, n_batch)
    seg_ffn = _seg_fn(tm_ffn, n_lat_rows, seq, n_batch)
    seg_256 = _seg_fn(256, n_lat_rows, seq, n_batch)

    cond8 = jnp.concatenate([c, c_ctx[None], jnp.zeros((8 - n_batch - 1, d), F32)], axis=0)
    mods = _ada(cond8, mod_w, mod_b).reshape(depth, 8, 6, 1, d)

    xs = jnp.concatenate([x.reshape(n_lat_rows, d), ctx.reshape(n_ctx_rows, d)], axis=0)

    h = _modulate(xs, norm_mix_g[0], mods[0], 0, 1, seg_256, 256)
    w_in = ab_w_in[0]
    n_gate = 4 * n_heads
    w1 = w_in[:, :2 * qkw + 2 * mw].astype(BF16)
    w2 = jnp.concatenate([w_in[:, 2 * qkw + 2 * mw + n_gate:], w_in[:, 2 * qkw + 2 * mw:2 * qkw + 2 * mw + n_gate],
                          jnp.zeros((d, LANE - n_gate), F32)], axis=1).astype(BF16)
    p1 = _matmul(h, w1, BF16, tm_big, _pick(w1.shape[1], (512, 256, 128)), "ab_in_proj")
    p2 = _matmul(h, w2, F32, tm_ffn, w2.shape[1], "ab_in_proj_s5")
    grow = jnp.pad(p2[:, s5w:s5w + n_gate].T, ((0, -n_gate % 8), (0, 0)))

    cos_t, sin_t = _rope_tables(seq, ctx_len, dqk)
    common = dict(n_batch=n_batch, seq=seq, n_heads=n_heads, dqk=dqk, dv=dv)
    hf = _mlstm_direction(p1, p2, grow, mlstm_gate_b[0], cos_t, sin_t, None, None, reverse=False, **common)
    mix_a = _mlstm_direction(p1, p2, grow, mlstm_gate_b[0], cos_t, sin_t, mlstm_head_g[0], hf,
                             reverse=True, **common)

    n_lat_ch, n_ctx_ch = seq // S5_CHUNK, ctx_len // S5_CHUNK
    row_w = p2.shape[1]
    tables = _s5_tables(s5_a_re[0], s5_a_im[0], s5_log_dt[0], s5_b_re[0], s5_b_im[0], s5_c_re[0], s5_c_im[0])
    y16 = _s5_scan(p2.reshape(-1, S5_CHUNK * row_w), tables, nb=n_batch, n_lat=n_lat_ch, n_ctx=n_ctx_ch,
                   n_groups=n_groups, hsz=s5_group, row_w=row_w)
    y_rows = y16.transpose(1, 0, 2).reshape(-1, s5w)
    mix_b = _glu(y_rows, p2, s5_d[0], s5_glu_w[0], s5_glu_b[0], tm_ffn)

    w_out = ab_w_out[0].astype(BF16)
    xs = _matmul_residual([mix_a, mix_b], [w_out[:mw], w_out[mw:]], xs, mods[0], 2, seg_big,
                          tm_big, 512, "ab_out_proj")
    h = _modulate_halo(xs, norm_ffn_g[0], mods[0], 3, 4, seg_ffn, tm_ffn)
    ffn_kw = dict(tm=tm_ffn, tf=tf, n_lat_rows=n_lat_rows, seq=seq, ctx_len=ctx_len)
    xs = _conv_ffn(h, xs, _pad_ffn_weights(ffn_w_up[0], ffn_conv_w[0], ffn_conv_b[0], ffn_w_down[0], tf),
                   mods[0], 5, seg_ffn, **ffn_kw)

    h = _modulate(xs, norm_mix_g[1], mods[1], 0, 1, seg_256, 256)
    qkv = _matmul(h, na_w_qkv[0].astype(BF16), BF16, tm_big, 512, "na_qkv_proj", unit=hd)
    t2, rmask = _na_tables(na_rpb[0], rows)
    o = _na_attention(qkv, t2, rmask, n_batch=n_batch, seq=seq, ctx_len=ctx_len, n_heads=na_heads, hd=hd,
                      kh=kh, win_h=(na_rpb.shape[2] + 1) // 2)
    xl = _matmul_residual([o], [na_w_out[0].astype(BF16)], xs, mods[1], 2, seg_big, tm_big, 512,
                          "na_out_proj")
    h = _modulate_halo(xl, norm_ffn_g[1], mods[1], 3, 4, seg_ffn, tm_ffn)
    xl = _conv_ffn(h, xl, _pad_ffn_weights(ffn_w_up[1], ffn_conv_w[1], ffn_conv_b[1], ffn_w_down[1], tf),
                   mods[1], 5, seg_ffn, **ffn_kw)
    return _rmsnorm(xl, final_norm_g, 256).reshape(n_batch, seq, d)
```

```python
import functools
import math

import jax
import jax.numpy as jnp
import numpy as np
from jax import lax
from jax.experimental import pallas as pl
from jax.experimental.pallas import tpu as pltpu

F32 = jnp.float32
BF16 = jnp.bfloat16

EPS = 1e-6
ROPE_BASE = 10000.0
GRID_W = 64
MLSTM_CHUNK = 256
S5_CHUNK = 16
NA_QROWS = 4
NA_KROWS = 12
NA_UNROLL = 4
LOG2E = 1.4426950408889634
LANE = 128
VMEM_LIMIT_BYTES = 58 * 1024 * 1024


def _cparams(*sem):
    return pltpu.CompilerParams(dimension_semantics=sem, vmem_limit_bytes=VMEM_LIMIT_BYTES)


def _pick(n, candidates):
    for c in candidates:
        if n % c == 0:
            return c
    raise ValueError(f"no tile in {candidates} divides {n}")


def _ada_kernel(c_ref, w_ref, b_ref, o_ref):
    s = jax.nn.silu(c_ref[...]).astype(BF16)
    o_ref[...] = jnp.dot(s, w_ref[...].astype(BF16), preferred_element_type=F32) + b_ref[...]


def _ada(cond8, mod_w, mod_b):
    depth, d, n6 = mod_w.shape
    tn = _pick(n6, (512, 256, 128))
    return pl.pallas_call(
        _ada_kernel,
        out_shape=jax.ShapeDtypeStruct((depth, 8, n6), F32),
        grid=(depth, n6 // tn),
        in_specs=[pl.BlockSpec((8, d), lambda l, j: (0, 0)),
                  pl.BlockSpec((None, d, tn), lambda l, j: (l, 0, j)),
                  pl.BlockSpec((None, 1, tn), lambda l, j: (l, 0, j))],
        out_specs=pl.BlockSpec((None, 8, tn), lambda l, j: (l, 0, j)),
        compiler_params=_cparams("parallel", "parallel"),
        name="ada_mod",
    )(cond8, mod_w, mod_b.reshape(depth, 1, n6))


def _seg_fn(tm, n_lat_rows, seq, n_batch):
    n_lat_blk = n_lat_rows // tm

    def seg(i):
        return jnp.where(i < n_lat_blk, (i * tm) // seq, n_batch)
    return seg


def _two_source_specs(block, n_first_blk, col=None):
    if col is None:
        return (pl.BlockSpec(block, lambda i: (jnp.minimum(i, n_first_blk - 1), 0)),
                pl.BlockSpec(block, lambda i: (jnp.maximum(i - n_first_blk, 0), 0)))
    return (pl.BlockSpec(block, lambda i, j: (jnp.minimum(i, n_first_blk - 1), jnp.where(i < n_first_blk, j, 0))),
            pl.BlockSpec(block, lambda i, j: (jnp.maximum(i - n_first_blk, 0), jnp.where(i < n_first_blk, 0, j))))


def _modulate2_kernel(x_ref, c_ref, g_ref, sh_ref, sc_ref, o_ref, *, n_first_blk):
    x = jnp.where(pl.program_id(0) < n_first_blk, x_ref[...], c_ref[...])
    y = x * lax.rsqrt(jnp.mean(x * x, axis=-1, keepdims=True) + EPS) * g_ref[...]
    o_ref[...] = (y * (1.0 + sc_ref[...]) + sh_ref[...]).astype(o_ref.dtype)


def _modulate2(x2d, c2d, gain, mods, shift_idx, scale_idx, seg, tm):
    d = x2d.shape[1]
    r = x2d.shape[0] + c2d.shape[0]
    n_first_blk = x2d.shape[0] // tm
    return pl.pallas_call(
        functools.partial(_modulate2_kernel, n_first_blk=n_first_blk),
        out_shape=jax.ShapeDtypeStruct((r, d), BF16),
        grid=(r // tm,),
        in_specs=[*_two_source_specs((tm, d), n_first_blk),
                  pl.BlockSpec((1, d), lambda i: (0, 0)),
                  pl.BlockSpec((None, None, 1, d), lambda i: (seg(i), shift_idx, 0, 0)),
                  pl.BlockSpec((None, None, 1, d), lambda i: (seg(i), scale_idx, 0, 0))],
        out_specs=pl.BlockSpec((tm, d), lambda i: (i, 0)),
        compiler_params=_cparams("parallel"),
        name="modulate",
    )(x2d, c2d, gain.reshape(1, d), mods, mods)


def _modulate_halo_kernel(x_ref, xp_ref, xn_ref, g_ref, sh_ref, sc_ref, o_ref):
    x = jnp.concatenate([x_ref[...], xp_ref[...], xn_ref[...]], axis=0)
    y = x * lax.rsqrt(jnp.mean(x * x, axis=-1, keepdims=True) + EPS) * g_ref[...]
    o_ref[...] = (y * (1.0 + sc_ref[...]) + sh_ref[...]).astype(o_ref.dtype)


def _modulate_halo(xs, gain, mods, shift_idx, scale_idx, seg, tm):
    r, d = xs.shape
    n_blk, per8 = r // tm, tm // 8
    return pl.pallas_call(
        _modulate_halo_kernel,
        out_shape=jax.ShapeDtypeStruct((n_blk, tm + 16, d), BF16),
        grid=(n_blk,),
        in_specs=[pl.BlockSpec((tm, d), lambda i: (i, 0)),
                  pl.BlockSpec((8, d), lambda i: (jnp.maximum(i * per8 - 1, 0), 0)),
                  pl.BlockSpec((8, d), lambda i: (jnp.minimum((i + 1) * per8, r // 8 - 1), 0)),
                  pl.BlockSpec((1, d), lambda i: (0, 0)),
                  pl.BlockSpec((None, None, 1, d), lambda i: (seg(i), shift_idx, 0, 0)),
                  pl.BlockSpec((None, None, 1, d), lambda i: (seg(i), scale_idx, 0, 0))],
        out_specs=pl.BlockSpec((None, tm + 16, d), lambda i: (i, 0, 0)),
        compiler_params=_cparams("parallel"),
        name="modulate_halo",
    )(xs, xs, xs, gain.reshape(1, d), mods, mods)


def _mm_kernel(a_ref, b_ref, o_ref):
    res = jnp.dot(a_ref[...], b_ref[...], preferred_element_type=F32).astype(o_ref.dtype)
    if len(o_ref.shape) == 3:
        w = o_ref.shape[2]
        for t in range(o_ref.shape[0]):
            o_ref[t] = res[:, t * w:(t + 1) * w]
    else:
        o_ref[...] = res


def _matmul(a, b, out_dtype, tm, tn, name, unit=None):
    m, k = a.shape
    n = b.shape[1]
    if unit is None:
        out_shape = jax.ShapeDtypeStruct((m, n), out_dtype)
        out_spec = pl.BlockSpec((tm, tn), lambda i, j: (i, j))
    else:
        out_shape = jax.ShapeDtypeStruct((n // unit, m, unit), out_dtype)
        out_spec = pl.BlockSpec((tn // unit, tm, unit), lambda i, j: (j, i, 0))
    return pl.pallas_call(
        _mm_kernel,
        out_shape=out_shape,
        grid=(m // tm, n // tn),
        in_specs=[pl.BlockSpec((tm, k), lambda i, j: (i, 0)),
                  pl.BlockSpec((k, tn), lambda i, j: (0, j))],
        out_specs=out_spec,
        compiler_params=_cparams("parallel", "parallel"),
        name=name,
    )(a, b)


def _rows_by_unit(ref):
    if len(ref.shape) == 2:
        return ref[...]
    return jnp.concatenate([ref[t] for t in range(ref.shape[0])], axis=1)


def _mm_res_kernel(*refs, n_a, n_first_blk):
    a_refs, b_refs = refs[:n_a], refs[n_a:2 * n_a]
    if n_first_blk is None:
        x_ref, gate_ref, o_ref = refs[2 * n_a:]
        x = x_ref[...]
    else:
        x_ref, c_ref, gate_ref, o_ref = refs[2 * n_a:]
        x = jnp.where(pl.program_id(0) < n_first_blk, x_ref[...], c_ref[...])
    acc = jnp.dot(_rows_by_unit(a_refs[0]), b_refs[0][...], preferred_element_type=F32)
    for a_ref, b_ref in zip(a_refs[1:], b_refs[1:]):
        acc = acc + jnp.dot(_rows_by_unit(a_ref), b_ref[...], preferred_element_type=F32)
    o_ref[...] = x + gate_ref[...] * acc


def _matmul_residual(a_list, b_list, xs, mods, gate_idx, seg, tm, tn, name):
    m = a_list[0].shape[-2]
    n = b_list[0].shape[1]

    def a_spec(a):
        if a.ndim == 2:
            return pl.BlockSpec((tm, a.shape[1]), lambda i, j: (i, 0))
        return pl.BlockSpec((a.shape[0], tm, a.shape[2]), lambda i, j: (0, i, 0))

    if isinstance(xs, tuple):
        n_first_blk = xs[0].shape[0] // tm
        x_specs = list(_two_source_specs((tm, tn), n_first_blk, col=True))
        x_args = list(xs)
    else:
        n_first_blk = None
        x_specs = [pl.BlockSpec((tm, tn), lambda i, j: (i, j))]
        x_args = [xs]
    in_specs = ([a_spec(a) for a in a_list]
                + [pl.BlockSpec((b.shape[0], tn), lambda i, j: (0, j)) for b in b_list]
                + x_specs
                + [pl.BlockSpec((None, None, 1, tn), lambda i, j: (seg(i), gate_idx, 0, j))])
    return pl.pallas_call(
        functools.partial(_mm_res_kernel, n_a=len(a_list), n_first_blk=n_first_blk),
        out_shape=jax.ShapeDtypeStruct((m, n), F32),
        grid=(m // tm, n // tn),
        in_specs=in_specs,
        out_specs=pl.BlockSpec((tm, tn), lambda i, j: (i, j)),
        compiler_params=_cparams("parallel", "parallel"),
        name=name,
    )(*a_list, *b_list, *x_args, mods)


def _rope(x, cos, sin):
    parts = []
    for j in range(x.shape[1] // LANE):
        sl = slice(j * LANE, (j + 1) * LANE)
        xs = x[:, sl]
        parts.append(xs * cos[:, sl] + pltpu.roll(xs, LANE // 2, axis=1) * sin[:, sl])
    return jnp.concatenate(parts, axis=1)


def _mlstm_kernel(gb_ref, q_ref, k_ref, v_ref, cos_ref, sin_ref, gcol_ref, grow_ref, *rest,
                  reverse, n_heads, scale):
    if reverse:
        o_ref, hf_ref, hg_ref, out_ref, ct_ref, m_ref = rest
    else:
        out_ref, ct_ref, m_ref = rest
    step = pl.program_id(1)
    d = 1 if reverse else 0
    L = q_ref.shape[0]
    dqk = q_ref.shape[1] // n_heads
    dv = v_ref.shape[1] // n_heads

    @pl.when(step == 0)
    def _():
        ct_ref[...] = jnp.zeros_like(ct_ref)
        m_ref[...] = jnp.zeros_like(m_ref)

    row = lax.broadcasted_iota(jnp.int32, (L, L), 0)
    col = lax.broadcasted_iota(jnp.int32, (L, L), 1)
    incl = (col >= row) if reverse else (col <= row)
    incl_t = (row >= col) if reverse else (row <= col)
    n_lat = pl.num_programs(1) - 1
    tab = jnp.where(step == 0, n_lat, (n_lat - step) if reverse else (step - 1))
    t0 = pl.multiple_of(tab * L, L)
    cos = cos_ref[pl.ds(t0, L), :]
    sin = sin_ref[pl.ds(t0, L), :]
    ones = jnp.ones((L, LANE), v_ref.dtype)

    for hh in range(n_heads):
        gi = (d * 2 + 0) * n_heads + hh
        gf = (d * 2 + 1) * n_heads + hh
        b_i = gb_ref[gi]
        b_f = gb_ref[gf]
        li_col = gcol_ref[:, gi:gi + 1] + b_i
        lf_col = jax.nn.log_sigmoid(gcol_ref[:, gf:gf + 1] + b_f)
        li_row = grow_ref[gi:gi + 1, :] + b_i
        lf_row = jax.nn.log_sigmoid(grow_ref[gf:gf + 1, :] + b_f)
        cum_col = jnp.sum(jnp.where(incl, lf_row, 0.0), axis=1, keepdims=True)
        cum_row = jnp.sum(jnp.where(incl_t, lf_col, 0.0), axis=0, keepdims=True)
        total = jnp.sum(lf_row, axis=1, keepdims=True)
        m_prev = m_ref[hh, 0:1, 0:1]

        dmat = jnp.where(incl, cum_col - cum_row + li_row, -jnp.inf)
        carried = cum_col + m_prev
        m_loc = jnp.maximum(carried, jnp.max(dmat, axis=1, keepdims=True))
        w = jnp.exp(dmat - m_loc)
        w_state = jnp.exp(carried - m_loc)

        qs = slice(hh * dqk, (hh + 1) * dqk)
        vs = slice(hh * dv, (hh + 1) * dv)
        qb = (_rope(q_ref[:, qs].astype(F32), cos, sin) * scale).astype(BF16)
        kb = _rope(k_ref[:, qs].astype(F32), cos, sin).astype(BF16)
        s_qk = lax.dot_general(qb, kb, (((1,), (1,)), ((), ())), preferred_element_type=F32)
        sc = (s_qk * w).astype(BF16)
        vext = jnp.concatenate([v_ref[:, vs], ones], axis=1)
        ct = ct_ref[hh]
        res = (jnp.dot(sc, vext, preferred_element_type=F32)
               + w_state * jnp.dot(qb, ct.astype(BF16), preferred_element_type=F32))
        den = res[:, dv:dv + 1]
        hval = res[:, :dv] / jnp.maximum(jnp.abs(den), jnp.exp(-m_loc))

        src = total - cum_col + li_col
        m_new = jnp.maximum(total + m_prev, jnp.max(src, axis=0, keepdims=True))
        decay = jnp.exp(total + m_prev - m_new)
        w_src = jnp.exp(src - m_new)
        upd = lax.dot_general(kb, (w_src * vext.astype(F32)).astype(BF16), (((0,), (0,)), ((), ())),
                              preferred_element_type=F32)
        ct_ref[hh] = decay * ct + upd
        m_ref[hh] = jnp.broadcast_to(m_new, m_ref.shape[1:])

        if reverse:
            hm = hf_ref[:, vs] + hval
            hn = hm * lax.rsqrt(jnp.mean(hm * hm, axis=-1, keepdims=True) + EPS) * hg_ref[:, vs]
            out_ref[:, vs] = (jax.nn.sigmoid(o_ref[:, vs].astype(F32)) * hn).astype(out_ref.dtype)
        else:
            out_ref[:, vs] = hval


def _mlstm_direction(p1, p2, grow, gate_b, cos_t, sin_t, head_g, hf, *, reverse, n_batch, seq,
                     n_heads, dqk, dv):
    r = p1.shape[0]
    L = MLSTM_CHUNK
    n_lat = seq // L
    lat_blocks = n_batch * n_lat
    qkw, mw = n_heads * dqk, n_heads * dv
    gate_tile = p2.shape[1] // LANE - 1

    def blk(b, s):
        lat = b * n_lat + ((n_lat - s) if reverse else (s - 1))
        return jnp.where(s == 0, lat_blocks + b, lat)

    once = pl.Buffered(1)
    in_specs = [
        pl.BlockSpec(memory_space=pltpu.SMEM),
        pl.BlockSpec((L, qkw), lambda b, s: (blk(b, s), 0)),
        pl.BlockSpec((L, qkw), lambda b, s: (blk(b, s), 1)),
        pl.BlockSpec((L, mw), lambda b, s: (blk(b, s), (2 * qkw) // mw)),
        pl.BlockSpec(cos_t.shape, lambda b, s: (0, 0), pipeline_mode=once),
        pl.BlockSpec(sin_t.shape, lambda b, s: (0, 0), pipeline_mode=once),
        pl.BlockSpec((L, LANE), lambda b, s: (blk(b, s), gate_tile)),
        pl.BlockSpec((grow.shape[0], L), lambda b, s: (0, blk(b, s))),
    ]
    args = [gate_b.reshape(-1), p1, p1, p1, cos_t, sin_t, p2, grow]
    if reverse:
        in_specs += [
            pl.BlockSpec((L, mw), lambda b, s: (blk(b, s), (2 * qkw) // mw + 1)),
            pl.BlockSpec((L, mw), lambda b, s: (blk(b, s), 0)),
            pl.BlockSpec((1, mw), lambda b, s: (0, 0)),
        ]
        args += [p1, hf, head_g.reshape(1, mw)]
    return pl.pallas_call(
        functools.partial(_mlstm_kernel, reverse=reverse, n_heads=n_heads, scale=dqk ** -0.5),
        out_shape=jax.ShapeDtypeStruct((r, mw), BF16 if reverse else F32),
        grid=(n_batch, n_lat + 1),
        in_specs=in_specs,
        out_specs=pl.BlockSpec((L, mw), lambda b, s: (blk(b, s), 0)),
        scratch_shapes=[pltpu.VMEM((n_heads, dqk, dv + LANE), F32), pltpu.VMEM((n_heads, 8, LANE), F32)],
        compiler_params=_cparams("parallel", "arbitrary"),
        name="mlstm_bwd" if reverse else "mlstm_fwd",
    )(*args)


def _rope_tables(seq, ctx_len, dqk):
    half = dqk // 2
    inv = ROPE_BASE ** (-jnp.arange(0, half, 2, dtype=F32) / half)
    t = jnp.arange(seq)

    def one(pos):
        ang = pos.astype(F32)[:, None] * inv
        c, s = jnp.cos(ang), jnp.sin(ang)
        return jnp.concatenate([c, c], axis=-1), jnp.concatenate([-s, s], axis=-1)

    c_r, s_r = one(t // GRID_W)
    c_c, s_c = one(t % GRID_W)
    cos_t = jnp.concatenate([c_r, c_c], axis=-1)
    sin_t = jnp.concatenate([s_r, s_c], axis=-1)
    cos_t = jnp.concatenate([cos_t, jnp.ones((ctx_len, dqk), F32)], axis=0)
    sin_t = jnp.concatenate([sin_t, jnp.zeros((ctx_len, dqk), F32)], axis=0)
    return cos_t, sin_t


def _s5_tables(a_re, a_im, log_dt, b_re, b_im, c_re, c_im):
    g_, p_, hsz = b_re.shape
    lc = S5_CHUNK
    bmat = lax.complex(b_re.astype(F32), b_im.astype(F32))
    cmat = lax.complex(c_re.astype(F32), c_im.astype(F32))
    tau = jnp.arange(lc + 1, dtype=F32)
    s_idx = jnp.arange(lc)
    m_tot = 0.0
    w_parts, v_parts, a_parts = [], [], []
    for d in range(2):
        lam = lax.complex(a_re[d].astype(F32), a_im[d].astype(F32))
        dt = jnp.exp(log_dt[d].astype(F32))[:, None]
        fac = (jnp.exp(lam * dt) - 1) / lam
        apow = jnp.exp(tau[:, None, None] * (lam * dt)[None])
        bf = bmat * fac[:, :, None]
        cb = cmat.transpose(0, 2, 1)[:, :, :, None] * bf[:, :, None, :]
        kt = jnp.real(jnp.einsum('tgp,gpij->tgij', apow[:lc], cb,
                                 precision=lax.Precision.HIGHEST))
        lag = (s_idx[None, :] - s_idx[:, None]) if d == 0 else (s_idx[:, None] - s_idx[None, :])
        ok = lag >= 0
        kk = kt[jnp.clip(lag, 0, lc - 1)]
        kk = jnp.where(ok[:, :, None, None, None], kk, 0.0)
        m_tot = m_tot + kk.transpose(2, 0, 4, 1, 3).reshape(g_, lc * hsz, lc * hsz)
        pw = apow[(lc - 1 - s_idx) if d == 0 else s_idx]
        wc = pw[:, :, :, None] * bf[None]
        wc = wc.transpose(1, 0, 3, 2).reshape(g_, lc * hsz, p_)
        pv = apow[(s_idx + 1) if d == 0 else (lc - s_idx)]
        vc = cmat[None] * pv[:, :, None, :]
        vc = vc.transpose(1, 3, 0, 2).reshape(g_, p_, lc * hsz)
        pad_w = jnp.zeros((g_, lc * hsz, LANE - p_), F32)
        pad_v = jnp.zeros((g_, LANE - p_, lc * hsz), F32)
        w_parts += [jnp.real(wc), pad_w, jnp.imag(wc), pad_w]
        v_parts += [jnp.real(vc), pad_v, -jnp.imag(vc), pad_v]
        a_chunk = apow[lc]
        pad_a = jnp.zeros((g_, LANE - p_), F32)
        a_parts += [jnp.concatenate([jnp.real(a_chunk), pad_a], axis=1),
                    jnp.concatenate([jnp.imag(a_chunk), pad_a], axis=1)]
    w_all = jnp.concatenate(w_parts, axis=2)
    v_all = jnp.concatenate(v_parts, axis=1)
    a_all = jnp.stack(a_parts + [jnp.zeros_like(a_parts[0])] * 4, axis=1)
    return m_tot.astype(BF16), w_all.astype(BF16), v_all.astype(BF16), a_all


def _s5_kernel(*refs, nb, n_lat, n_ctx, hsz):
    lc = S5_CHUNK
    x_refs = refs[:lc]
    m_ref, w_ref, v_ref, a_ref, o_ref = refs[lc:lc + 5]
    z_refs = refs[lc + 5:lc + 9]
    sp_refs = refs[lc + 9:lc + 13]
    k = pl.program_id(0) % (LANE // hsz)
    cw = lc * hsz
    ri = lax.broadcasted_iota(jnp.int32, (lc * LANE, cw), 0)
    ci = lax.broadcasted_iota(jnp.int32, (lc * LANE, cw), 1)
    sel = jnp.where(ri == (ci // hsz) * LANE + k * hsz + ci % hsz, 1.0, 0.0).astype(BF16)
    ri_t = lax.broadcasted_iota(jnp.int32, (cw, lc * LANE), 0)
    ci_t = lax.broadcasted_iota(jnp.int32, (cw, lc * LANE), 1)
    sel_t = jnp.where(ci_t == (ri_t // hsz) * LANE + k * hsz + ri_t % hsz, 1.0, 0.0).astype(BF16)

    x = jnp.concatenate([r[...] for r in x_refs], axis=1).astype(BF16)
    ub = jnp.dot(x, sel, preferred_element_type=F32).astype(BF16)
    z = jnp.dot(ub, w_ref[...], preferred_element_type=F32)
    for q in range(4):
        z_refs[q][...] = z[:, q * LANE:(q + 1) * LANE]
    n_chunks = n_lat + n_ctx

    def rows_of(chunk):
        if chunk < n_lat:
            return pl.ds(chunk, nb, stride=n_lat)
        return pl.ds(nb * n_lat + chunk - n_lat, nb, stride=n_ctx)

    a = a_ref[...]
    coef = [(a[2 * d:2 * d + 1, :], a[2 * d + 1:2 * d + 2, :]) for d in range(2)]
    state = [(jnp.zeros((nb, LANE), F32), jnp.zeros((nb, LANE), F32)) for _ in range(2)]
    for t in range(n_chunks):
        chunk = ((t + n_lat) % n_chunks, n_chunks - 1 - t)
        for d in range(2):
            rs = rows_of(chunk[d])
            a_r, a_i = coef[d]
            s_re, s_im = state[d]
            sp_refs[2 * d][rs, :] = s_re
            sp_refs[2 * d + 1][rs, :] = s_im
            z_re = z_refs[2 * d][rs, :]
            z_im = z_refs[2 * d + 1][rs, :]
            state[d] = (a_r * s_re - a_i * s_im + z_re, a_r * s_im + a_i * s_re + z_im)
    sp = jnp.concatenate([r[...] for r in sp_refs], axis=1)
    y = (jnp.dot(ub, m_ref[...], preferred_element_type=F32)
         + jnp.dot(sp.astype(BF16), v_ref[...], preferred_element_type=F32))
    y_hi = y.astype(BF16)
    y_lo = (y - y_hi.astype(F32)).astype(BF16)
    back = (jnp.dot(y_hi, sel_t, preferred_element_type=F32)
            + jnp.dot(y_lo, sel_t, preferred_element_type=F32))

    @pl.when(k == 0)
    def _():
        o_ref[...] = jnp.zeros_like(o_ref)

    for s in range(lc):
        o_ref[s] += back[:, s * LANE:(s + 1) * LANE]


def _s5_scan(p2v, tables, *, nb, n_lat, n_ctx, n_groups, hsz, row_w):
    m_tot, w_all, v_all, a_all = tables
    rows = p2v.shape[0]
    lc = S5_CHUNK
    cw = lc * hsz
    gpt = LANE // hsz
    tiles_per_tok = row_w // LANE

    def x_spec(s):
        return pl.BlockSpec((rows, LANE), lambda g: (0, s * tiles_per_tok + g // gpt))

    return pl.pallas_call(
        functools.partial(_s5_kernel, nb=nb, n_lat=n_lat, n_ctx=n_ctx, hsz=hsz),
        out_shape=jax.ShapeDtypeStruct((lc, rows, n_groups * hsz), F32),
        grid=(n_groups,),
        in_specs=[x_spec(s) for s in range(lc)] + [
            pl.BlockSpec((None, cw, cw), lambda g: (g, 0, 0)),
            pl.BlockSpec((None, cw, 4 * LANE), lambda g: (g, 0, 0)),
            pl.BlockSpec((None, 4 * LANE, cw), lambda g: (g, 0, 0)),
            pl.BlockSpec((None, 8, LANE), lambda g: (g, 0, 0))],
        out_specs=pl.BlockSpec((lc, rows, LANE), lambda g: (0, 0, g // gpt)),
        scratch_shapes=[pltpu.VMEM((rows, LANE), F32)] * 8,
        compiler_params=_cparams("arbitrary"),
        name="s5_scan",
    )(*([p2v] * lc), m_tot, w_all, v_all, a_all)


def _glu_kernel(y_ref, p_ref, d_ref, w_ref, b_ref, o_ref):
    width = y_ref.shape[1]
    g = jax.nn.gelu(y_ref[...] + d_ref[...] * p_ref[:, :width])
    z = jnp.dot(g.astype(BF16), w_ref[...], preferred_element_type=F32) + b_ref[...]
    o_ref[...] = (g * jax.nn.sigmoid(z)).astype(o_ref.dtype)


def _glu(y, p2, d_skip, glu_w, glu_b, tm):
    r, w = y.shape
    return pl.pallas_call(
        _glu_kernel,
        out_shape=jax.ShapeDtypeStruct((r, w), BF16),
        grid=(r // tm,),
        in_specs=[pl.BlockSpec((tm, w), lambda i: (i, 0)),
                  pl.BlockSpec((tm, p2.shape[1]), lambda i: (i, 0)),
                  pl.BlockSpec((1, w), lambda i: (0, 0)),
                  pl.BlockSpec((w, w), lambda i: (0, 0)),
                  pl.BlockSpec((1, w), lambda i: (0, 0))],
        out_specs=pl.BlockSpec((tm, w), lambda i: (i, 0)),
        compiler_params=_cparams("parallel"),
        name="s5_glu",
    )(y, p2, d_skip.astype(F32).reshape(1, w), glu_w.astype(BF16), glu_b.astype(F32).reshape(1, w))


def _na_variants(rows, kh):
    n_blk = rows // NA_QROWS
    out = []
    for i in (0, min(1, n_blk - 1), n_blk - 1):
        out.append((i, min(max(i * NA_QROWS - kh // 2, 0), rows - NA_KROWS)))
    return out


def _na_tables(rpb, rows):
    n_heads, nri, nci = rpb.shape
    win_h, win_w = (nri + 1) // 2, (nci + 1) // 2
    kh = min(win_h, rows)
    col = np.arange(GRID_W)
    col_start = np.clip(col - win_w // 2, 0, GRID_W - win_w)
    col_ok = (col[None, :] >= col_start[:, None]) & (col[None, :] < col_start[:, None] + win_w)
    ci = np.clip(col[None, :] - col[:, None], -(win_w - 1), win_w - 1) + (win_w - 1)
    onehot = (ci[None] == np.arange(nci)[:, None, None]).astype(np.float32)
    t1 = jnp.einsum('hrc,cqk->hrqk', rpb.astype(F32), onehot, precision=lax.Precision.HIGHEST)
    t1 = jnp.where(col_ok[None, None], t1, -jnp.inf)
    dummy = jnp.full((n_heads, 1, GRID_W, GRID_W), -jnp.inf, F32)
    t1 = jnp.concatenate([dummy, t1, dummy], axis=1)
    t2 = jnp.concatenate([t1[:, :nri + 1], t1[:, 1:nri + 2]], axis=-1)
    rmask = np.zeros((3, NA_QROWS, NA_KROWS // 2, 1, 2 * GRID_W), np.float32)
    for v, (i, ks) in enumerate(_na_variants(rows, kh)):
        for a in range(NA_QROWS):
            r_start = min(max(i * NA_QROWS + a - kh // 2, 0), rows - kh)
            for b in range(NA_KROWS):
                if not (r_start <= ks + b < r_start + kh):
                    rmask[v, a, b // 2, 0, (b % 2) * GRID_W:(b % 2 + 1) * GRID_W] = -np.inf
    return t2, jnp.asarray(rmask)


def _na_kernel(q_ref, k_ref, v_ref, kc_ref, vc_ref, t2_ref, rm_ref, o_ref, bias_ref, *, rows, kh, win_h,
               scale):
    n_blk = rows // NA_QROWS
    tq = NA_QROWS * GRID_W
    tk = NA_KROWS * GRID_W
    n_r = t2_ref.shape[0]

    @pl.when(pl.program_id(1) == 0)
    def _():
        for v, (i, ks) in enumerate(_na_variants(rows, kh)):
            for a in range(NA_QROWS):
                for pair in range(NA_KROWS // 2):
                    r = (ks + 2 * pair) - (i * NA_QROWS + a) + (win_h - 1)
                    tile = t2_ref[min(max(r + 1, 0), n_r - 1)] + rm_ref[v, a, pair]
                    bias_ref[v, a * GRID_W:(a + 1) * GRID_W,
                             pair * 2 * GRID_W:(pair + 1) * 2 * GRID_W] = tile * LOG2E

    kc = kc_ref[...]
    vc = vc_ref[...]
    nt = (((1,), (1,)), ((), ()))

    def body(i, carry):
        q0 = pl.multiple_of(i * tq, tq)
        ks = jnp.clip(i * NA_QROWS - kh // 2, 0, rows - NA_KROWS)
        k0 = pl.multiple_of(ks * GRID_W, 4 * GRID_W)
        variant = jnp.where(i == 0, 0, jnp.where(i == n_blk - 1, 2, 1))
        qb = q_ref[pl.ds(q0, tq), :]
        kb = k_ref[pl.ds(k0, tk), :]
        vb = v_ref[pl.ds(k0, tk), :]
        s_loc = lax.dot_general(qb, kb, nt, preferred_element_type=F32) * (scale * LOG2E) + bias_ref[variant]
        s_ctx = lax.dot_general(qb, kc, nt, preferred_element_type=F32) * (scale * LOG2E)
        m = jnp.maximum(jnp.max(s_loc, axis=1, keepdims=True), jnp.max(s_ctx, axis=1, keepdims=True))
        p_loc = jnp.exp2(s_loc - m)
        p_ctx = jnp.exp2(s_ctx - m)
        denom = jnp.sum(p_loc, axis=1, keepdims=True) + jnp.sum(p_ctx, axis=1, keepdims=True)
        acc = (jnp.dot(p_loc.astype(BF16), vb, preferred_element_type=F32)
               + jnp.dot(p_ctx.astype(BF16), vc, preferred_element_type=F32))
        o_ref[pl.ds(q0, tq), :] = (acc / denom).astype(o_ref.dtype)
        return carry

    lax.fori_loop(0, n_blk, body, 0, unroll=NA_UNROLL)


def _na_attention(qkv, t2, rmask, *, n_batch, seq, ctx_len, n_heads, hd, kh, win_h):
    rows = seq // GRID_W
    lat_ctx_blk = (n_batch * seq) // ctx_len
    tq, tk = NA_QROWS * GRID_W, NA_KROWS * GRID_W
    return pl.pallas_call(
        functools.partial(_na_kernel, rows=rows, kh=kh, win_h=win_h, scale=hd ** -0.5),
        out_shape=jax.ShapeDtypeStruct((n_heads, n_batch * seq, hd), BF16),
        grid=(n_heads, n_batch),
        in_specs=[pl.BlockSpec((None, seq, hd), lambda h, b: (h, b, 0)),
                  pl.BlockSpec((None, seq, hd), lambda h, b: (n_heads + h, b, 0)),
                  pl.BlockSpec((None, seq, hd), lambda h, b: (2 * n_heads + h, b, 0)),
                  pl.BlockSpec((None, ctx_len, hd), lambda h, b: (n_heads + h, lat_ctx_blk + b, 0)),
                  pl.BlockSpec((None, ctx_len, hd), lambda h, b: (2 * n_heads + h, lat_ctx_blk + b, 0)),
                  pl.BlockSpec((None,) + t2.shape[1:], lambda h, b: (h, 0, 0, 0)),
                  pl.BlockSpec(rmask.shape, lambda h, b: (0, 0, 0, 0, 0))],
        out_specs=pl.BlockSpec((None, seq, hd), lambda h, b: (h, b, 0)),
        scratch_shapes=[pltpu.VMEM((3, tq, tk), F32)],
        compiler_params=_cparams("arbitrary", "arbitrary"),
        name="na_attention",
    )(qkv, qkv, qkv, qkv, qkv, t2, rmask)


def _ffn_kernel(hx_ref, wg_ref, wa_ref, cw_ref, cb_ref, wd_ref, x_ref, gate_ref, ng_ref, *rest,
                tm, n_lat_rows, seq, ctx_len, post):
    if post == "modulate":
        nsh_ref, nsc_ref, o_ref, h_ref = rest
    else:
        (o_ref,) = rest
    i = pl.program_id(0)
    j = pl.program_id(1)

    @pl.when(j == 0)
    def _():
        o_ref[...] = jnp.zeros_like(o_ref)

    g_all = jnp.dot(hx_ref[...], wg_ref[...], preferred_element_type=F32)
    a = jnp.dot(hx_ref[:tm, :], wa_ref[...], preferred_element_type=F32)
    g = g_all[:tm]
    local = lax.broadcasted_iota(jnp.int32, (tm, 1), 0)
    grow = i * tm + local
    in_lat = grow < n_lat_rows
    pos = jnp.where(in_lat, grow % seq, (grow - n_lat_rows) % ctx_len)
    last = jnp.where(in_lat, seq - 1, ctx_len - 1)
    g_prev = jnp.where(local == 0, g_all[tm + 7:tm + 8, :], pltpu.roll(g, 1, axis=0))
    g_prev = jnp.where(pos == 0, 0.0, g_prev)
    g_next = jnp.where(local == tm - 1, g_all[tm + 8:tm + 9, :], pltpu.roll(g, tm - 1, axis=0))
    g_next = jnp.where(pos == last, 0.0, g_next)
    cw = cw_ref[...]
    gl = jax.nn.gelu(cw[0:1, :] * g_prev + cw[1:2, :] * g + cw[2:3, :] * g_next + cb_ref[...])
    o_ref[...] += jnp.dot((gl * a).astype(BF16), wd_ref[...], preferred_element_type=F32)

    @pl.when(j == pl.num_programs(1) - 1)
    def _():
        rows_per = 64

        def slab(c, carry):
            rs = pl.ds(pl.multiple_of(c * rows_per, rows_per), rows_per)
            xn = x_ref[rs, :] + gate_ref[...] * o_ref[rs, :]
            y = xn * lax.rsqrt(jnp.mean(xn * xn, axis=-1, keepdims=True) + EPS) * ng_ref[...]
            if post == "modulate":
                o_ref[rs, :] = xn
                h_ref[rs, :] = (y * (1.0 + nsc_ref[...]) + nsh_ref[...]).astype(h_ref.dtype)
            else:
                o_ref[rs, :] = y
            return carry

        lax.fori_loop(0, tm // rows_per, slab, 0)


def _conv_ffn(hx, xs, weights, mods, gate_idx, seg, next_gain, next_mods=None, *, tm, tf, n_lat_rows,
              seq, ctx_len):
    w_g, w_a, cw8, cb, w_down = weights
    n_blk, _, d = hx.shape
    r = n_blk * tm
    dffp = w_down.shape[0]
    nj = dffp // tf
    once = pl.Buffered(1)
    post = "norm" if next_mods is None else "modulate"
    in_specs = [pl.BlockSpec((None, tm + 16, d), lambda i, j: (i, 0, 0), pipeline_mode=once),
                pl.BlockSpec((d, tf), lambda i, j: (0, j)),
                pl.BlockSpec((d, tf), lambda i, j: (0, j)),
                pl.BlockSpec((8, tf), lambda i, j: (0, j)),
                pl.BlockSpec((1, tf), lambda i, j: (0, j)),
                pl.BlockSpec((tf, d), lambda i, j: (j, 0)),
                pl.BlockSpec((tm, d), lambda i, j: (i, 0), pipeline_mode=once),
                pl.BlockSpec((None, None, 1, d), lambda i, j: (seg(i), gate_idx, 0, 0)),
                pl.BlockSpec((1, d), lambda i, j: (0, 0))]
    args = [hx, w_g, w_a, cw8, cb, w_down, xs, mods, next_gain.reshape(1, d)]
    out_spec = pl.BlockSpec((tm, d), lambda i, j: (i, 0), pipeline_mode=once)
    out_shape = jax.ShapeDtypeStruct((r, d), F32)
    if post == "modulate":
        in_specs += [pl.BlockSpec((None, None, 1, d), lambda i, j: (seg(i), 0, 0, 0)),
                     pl.BlockSpec((None, None, 1, d), lambda i, j: (seg(i), 1, 0, 0))]
        args += [next_mods, next_mods]
        out_spec = [out_spec, pl.BlockSpec((tm, d), lambda i, j: (i, 0), pipeline_mode=once)]
        out_shape = [out_shape, jax.ShapeDtypeStruct((r, d), BF16)]
    return pl.pallas_call(
        functools.partial(_ffn_kernel, tm=tm, n_lat_rows=n_lat_rows, seq=seq, ctx_len=ctx_len, post=post),
        out_shape=out_shape,
        grid=(n_blk, nj),
        in_specs=in_specs,
        out_specs=out_spec,
        compiler_params=_cparams("parallel", "arbitrary"),
        name="conv_ffn",
    )(*args)


def _pad_ffn_weights(w_up, conv_w, conv_b, w_down, tf):
    d = w_up.shape[0]
    dff = w_down.shape[0]
    pad = -dff % tf
    zc = jnp.zeros((d, pad), BF16)
    w_a = jnp.concatenate([w_up[:, :dff].astype(BF16), zc], axis=1)
    w_g = jnp.concatenate([w_up[:, dff:].astype(BF16), zc], axis=1)
    cw8 = jnp.pad(conv_w.astype(F32), ((0, 8 - conv_w.shape[0]), (0, pad)))
    cb = jnp.pad(conv_b.astype(F32), (0, pad)).reshape(1, dff + pad)
    w_down_p = jnp.concatenate([w_down.astype(BF16), jnp.zeros((pad, d), BF16)], axis=0)
    return w_g, w_a, cw8, cb, w_down_p


def kernel(x, c, ctx, c_ctx, mod_w, mod_b, norm_mix_g, norm_ffn_g, ab_w_in, mlstm_gate_b, mlstm_head_g, s5_a_re, s5_a_im, s5_log_dt, s5_b_re, s5_b_im, s5_c_re, s5_c_im, s5_d, s5_glu_w, s5_glu_b, ab_w_out, na_w_qkv, na_rpb, na_w_out, ffn_w_up, ffn_conv_w, ffn_conv_b, ffn_w_down, final_norm_g):
    n_batch, seq, d = x.shape
    ctx_len = ctx.shape[1]
    depth = mod_w.shape[0]
    assert depth == 2 and mod_w.shape[2] == 6 * d
    n_lat_rows = n_batch * seq
    n_ctx_rows = n_batch * ctx_len
    assert seq % MLSTM_CHUNK == 0 and ctx_len == MLSTM_CHUNK and seq % GRID_W == 0

    n_heads = mlstm_gate_b.shape[-1]
    s5w = s5_d.shape[-1]
    mw = d - s5w
    dv = mw // n_heads
    dqk = dv // 2
    qkw = n_heads * dqk
    n_groups, p_state, s5_group = s5_b_re.shape[1:]
    assert S5_CHUNK * s5_group == 2 * LANE and p_state <= LANE and dqk % (2 * LANE) == 0
    na_heads = na_rpb.shape[1]
    hd = d // na_heads
    rows = seq // GRID_W
    kh = min((na_rpb.shape[2] + 1) // 2, rows)
    assert rows % NA_QROWS == 0 and rows >= NA_KROWS and hd == LANE
    dff = ffn_w_down.shape[1]

    tm_big = _pick(math.gcd(n_lat_rows, n_ctx_rows), (1024, 512, 256))
    tm_ffn = _pick(math.gcd(n_lat_rows, n_ctx_rows), (512, 256))
    tf = 512
    seg_big = _seg_fn(tm_big, n_lat_rows, seq, n_batch)
    seg_ffn = _seg_fn(tm_ffn, n_lat_rows, seq, n_batch)
    seg_256 = _seg_fn(256, n_lat_rows, seq, n_batch)

    cond8 = jnp.concatenate([c, c_ctx[None], jnp.zeros((8 - n_batch - 1, d), F32)], axis=0)
    mods = _ada(cond8, mod_w, mod_b).reshape(depth, 8, 6, 1, d)

    x2d, c2d = x.reshape(n_lat_rows, d), ctx.reshape(n_ctx_rows, d)

    h = _modulate2(x2d, c2d, norm_mix_g[0], mods[0], 0, 1, seg_256, 256)
    w_in = ab_w_in[0]
    n_gate = 4 * n_heads
    w1 = w_in[:, :2 * qkw + 2 * mw].astype(BF16)
    w2 = jnp.concatenate([w_in[:, 2 * qkw + 2 * mw + n_gate:], w_in[:, 2 * qkw + 2 * mw:2 * qkw + 2 * mw + n_gate],
                          jnp.zeros((d, LANE - n_gate), F32)], axis=1).astype(BF16)
    p1 = _matmul(h, w1, BF16, tm_big, _pick(w1.shape[1], (512, 256, 128)), "ab_in_proj")
    p2 = _matmul(h, w2, F32, tm_ffn, w2.shape[1], "ab_in_proj_s5")
    grow = jnp.pad(p2[:, s5w:s5w + n_gate].T, ((0, -n_gate % 8), (0, 0)))

    cos_t, sin_t = _rope_tables(seq, ctx_len, dqk)
    common = dict(n_batch=n_batch, seq=seq, n_heads=n_heads, dqk=dqk, dv=dv)
    hf = _mlstm_direction(p1, p2, grow, mlstm_gate_b[0], cos_t, sin_t, None, None, reverse=False, **common)
    mix_a = _mlstm_direction(p1, p2, grow, mlstm_gate_b[0], cos_t, sin_t, mlstm_head_g[0], hf,
                             reverse=True, **common)

    n_lat_ch, n_ctx_ch = seq // S5_CHUNK, ctx_len // S5_CHUNK
    row_w = p2.shape[1]
    tables = _s5_tables(s5_a_re[0], s5_a_im[0], s5_log_dt[0], s5_b_re[0], s5_b_im[0], s5_c_re[0], s5_c_im[0])
    y16 = _s5_scan(p2.reshape(-1, S5_CHUNK * row_w), tables, nb=n_batch, n_lat=n_lat_ch, n_ctx=n_ctx_ch,
                   n_groups=n_groups, hsz=s5_group, row_w=row_w)
    y_rows = y16.transpose(1, 0, 2).reshape(-1, s5w)
    mix_b = _glu(y_rows, p2, s5_d[0], s5_glu_w[0], s5_glu_b[0], tm_ffn)

    w_out = ab_w_out[0].astype(BF16)
    xs = _matmul_residual([mix_a, mix_b], [w_out[:mw], w_out[mw:]], (x2d, c2d), mods[0], 2, seg_big,
                          tm_big, 512, "ab_out_proj")
    h = _modulate_halo(xs, norm_ffn_g[0], mods[0], 3, 4, seg_ffn, tm_ffn)
    ffn_kw = dict(tm=tm_ffn, tf=tf, n_lat_rows=n_lat_rows, seq=seq, ctx_len=ctx_len)
    xs, h = _conv_ffn(h, xs, _pad_ffn_weights(ffn_w_up[0], ffn_conv_w[0], ffn_conv_b[0], ffn_w_down[0], tf),
                      mods[0], 5, seg_ffn, norm_mix_g[1], mods[1], **ffn_kw)

    qkv = _matmul(h, na_w_qkv[0].astype(BF16), BF16, tm_big, 512, "na_qkv_proj", unit=hd)
    t2, rmask = _na_tables(na_rpb[0], rows)
    o = _na_attention(qkv, t2, rmask, n_batch=n_batch, seq=seq, ctx_len=ctx_len, n_heads=na_heads, hd=hd,
                      kh=kh, win_h=(na_rpb.shape[2] + 1) // 2)
    xl = _matmul_residual([o], [na_w_out[0].astype(BF16)], xs, mods[1], 2, seg_big, tm_big, 512,
                          "na_out_proj")
    h = _modulate_halo(xl, norm_ffn_g[1], mods[1], 3, 4, seg_ffn, tm_ffn)
    out = _conv_ffn(h, xl, _pad_ffn_weights(ffn_w_up[1], ffn_conv_w[1], ffn_conv_b[1], ffn_w_down[1], tf),
                    mods[1], 5, seg_ffn, final_norm_g, **ffn_kw)
    return out.reshape(n_batch, seq, d)
```

```python
import functools
import math

import jax
import jax.numpy as jnp
import numpy as np
from jax import lax
from jax.experimental import pallas as pl
from jax.experimental.pallas import tpu as pltpu

F32 = jnp.float32
BF16 = jnp.bfloat16

EPS = 1e-6
ROPE_BASE = 10000.0
GRID_W = 64
MLSTM_CHUNK = 256
S5_CHUNK = 16
NA_QROWS = 4
NA_KROWS = 12
NA_UNROLL = 4
LOG2E = 1.4426950408889634
LANE = 128
VMEM_LIMIT_BYTES = 58 * 1024 * 1024


def _cparams(*sem):
    return pltpu.CompilerParams(dimension_semantics=sem, vmem_limit_bytes=VMEM_LIMIT_BYTES)


def _pick(n, candidates):
    for c in candidates:
        if n % c == 0:
            return c
    raise ValueError(f"no tile in {candidates} divides {n}")


def _ada_kernel(c_ref, w_ref, b_ref, o_ref):
    s = jax.nn.silu(c_ref[...]).astype(BF16)
    o_ref[...] = jnp.dot(s, w_ref[...].astype(BF16), preferred_element_type=F32) + b_ref[...]


def _ada(cond8, mod_w, mod_b):
    depth, d, n6 = mod_w.shape
    tn = _pick(n6, (512, 256, 128))
    return pl.pallas_call(
        _ada_kernel,
        out_shape=jax.ShapeDtypeStruct((depth, 8, n6), F32),
        grid=(depth, n6 // tn),
        in_specs=[pl.BlockSpec((8, d), lambda l, j: (0, 0)),
                  pl.BlockSpec((None, d, tn), lambda l, j: (l, 0, j)),
                  pl.BlockSpec((None, 1, tn), lambda l, j: (l, 0, j))],
        out_specs=pl.BlockSpec((None, 8, tn), lambda l, j: (l, 0, j)),
        compiler_params=_cparams("parallel", "parallel"),
        name="ada_mod",
    )(cond8, mod_w, mod_b.reshape(depth, 1, n6))


def _seg_fn(tm, n_lat_rows, seq, n_batch):
    n_lat_blk = n_lat_rows // tm

    def seg(i):
        return jnp.where(i < n_lat_blk, (i * tm) // seq, n_batch)
    return seg


def _two_source_specs(block, n_first_blk, col=None):
    if col is None:
        return (pl.BlockSpec(block, lambda i: (jnp.minimum(i, n_first_blk - 1), 0)),
                pl.BlockSpec(block, lambda i: (jnp.maximum(i - n_first_blk, 0), 0)))
    return (pl.BlockSpec(block, lambda i, j: (jnp.minimum(i, n_first_blk - 1), jnp.where(i < n_first_blk, j, 0))),
            pl.BlockSpec(block, lambda i, j: (jnp.maximum(i - n_first_blk, 0), jnp.where(i < n_first_blk, 0, j))))


def _modulate2_kernel(x_ref, c_ref, g_ref, sh_ref, sc_ref, o_ref, *, n_first_blk):
    x = jnp.where(pl.program_id(0) < n_first_blk, x_ref[...], c_ref[...])
    y = x * lax.rsqrt(jnp.mean(x * x, axis=-1, keepdims=True) + EPS) * g_ref[...]
    o_ref[...] = (y * (1.0 + sc_ref[...]) + sh_ref[...]).astype(o_ref.dtype)


def _modulate2(x2d, c2d, gain, mods, shift_idx, scale_idx, seg, tm):
    d = x2d.shape[1]
    r = x2d.shape[0] + c2d.shape[0]
    n_first_blk = x2d.shape[0] // tm
    return pl.pallas_call(
        functools.partial(_modulate2_kernel, n_first_blk=n_first_blk),
        out_shape=jax.ShapeDtypeStruct((r, d), BF16),
        grid=(r // tm,),
        in_specs=[*_two_source_specs((tm, d), n_first_blk),
                  pl.BlockSpec((1, d), lambda i: (0, 0)),
                  pl.BlockSpec((None, None, 1, d), lambda i: (seg(i), shift_idx, 0, 0)),
                  pl.BlockSpec((None, None, 1, d), lambda i: (seg(i), scale_idx, 0, 0))],
        out_specs=pl.BlockSpec((tm, d), lambda i: (i, 0)),
        compiler_params=_cparams("parallel"),
        name="modulate",
    )(x2d, c2d, gain.reshape(1, d), mods, mods)


def _modulate_halo_kernel(x_ref, xp_ref, xn_ref, g_ref, sh_ref, sc_ref, o_ref):
    x = jnp.concatenate([x_ref[...], xp_ref[...], xn_ref[...]], axis=0)
    y = x * lax.rsqrt(jnp.mean(x * x, axis=-1, keepdims=True) + EPS) * g_ref[...]
    o_ref[...] = (y * (1.0 + sc_ref[...]) + sh_ref[...]).astype(o_ref.dtype)


def _modulate_halo(xs, gain, mods, shift_idx, scale_idx, seg, tm):
    r, d = xs.shape
    n_blk, per8 = r // tm, tm // 8
    return pl.pallas_call(
        _modulate_halo_kernel,
        out_shape=jax.ShapeDtypeStruct((n_blk, tm + 16, d), BF16),
        grid=(n_blk,),
        in_specs=[pl.BlockSpec((tm, d), lambda i: (i, 0)),
                  pl.BlockSpec((8, d), lambda i: (jnp.maximum(i * per8 - 1, 0), 0)),
                  pl.BlockSpec((8, d), lambda i: (jnp.minimum((i + 1) * per8, r // 8 - 1), 0)),
                  pl.BlockSpec((1, d), lambda i: (0, 0)),
                  pl.BlockSpec((None, None, 1, d), lambda i: (seg(i), shift_idx, 0, 0)),
                  pl.BlockSpec((None, None, 1, d), lambda i: (seg(i), scale_idx, 0, 0))],
        out_specs=pl.BlockSpec((None, tm + 16, d), lambda i: (i, 0, 0)),
        compiler_params=_cparams("parallel"),
        name="modulate_halo",
    )(xs, xs, xs, gain.reshape(1, d), mods, mods)


def _cast_kernel(x_ref, o_ref, *, n_valid):
    keep = pl.program_id(0) < n_valid
    o_ref[...] = jnp.where(keep, x_ref[...], 0.0).astype(o_ref.dtype)


def _cast_cols(w, layer, col0, n_cols, n_cols_pad, bw):
    d = w.shape[1]
    nv, off = n_cols // bw, col0 // bw
    assert n_cols % bw == 0 and col0 % bw == 0 and n_cols_pad % bw == 0
    return pl.pallas_call(
        functools.partial(_cast_kernel, n_valid=nv),
        out_shape=jax.ShapeDtypeStruct((d, n_cols_pad), BF16),
        grid=(n_cols_pad // bw,),
        in_specs=[pl.BlockSpec((None, d, bw), lambda i: (layer, 0, off + jnp.minimum(i, nv - 1)))],
        out_specs=pl.BlockSpec((d, bw), lambda i: (0, i)),
        compiler_params=_cparams("parallel"),
        name="cast_cols",
    )(w)


def _cast_rows(w, layer, n_rows_pad, br):
    n_rows, d = w.shape[1:]
    nv = n_rows // br
    assert n_rows % br == 0 and n_rows_pad % br == 0
    return pl.pallas_call(
        functools.partial(_cast_kernel, n_valid=nv),
        out_shape=jax.ShapeDtypeStruct((n_rows_pad, d), BF16),
        grid=(n_rows_pad // br,),
        in_specs=[pl.BlockSpec((None, br, d), lambda i: (layer, jnp.minimum(i, nv - 1), 0))],
        out_specs=pl.BlockSpec((br, d), lambda i: (i, 0)),
        compiler_params=_cparams("parallel"),
        name="cast_rows",
    )(w)


def _mm_kernel(a_ref, b_ref, o_ref):
    res = jnp.dot(a_ref[...], b_ref[...], preferred_element_type=F32).astype(o_ref.dtype)
    if len(o_ref.shape) == 3:
        w = o_ref.shape[2]
        for t in range(o_ref.shape[0]):
            o_ref[t] = res[:, t * w:(t + 1) * w]
    else:
        o_ref[...] = res


def _matmul(a, b, out_dtype, tm, tn, name, unit=None):
    m, k = a.shape
    n = b.shape[1]
    if unit is None:
        out_shape = jax.ShapeDtypeStruct((m, n), out_dtype)
        out_spec = pl.BlockSpec((tm, tn), lambda i, j: (i, j))
    else:
        out_shape = jax.ShapeDtypeStruct((n // unit, m, unit), out_dtype)
        out_spec = pl.BlockSpec((tn // unit, tm, unit), lambda i, j: (j, i, 0))
    return pl.pallas_call(
        _mm_kernel,
        out_shape=out_shape,
        grid=(m // tm, n // tn),
        in_specs=[pl.BlockSpec((tm, k), lambda i, j: (i, 0)),
                  pl.BlockSpec((k, tn), lambda i, j: (0, j))],
        out_specs=out_spec,
        compiler_params=_cparams("parallel", "parallel"),
        name=name,
    )(a, b)


def _rows_by_unit(ref):
    if len(ref.shape) == 2:
        return ref[...]
    return jnp.concatenate([ref[t] for t in range(ref.shape[0])], axis=1)


def _mm_res_kernel(*refs, n_a, n_first_blk):
    a_refs, b_refs = refs[:n_a], refs[n_a:2 * n_a]
    if n_first_blk is None:
        x_ref, gate_ref, o_ref = refs[2 * n_a:]
        x = x_ref[...]
    else:
        x_ref, c_ref, gate_ref, o_ref = refs[2 * n_a:]
        x = jnp.where(pl.program_id(0) < n_first_blk, x_ref[...], c_ref[...])
    acc = jnp.dot(_rows_by_unit(a_refs[0]), b_refs[0][...], preferred_element_type=F32)
    for a_ref, b_ref in zip(a_refs[1:], b_refs[1:]):
        acc = acc + jnp.dot(_rows_by_unit(a_ref), b_ref[...], preferred_element_type=F32)
    o_ref[...] = x + gate_ref[...] * acc


def _matmul_residual(a_list, b_list, xs, mods, gate_idx, seg, tm, tn, name):
    m = a_list[0].shape[-2]
    n = b_list[0].shape[1]

    def a_spec(a):
        if a.ndim == 2:
            return pl.BlockSpec((tm, a.shape[1]), lambda i, j: (i, 0))
        return pl.BlockSpec((a.shape[0], tm, a.shape[2]), lambda i, j: (0, i, 0))

    if isinstance(xs, tuple):
        n_first_blk = xs[0].shape[0] // tm
        x_specs = list(_two_source_specs((tm, tn), n_first_blk, col=True))
        x_args = list(xs)
    else:
        n_first_blk = None
        x_specs = [pl.BlockSpec((tm, tn), lambda i, j: (i, j))]
        x_args = [xs]
    in_specs = ([a_spec(a) for a in a_list]
                + [pl.BlockSpec((b.shape[0], tn), lambda i, j: (0, j)) for b in b_list]
                + x_specs
                + [pl.BlockSpec((None, None, 1, tn), lambda i, j: (seg(i), gate_idx, 0, j))])
    return pl.pallas_call(
        functools.partial(_mm_res_kernel, n_a=len(a_list), n_first_blk=n_first_blk),
        out_shape=jax.ShapeDtypeStruct((m, n), F32),
        grid=(m // tm, n // tn),
        in_specs=in_specs,
        out_specs=pl.BlockSpec((tm, tn), lambda i, j: (i, j)),
        compiler_params=_cparams("parallel", "parallel"),
        name=name,
    )(*a_list, *b_list, *x_args, mods)


def _rope(x, cos, sin):
    parts = []
    for j in range(x.shape[1] // LANE):
        sl = slice(j * LANE, (j + 1) * LANE)
        xs = x[:, sl]
        parts.append(xs * cos[:, sl] + pltpu.roll(xs, LANE // 2, axis=1) * sin[:, sl])
    return jnp.concatenate(parts, axis=1)


def _mlstm_kernel(gb_ref, q_ref, k_ref, v_ref, cos_ref, sin_ref, gcol_ref, grow_ref, *rest,
                  reverse, n_heads, scale):
    if reverse:
        o_ref, hf_ref, hg_ref, out_ref, ct_ref, m_ref = rest
    else:
        out_ref, ct_ref, m_ref = rest
    step = pl.program_id(1)
    d = 1 if reverse else 0
    L = q_ref.shape[0]
    dqk = q_ref.shape[1] // n_heads
    dv = v_ref.shape[1] // n_heads

    @pl.when(step == 0)
    def _():
        ct_ref[...] = jnp.zeros_like(ct_ref)
        m_ref[...] = jnp.zeros_like(m_ref)

    row = lax.broadcasted_iota(jnp.int32, (L, L), 0)
    col = lax.broadcasted_iota(jnp.int32, (L, L), 1)
    incl = (col >= row) if reverse else (col <= row)
    incl_t = (row >= col) if reverse else (row <= col)
    n_lat = pl.num_programs(1) - 1
    tab = jnp.where(step == 0, n_lat, (n_lat - step) if reverse else (step - 1))
    t0 = pl.multiple_of(tab * L, L)
    cos = cos_ref[pl.ds(t0, L), :]
    sin = sin_ref[pl.ds(t0, L), :]
    ones = jnp.ones((L, LANE), v_ref.dtype)

    for hh in range(n_heads):
        gi = (d * 2 + 0) * n_heads + hh
        gf = (d * 2 + 1) * n_heads + hh
        b_i = gb_ref[gi]
        b_f = gb_ref[gf]
        li_col = gcol_ref[:, gi:gi + 1] + b_i
        lf_col = jax.nn.log_sigmoid(gcol_ref[:, gf:gf + 1] + b_f)
        li_row = grow_ref[gi:gi + 1, :] + b_i
        lf_row = jax.nn.log_sigmoid(grow_ref[gf:gf + 1, :] + b_f)
        cum_col = jnp.sum(jnp.where(incl, lf_row, 0.0), axis=1, keepdims=True)
        cum_row = jnp.sum(jnp.where(incl_t, lf_col, 0.0), axis=0, keepdims=True)
        total = jnp.sum(lf_row, axis=1, keepdims=True)
        m_prev = m_ref[hh, 0:1, 0:1]

        dmat = jnp.where(incl, cum_col - cum_row + li_row, -jnp.inf)
        carried = cum_col + m_prev
        m_loc = jnp.maximum(carried, jnp.max(dmat, axis=1, keepdims=True))
        w = jnp.exp(dmat - m_loc)
        w_state = jnp.exp(carried - m_loc)

        qs = slice(hh * dqk, (hh + 1) * dqk)
        vs = slice(hh * dv, (hh + 1) * dv)
        qb = (_rope(q_ref[:, qs].astype(F32), cos, sin) * scale).astype(BF16)
        kb = _rope(k_ref[:, qs].astype(F32), cos, sin).astype(BF16)
        s_qk = lax.dot_general(qb, kb, (((1,), (1,)), ((), ())), preferred_element_type=F32)
        sc = (s_qk * w).astype(BF16)
        vext = jnp.concatenate([v_ref[:, vs], ones], axis=1)
        ct = ct_ref[hh]
        res = (jnp.dot(sc, vext, preferred_element_type=F32)
               + w_state * jnp.dot(qb, ct.astype(BF16), preferred_element_type=F32))
        den = res[:, dv:dv + 1]
        hval = res[:, :dv] / jnp.maximum(jnp.abs(den), jnp.exp(-m_loc))

        src = total - cum_col + li_col
        m_new = jnp.maximum(total + m_prev, jnp.max(src, axis=0, keepdims=True))
        decay = jnp.exp(total + m_prev - m_new)
        w_src = jnp.exp(src - m_new)
        upd = lax.dot_general(kb, (w_src * vext.astype(F32)).astype(BF16), (((0,), (0,)), ((), ())),
                              preferred_element_type=F32)
        ct_ref[hh] = decay * ct + upd
        m_ref[hh] = jnp.broadcast_to(m_new, m_ref.shape[1:])

        if reverse:
            hm = hf_ref[:, vs] + hval
            hn = hm * lax.rsqrt(jnp.mean(hm * hm, axis=-1, keepdims=True) + EPS) * hg_ref[:, vs]
            out_ref[:, vs] = (jax.nn.sigmoid(o_ref[:, vs].astype(F32)) * hn).astype(out_ref.dtype)
        else:
            out_ref[:, vs] = hval


def _mlstm_direction(p1, p2, grow, gate_b, cos_t, sin_t, head_g, hf, *, reverse, n_batch, seq,
                     n_heads, dqk, dv):
    r = p1.shape[0]
    L = MLSTM_CHUNK
    n_lat = seq // L
    lat_blocks = n_batch * n_lat
    qkw, mw = n_heads * dqk, n_heads * dv
    gate_tile = p2.shape[1] // LANE - 1

    def blk(b, s):
        lat = b * n_lat + ((n_lat - s) if reverse else (s - 1))
        return jnp.where(s == 0, lat_blocks + b, lat)

    once = pl.Buffered(1)
    in_specs = [
        pl.BlockSpec(memory_space=pltpu.SMEM),
        pl.BlockSpec((L, qkw), lambda b, s: (blk(b, s), 0)),
        pl.BlockSpec((L, qkw), lambda b, s: (blk(b, s), 1)),
        pl.BlockSpec((L, mw), lambda b, s: (blk(b, s), (2 * qkw) // mw)),
        pl.BlockSpec(cos_t.shape, lambda b, s: (0, 0), pipeline_mode=once),
        pl.BlockSpec(sin_t.shape, lambda b, s: (0, 0), pipeline_mode=once),
        pl.BlockSpec((L, LANE), lambda b, s: (blk(b, s), gate_tile)),
        pl.BlockSpec((grow.shape[0], L), lambda b, s: (0, blk(b, s))),
    ]
    args = [gate_b.reshape(-1), p1, p1, p1, cos_t, sin_t, p2, grow]
    if reverse:
        in_specs += [
            pl.BlockSpec((L, mw), lambda b, s: (blk(b, s), (2 * qkw) // mw + 1)),
            pl.BlockSpec((L, mw), lambda b, s: (blk(b, s), 0)),
            pl.BlockSpec((1, mw), lambda b, s: (0, 0)),
        ]
        args += [p1, hf, head_g.reshape(1, mw)]
    return pl.pallas_call(
        functools.partial(_mlstm_kernel, reverse=reverse, n_heads=n_heads, scale=dqk ** -0.5),
        out_shape=jax.ShapeDtypeStruct((r, mw), BF16 if reverse else F32),
        grid=(n_batch, n_lat + 1),
        in_specs=in_specs,
        out_specs=pl.BlockSpec((L, mw), lambda b, s: (blk(b, s), 0)),
        scratch_shapes=[pltpu.VMEM((n_heads, dqk, dv + LANE), F32), pltpu.VMEM((n_heads, 8, LANE), F32)],
        compiler_params=_cparams("parallel", "arbitrary"),
        name="mlstm_bwd" if reverse else "mlstm_fwd",
    )(*args)


def _rope_tables(seq, ctx_len, dqk):
    half = dqk // 2
    inv = ROPE_BASE ** (-jnp.arange(0, half, 2, dtype=F32) / half)
    t = jnp.arange(seq)

    def one(pos):
        ang = pos.astype(F32)[:, None] * inv
        c, s = jnp.cos(ang), jnp.sin(ang)
        return jnp.concatenate([c, c], axis=-1), jnp.concatenate([-s, s], axis=-1)

    c_r, s_r = one(t // GRID_W)
    c_c, s_c = one(t % GRID_W)
    cos_t = jnp.concatenate([c_r, c_c], axis=-1)
    sin_t = jnp.concatenate([s_r, s_c], axis=-1)
    cos_t = jnp.concatenate([cos_t, jnp.ones((ctx_len, dqk), F32)], axis=0)
    sin_t = jnp.concatenate([sin_t, jnp.zeros((ctx_len, dqk), F32)], axis=0)
    return cos_t, sin_t


def _s5_tables(a_re, a_im, log_dt, b_re, b_im, c_re, c_im):
    g_, p_, hsz = b_re.shape
    lc = S5_CHUNK
    bmat = lax.complex(b_re.astype(F32), b_im.astype(F32))
    cmat = lax.complex(c_re.astype(F32), c_im.astype(F32))
    tau = jnp.arange(lc + 1, dtype=F32)
    s_idx = jnp.arange(lc)
    m_tot = 0.0
    w_parts, v_parts, a_parts = [], [], []
    for d in range(2):
        lam = lax.complex(a_re[d].astype(F32), a_im[d].astype(F32))
        dt = jnp.exp(log_dt[d].astype(F32))[:, None]
        fac = (jnp.exp(lam * dt) - 1) / lam
        apow = jnp.exp(tau[:, None, None] * (lam * dt)[None])
        bf = bmat * fac[:, :, None]
        cb = cmat.transpose(0, 2, 1)[:, :, :, None] * bf[:, :, None, :]
        kt = jnp.real(jnp.einsum('tgp,gpij->tgij', apow[:lc], cb,
                                 precision=lax.Precision.HIGHEST))
        lag = (s_idx[None, :] - s_idx[:, None]) if d == 0 else (s_idx[:, None] - s_idx[None, :])
        ok = lag >= 0
        kk = kt[jnp.clip(lag, 0, lc - 1)]
        kk = jnp.where(ok[:, :, None, None, None], kk, 0.0)
        m_tot = m_tot + kk.transpose(2, 0, 4, 1, 3).reshape(g_, lc * hsz, lc * hsz)
        pw = apow[(lc - 1 - s_idx) if d == 0 else s_idx]
        wc = pw[:, :, :, None] * bf[None]
        wc = wc.transpose(1, 0, 3, 2).reshape(g_, lc * hsz, p_)
        pv = apow[(s_idx + 1) if d == 0 else (lc - s_idx)]
        vc = cmat[None] * pv[:, :, None, :]
        vc = vc.transpose(1, 3, 0, 2).reshape(g_, p_, lc * hsz)
        pad_w = jnp.zeros((g_, lc * hsz, LANE - p_), F32)
        pad_v = jnp.zeros((g_, LANE - p_, lc * hsz), F32)
        w_parts += [jnp.real(wc), pad_w, jnp.imag(wc), pad_w]
        v_parts += [jnp.real(vc), pad_v, -jnp.imag(vc), pad_v]
        a_chunk = apow[lc]
        pad_a = jnp.zeros((g_, LANE - p_), F32)
        a_parts += [jnp.concatenate([jnp.real(a_chunk), pad_a], axis=1),
                    jnp.concatenate([jnp.imag(a_chunk), pad_a], axis=1)]
    w_all = jnp.concatenate(w_parts, axis=2)
    v_all = jnp.concatenate(v_parts, axis=1)
    a_all = jnp.stack(a_parts + [jnp.zeros_like(a_parts[0])] * 4, axis=1)
    return m_tot.astype(BF16), w_all.astype(BF16), v_all.astype(BF16), a_all


def _s5_kernel(*refs, nb, n_lat, n_ctx, hsz):
    lc = S5_CHUNK
    x_refs = refs[:lc]
    m_ref, w_ref, v_ref, a_ref, o_ref = refs[lc:lc + 5]
    z_refs = refs[lc + 5:lc + 9]
    sp_refs = refs[lc + 9:lc + 13]
    k = pl.program_id(0) % (LANE // hsz)
    cw = lc * hsz
    ri = lax.broadcasted_iota(jnp.int32, (lc * LANE, cw), 0)
    ci = lax.broadcasted_iota(jnp.int32, (lc * LANE, cw), 1)
    sel = jnp.where(ri == (ci // hsz) * LANE + k * hsz + ci % hsz, 1.0, 0.0).astype(BF16)
    ri_t = lax.broadcasted_iota(jnp.int32, (cw, lc * LANE), 0)
    ci_t = lax.broadcasted_iota(jnp.int32, (cw, lc * LANE), 1)
    sel_t = jnp.where(ci_t == (ri_t // hsz) * LANE + k * hsz + ri_t % hsz, 1.0, 0.0).astype(BF16)

    x = jnp.concatenate([r[...] for r in x_refs], axis=1).astype(BF16)
    ub = jnp.dot(x, sel, preferred_element_type=F32).astype(BF16)
    z = jnp.dot(ub, w_ref[...], preferred_element_type=F32)
    for q in range(4):
        z_refs[q][...] = z[:, q * LANE:(q + 1) * LANE]
    n_chunks = n_lat + n_ctx

    def rows_of(chunk):
        if chunk < n_lat:
            return pl.ds(chunk, nb, stride=n_lat)
        return pl.ds(nb * n_lat + chunk - n_lat, nb, stride=n_ctx)

    a = a_ref[...]
    coef = [(a[2 * d:2 * d + 1, :], a[2 * d + 1:2 * d + 2, :]) for d in range(2)]
    state = [(jnp.zeros((nb, LANE), F32), jnp.zeros((nb, LANE), F32)) for _ in range(2)]
    for t in range(n_chunks):
        chunk = ((t + n_lat) % n_chunks, n_chunks - 1 - t)
        for d in range(2):
            rs = rows_of(chunk[d])
            a_r, a_i = coef[d]
            s_re, s_im = state[d]
            sp_refs[2 * d][rs, :] = s_re
            sp_refs[2 * d + 1][rs, :] = s_im
            z_re = z_refs[2 * d][rs, :]
            z_im = z_refs[2 * d + 1][rs, :]
            state[d] = (a_r * s_re - a_i * s_im + z_re, a_r * s_im + a_i * s_re + z_im)
    sp = jnp.concatenate([r[...] for r in sp_refs], axis=1)
    y = (jnp.dot(ub, m_ref[...], preferred_element_type=F32)
         + jnp.dot(sp.astype(BF16), v_ref[...], preferred_element_type=F32))
    y_hi = y.astype(BF16)
    y_lo = (y - y_hi.astype(F32)).astype(BF16)
    back = (jnp.dot(y_hi, sel_t, preferred_element_type=F32)
            + jnp.dot(y_lo, sel_t, preferred_element_type=F32))

    @pl.when(k == 0)
    def _():
        o_ref[...] = jnp.zeros_like(o_ref)

    for s in range(lc):
        o_ref[s] += back[:, s * LANE:(s + 1) * LANE]


def _s5_scan(p2v, tables, *, nb, n_lat, n_ctx, n_groups, hsz, row_w):
    m_tot, w_all, v_all, a_all = tables
    rows = p2v.shape[0]
    lc = S5_CHUNK
    cw = lc * hsz
    gpt = LANE // hsz
    tiles_per_tok = row_w // LANE

    def x_spec(s):
        return pl.BlockSpec((rows, LANE), lambda g: (0, s * tiles_per_tok + g // gpt))

    return pl.pallas_call(
        functools.partial(_s5_kernel, nb=nb, n_lat=n_lat, n_ctx=n_ctx, hsz=hsz),
        out_shape=jax.ShapeDtypeStruct((lc, rows, n_groups * hsz), F32),
        grid=(n_groups,),
        in_specs=[x_spec(s) for s in range(lc)] + [
            pl.BlockSpec((None, cw, cw), lambda g: (g, 0, 0)),
            pl.BlockSpec((None, cw, 4 * LANE), lambda g: (g, 0, 0)),
            pl.BlockSpec((None, 4 * LANE, cw), lambda g: (g, 0, 0)),
            pl.BlockSpec((None, 8, LANE), lambda g: (g, 0, 0))],
        out_specs=pl.BlockSpec((lc, rows, LANE), lambda g: (0, 0, g // gpt)),
        scratch_shapes=[pltpu.VMEM((rows, LANE), F32)] * 8,
        compiler_params=_cparams("arbitrary"),
        name="s5_scan",
    )(*([p2v] * lc), m_tot, w_all, v_all, a_all)


def _glu_kernel(y_ref, p_ref, d_ref, w_ref, b_ref, o_ref):
    width = y_ref.shape[1]
    g = jax.nn.gelu(y_ref[...] + d_ref[...] * p_ref[:, :width])
    z = jnp.dot(g.astype(BF16), w_ref[...], preferred_element_type=F32) + b_ref[...]
    o_ref[...] = (g * jax.nn.sigmoid(z)).astype(o_ref.dtype)


def _glu(y, p2, d_skip, glu_w, glu_b, tm):
    r, w = y.shape
    return pl.pallas_call(
        _glu_kernel,
        out_shape=jax.ShapeDtypeStruct((r, w), BF16),
        grid=(r // tm,),
        in_specs=[pl.BlockSpec((tm, w), lambda i: (i, 0)),
                  pl.BlockSpec((tm, p2.shape[1]), lambda i: (i, 0)),
                  pl.BlockSpec((1, w), lambda i: (0, 0)),
                  pl.BlockSpec((w, w), lambda i: (0, 0)),
                  pl.BlockSpec((1, w), lambda i: (0, 0))],
        out_specs=pl.BlockSpec((tm, w), lambda i: (i, 0)),
        compiler_params=_cparams("parallel"),
        name="s5_glu",
    )(y, p2, d_skip.astype(F32).reshape(1, w), glu_w.astype(BF16), glu_b.astype(F32).reshape(1, w))


def _na_variants(rows, kh):
    n_blk = rows // NA_QROWS
    out = []
    for i in (0, min(1, n_blk - 1), n_blk - 1):
        out.append((i, min(max(i * NA_QROWS - kh // 2, 0), rows - NA_KROWS)))
    return out


def _na_tables(rpb, rows):
    n_heads, nri, nci = rpb.shape
    win_h, win_w = (nri + 1) // 2, (nci + 1) // 2
    kh = min(win_h, rows)
    col = np.arange(GRID_W)
    col_start = np.clip(col - win_w // 2, 0, GRID_W - win_w)
    col_ok = (col[None, :] >= col_start[:, None]) & (col[None, :] < col_start[:, None] + win_w)
    ci = np.clip(col[None, :] - col[:, None], -(win_w - 1), win_w - 1) + (win_w - 1)
    onehot = (ci[None] == np.arange(nci)[:, None, None]).astype(np.float32)
    t1 = jnp.einsum('hrc,cqk->hrqk', rpb.astype(F32), onehot, precision=lax.Precision.HIGHEST)
    t1 = jnp.where(col_ok[None, None], t1, -jnp.inf)
    dummy = jnp.full((n_heads, 1, GRID_W, GRID_W), -jnp.inf, F32)
    t1 = jnp.concatenate([dummy, t1, dummy], axis=1)
    t2 = jnp.concatenate([t1[:, :nri + 1], t1[:, 1:nri + 2]], axis=-1)
    rmask = np.zeros((3, NA_QROWS, NA_KROWS // 2, 1, 2 * GRID_W), np.float32)
    for v, (i, ks) in enumerate(_na_variants(rows, kh)):
        for a in range(NA_QROWS):
            r_start = min(max(i * NA_QROWS + a - kh // 2, 0), rows - kh)
            for b in range(NA_KROWS):
                if not (r_start <= ks + b < r_start + kh):
                    rmask[v, a, b // 2, 0, (b % 2) * GRID_W:(b % 2 + 1) * GRID_W] = -np.inf
    return t2, jnp.asarray(rmask)


def _na_kernel(q_ref, k_ref, v_ref, kc_ref, vc_ref, t2_ref, rm_ref, o_ref, bias_ref, *, rows, kh, win_h,
               scale):
    n_blk = rows // NA_QROWS
    tq = NA_QROWS * GRID_W
    tk = NA_KROWS * GRID_W
    n_r = t2_ref.shape[0]

    @pl.when(pl.program_id(1) == 0)
    def _():
        for v, (i, ks) in enumerate(_na_variants(rows, kh)):
            for a in range(NA_QROWS):
                for pair in range(NA_KROWS // 2):
                    r = (ks + 2 * pair) - (i * NA_QROWS + a) + (win_h - 1)
                    tile = t2_ref[min(max(r + 1, 0), n_r - 1)] + rm_ref[v, a, pair]
                    bias_ref[v, a * GRID_W:(a + 1) * GRID_W,
                             pair * 2 * GRID_W:(pair + 1) * 2 * GRID_W] = tile * LOG2E

    kc = kc_ref[...]
    vc = vc_ref[...]
    nt = (((1,), (1,)), ((), ()))

    def body(i, carry):
        q0 = pl.multiple_of(i * tq, tq)
        ks = jnp.clip(i * NA_QROWS - kh // 2, 0, rows - NA_KROWS)
        k0 = pl.multiple_of(ks * GRID_W, 4 * GRID_W)
        variant = jnp.where(i == 0, 0, jnp.where(i == n_blk - 1, 2, 1))
        qb = q_ref[pl.ds(q0, tq), :]
        kb = k_ref[pl.ds(k0, tk), :]
        vb = v_ref[pl.ds(k0, tk), :]
        s_loc = lax.dot_general(qb, kb, nt, preferred_element_type=F32) * (scale * LOG2E) + bias_ref[variant]
        s_ctx = lax.dot_general(qb, kc, nt, preferred_element_type=F32) * (scale * LOG2E)
        m = jnp.maximum(jnp.max(s_loc, axis=1, keepdims=True), jnp.max(s_ctx, axis=1, keepdims=True))
        p_loc = jnp.exp2(s_loc - m)
        p_ctx = jnp.exp2(s_ctx - m)
        denom = jnp.sum(p_loc, axis=1, keepdims=True) + jnp.sum(p_ctx, axis=1, keepdims=True)
        acc = (jnp.dot(p_loc.astype(BF16), vb, preferred_element_type=F32)
               + jnp.dot(p_ctx.astype(BF16), vc, preferred_element_type=F32))
        o_ref[pl.ds(q0, tq), :] = (acc / denom).astype(o_ref.dtype)
        return carry

    lax.fori_loop(0, n_blk, body, 0, unroll=NA_UNROLL)


def _na_attention(qkv, t2, rmask, *, n_batch, seq, ctx_len, n_heads, hd, kh, win_h):
    rows = seq // GRID_W
    lat_ctx_blk = (n_batch * seq) // ctx_len
    tq, tk = NA_QROWS * GRID_W, NA_KROWS * GRID_W
    return pl.pallas_call(
        functools.partial(_na_kernel, rows=rows, kh=kh, win_h=win_h, scale=hd ** -0.5),
        out_shape=jax.ShapeDtypeStruct((n_heads, n_batch * seq, hd), BF16),
        grid=(n_heads, n_batch),
        in_specs=[pl.BlockSpec((None, seq, hd), lambda h, b: (h, b, 0)),
                  pl.BlockSpec((None, seq, hd), lambda h, b: (n_heads + h, b, 0)),
                  pl.BlockSpec((None, seq, hd), lambda h, b: (2 * n_heads + h, b, 0)),
                  pl.BlockSpec((None, ctx_len, hd), lambda h, b: (n_heads + h, lat_ctx_blk + b, 0)),
                  pl.BlockSpec((None, ctx_len, hd), lambda h, b: (2 * n_heads + h, lat_ctx_blk + b, 0)),
                  pl.BlockSpec((None,) + t2.shape[1:], lambda h, b: (h, 0, 0, 0)),
                  pl.BlockSpec(rmask.shape, lambda h, b: (0, 0, 0, 0, 0))],
        out_specs=pl.BlockSpec((None, seq, hd), lambda h, b: (h, b, 0)),
        scratch_shapes=[pltpu.VMEM((3, tq, tk), F32)],
        compiler_params=_cparams("arbitrary", "arbitrary"),
        name="na_attention",
    )(qkv, qkv, qkv, qkv, qkv, t2, rmask)


def _ffn_kernel(hx_ref, wg_ref, wa_ref, cw_ref, cb_ref, wd_ref, x_ref, gate_ref, ng_ref, *rest,
                tm, n_lat_rows, seq, ctx_len, post):
    if post == "modulate":
        nsh_ref, nsc_ref, o_ref, h_ref = rest
    else:
        (o_ref,) = rest
    i = pl.program_id(0)
    j = pl.program_id(1)

    @pl.when(j == 0)
    def _():
        o_ref[...] = jnp.zeros_like(o_ref)

    g_all = jnp.dot(hx_ref[...], wg_ref[...], preferred_element_type=F32)
    a = jnp.dot(hx_ref[:tm, :], wa_ref[...], preferred_element_type=F32)
    g = g_all[:tm]
    local = lax.broadcasted_iota(jnp.int32, (tm, 1), 0)
    grow = i * tm + local
    in_lat = grow < n_lat_rows
    pos = jnp.where(in_lat, grow % seq, (grow - n_lat_rows) % ctx_len)
    last = jnp.where(in_lat, seq - 1, ctx_len - 1)
    g_prev = jnp.where(local == 0, g_all[tm + 7:tm + 8, :], pltpu.roll(g, 1, axis=0))
    g_prev = jnp.where(pos == 0, 0.0, g_prev)
    g_next = jnp.where(local == tm - 1, g_all[tm + 8:tm + 9, :], pltpu.roll(g, tm - 1, axis=0))
    g_next = jnp.where(pos == last, 0.0, g_next)
    cw = cw_ref[...]
    gl = jax.nn.gelu(cw[0:1, :] * g_prev + cw[1:2, :] * g + cw[2:3, :] * g_next + cb_ref[...])
    o_ref[...] += jnp.dot((gl * a).astype(BF16), wd_ref[...], preferred_element_type=F32)

    @pl.when(j == pl.num_programs(1) - 1)
    def _():
        rows_per = 64

        def slab(c, carry):
            rs = pl.ds(pl.multiple_of(c * rows_per, rows_per), rows_per)
            xn = x_ref[rs, :] + gate_ref[...] * o_ref[rs, :]
            y = xn * lax.rsqrt(jnp.mean(xn * xn, axis=-1, keepdims=True) + EPS) * ng_ref[...]
            if post == "modulate":
                o_ref[rs, :] = xn
                h_ref[rs, :] = (y * (1.0 + nsc_ref[...]) + nsh_ref[...]).astype(h_ref.dtype)
            else:
                o_ref[rs, :] = y
            return carry

        lax.fori_loop(0, tm // rows_per, slab, 0)


def _conv_ffn(hx, xs, weights, mods, gate_idx, seg, next_gain, next_mods=None, *, tm, tf, n_lat_rows,
              seq, ctx_len):
    w_g, w_a, cw8, cb, w_down = weights
    n_blk, _, d = hx.shape
    r = n_blk * tm
    dffp = w_down.shape[0]
    nj = dffp // tf
    once = pl.Buffered(1)
    post = "norm" if next_mods is None else "modulate"
    in_specs = [pl.BlockSpec((None, tm + 16, d), lambda i, j: (i, 0, 0), pipeline_mode=once),
                pl.BlockSpec((d, tf), lambda i, j: (0, j)),
                pl.BlockSpec((d, tf), lambda i, j: (0, j)),
                pl.BlockSpec((8, tf), lambda i, j: (0, j)),
                pl.BlockSpec((1, tf), lambda i, j: (0, j)),
                pl.BlockSpec((tf, d), lambda i, j: (j, 0)),
                pl.BlockSpec((tm, d), lambda i, j: (i, 0), pipeline_mode=once),
                pl.BlockSpec((None, None, 1, d), lambda i, j: (seg(i), gate_idx, 0, 0)),
                pl.BlockSpec((1, d), lambda i, j: (0, 0))]
    args = [hx, w_g, w_a, cw8, cb, w_down, xs, mods, next_gain.reshape(1, d)]
    out_spec = pl.BlockSpec((tm, d), lambda i, j: (i, 0), pipeline_mode=once)
    out_shape = jax.ShapeDtypeStruct((r, d), F32)
    if post == "modulate":
        in_specs += [pl.BlockSpec((None, None, 1, d), lambda i, j: (seg(i), 0, 0, 0)),
                     pl.BlockSpec((None, None, 1, d), lambda i, j: (seg(i), 1, 0, 0))]
        args += [next_mods, next_mods]
        out_spec = [out_spec, pl.BlockSpec((tm, d), lambda i, j: (i, 0), pipeline_mode=once)]
        out_shape = [out_shape, jax.ShapeDtypeStruct((r, d), BF16)]
    return pl.pallas_call(
        functools.partial(_ffn_kernel, tm=tm, n_lat_rows=n_lat_rows, seq=seq, ctx_len=ctx_len, post=post),
        out_shape=out_shape,
        grid=(n_blk, nj),
        in_specs=in_specs,
        out_specs=out_spec,
        compiler_params=_cparams("parallel", "arbitrary"),
        name="conv_ffn",
    )(*args)


def _pad_ffn_weights(w_up, conv_w, conv_b, w_down, layer, tf):
    dff = w_down.shape[1]
    dffp = dff + (-dff % tf)
    bw = _pick(dff, (256, 128))
    w_a = _cast_cols(w_up, layer, 0, dff, dffp, bw)
    w_g = _cast_cols(w_up, layer, dff, dff, dffp, bw)
    cw8 = jnp.pad(conv_w[layer].astype(F32), ((0, 8 - conv_w.shape[1]), (0, dffp - dff)))
    cb = jnp.pad(conv_b[layer].astype(F32), (0, dffp - dff)).reshape(1, dffp)
    w_down_p = _cast_rows(w_down, layer, dffp, bw)
    return w_g, w_a, cw8, cb, w_down_p


def kernel(x, c, ctx, c_ctx, mod_w, mod_b, norm_mix_g, norm_ffn_g, ab_w_in, mlstm_gate_b, mlstm_head_g, s5_a_re, s5_a_im, s5_log_dt, s5_b_re, s5_b_im, s5_c_re, s5_c_im, s5_d, s5_glu_w, s5_glu_b, ab_w_out, na_w_qkv, na_rpb, na_w_out, ffn_w_up, ffn_conv_w, ffn_conv_b, ffn_w_down, final_norm_g):
    n_batch, seq, d = x.shape
    ctx_len = ctx.shape[1]
    depth = mod_w.shape[0]
    assert depth == 2 and mod_w.shape[2] == 6 * d
    n_lat_rows = n_batch * seq
    n_ctx_rows = n_batch * ctx_len
    assert seq % MLSTM_CHUNK == 0 and ctx_len == MLSTM_CHUNK and seq % GRID_W == 0

    n_heads = mlstm_gate_b.shape[-1]
    s5w = s5_d.shape[-1]
    mw = d - s5w
    dv = mw // n_heads
    dqk = dv // 2
    qkw = n_heads * dqk
    n_groups, p_state, s5_group = s5_b_re.shape[1:]
    assert S5_CHUNK * s5_group == 2 * LANE and p_state <= LANE and dqk % (2 * LANE) == 0
    na_heads = na_rpb.shape[1]
    hd = d // na_heads
    rows = seq // GRID_W
    kh = min((na_rpb.shape[2] + 1) // 2, rows)
    assert rows % NA_QROWS == 0 and rows >= NA_KROWS and hd == LANE
    dff = ffn_w_down.shape[1]

    tm_big = _pick(math.gcd(n_lat_rows, n_ctx_rows), (1024, 512, 256))
    tm_ffn = _pick(math.gcd(n_lat_rows, n_ctx_rows), (512, 256))
    tf = 512
    seg_big = _seg_fn(tm_big, n_lat_rows, seq, n_batch)
    seg_ffn = _seg_fn(tm_ffn, n_lat_rows, seq, n_batch)
    seg_256 = _seg_fn(256, n_lat_rows, seq, n_batch)

    cond8 = jnp.concatenate([c, c_ctx[None], jnp.zeros((8 - n_batch - 1, d), F32)], axis=0)
    mods = _ada(cond8, mod_w, mod_b).reshape(depth, 8, 6, 1, d)

    x2d, c2d = x.reshape(n_lat_rows, d), ctx.reshape(n_ctx_rows, d)

    h = _modulate2(x2d, c2d, norm_mix_g[0], mods[0], 0, 1, seg_256, 256)
    w_in = ab_w_in[0]
    n_gate = 4 * n_heads
    w1 = _cast_cols(ab_w_in, 0, 0, 2 * qkw + 2 * mw, 2 * qkw + 2 * mw, 512)
    w2 = jnp.concatenate([w_in[:, 2 * qkw + 2 * mw + n_gate:], w_in[:, 2 * qkw + 2 * mw:2 * qkw + 2 * mw + n_gate],
                          jnp.zeros((d, LANE - n_gate), F32)], axis=1).astype(BF16)
    p1 = _matmul(h, w1, BF16, tm_big, _pick(w1.shape[1], (512, 256, 128)), "ab_in_proj")
    p2 = _matmul(h, w2, F32, tm_ffn, w2.shape[1], "ab_in_proj_s5")
    grow = jnp.pad(p2[:, s5w:s5w + n_gate].T, ((0, -n_gate % 8), (0, 0)))

    cos_t, sin_t = _rope_tables(seq, ctx_len, dqk)
    common = dict(n_batch=n_batch, seq=seq, n_heads=n_heads, dqk=dqk, dv=dv)
    hf = _mlstm_direction(p1, p2, grow, mlstm_gate_b[0], cos_t, sin_t, None, None, reverse=False, **common)
    mix_a = _mlstm_direction(p1, p2, grow, mlstm_gate_b[0], cos_t, sin_t, mlstm_head_g[0], hf,
                             reverse=True, **common)

    n_lat_ch, n_ctx_ch = seq // S5_CHUNK, ctx_len // S5_CHUNK
    row_w = p2.shape[1]
    tables = _s5_tables(s5_a_re[0], s5_a_im[0], s5_log_dt[0], s5_b_re[0], s5_b_im[0], s5_c_re[0], s5_c_im[0])
    y16 = _s5_scan(p2.reshape(-1, S5_CHUNK * row_w), tables, nb=n_batch, n_lat=n_lat_ch, n_ctx=n_ctx_ch,
                   n_groups=n_groups, hsz=s5_group, row_w=row_w)
    y_rows = y16.transpose(1, 0, 2).reshape(-1, s5w)
    mix_b = _glu(y_rows, p2, s5_d[0], s5_glu_w[0], s5_glu_b[0], tm_ffn)

    w_out = ab_w_out[0].astype(BF16)
    xs = _matmul_residual([mix_a, mix_b], [w_out[:mw], w_out[mw:]], (x2d, c2d), mods[0], 2, seg_big,
                          tm_big, 512, "ab_out_proj")
    h = _modulate_halo(xs, norm_ffn_g[0], mods[0], 3, 4, seg_ffn, tm_ffn)
    ffn_kw = dict(tm=tm_ffn, tf=tf, n_lat_rows=n_lat_rows, seq=seq, ctx_len=ctx_len)
    xs, h = _conv_ffn(h, xs, _pad_ffn_weights(ffn_w_up, ffn_conv_w, ffn_conv_b, ffn_w_down, 0, tf),
                      mods[0], 5, seg_ffn, norm_mix_g[1], mods[1], **ffn_kw)

    qkv = _matmul(h, na_w_qkv[0].astype(BF16), BF16, tm_big, 512, "na_qkv_proj", unit=hd)
    t2, rmask = _na_tables(na_rpb[0], rows)
    o = _na_attention(qkv, t2, rmask, n_batch=n_batch, seq=seq, ctx_len=ctx_len, n_heads=na_heads, hd=hd,
                      kh=kh, win_h=(na_rpb.shape[2] + 1) // 2)
    xl = _matmul_residual([o], [na_w_out[0].astype(BF16)], xs, mods[1], 2, seg_big, tm_big, 512,
                          "na_out_proj")
    h = _modulate_halo(xl, norm_ffn_g[1], mods[1], 3, 4, seg_ffn, tm_ffn)
    out = _conv_ffn(h, xl, _pad_ffn_weights(ffn_w_up, ffn_conv_w, ffn_conv_b, ffn_w_down, 1, tf),
                    mods[1], 5, seg_ffn, final_norm_g, **ffn_kw)
    return out.reshape(n_batch, seq, d)
```

```python
import functools
import math

import jax
import jax.numpy as jnp
import numpy as np
from jax import lax
from jax.experimental import pallas as pl
from jax.experimental.pallas import tpu as pltpu

F32 = jnp.float32
BF16 = jnp.bfloat16

EPS = 1e-6
ROPE_BASE = 10000.0
GRID_W = 64
MLSTM_CHUNK = 256
S5_CHUNK = 16
NA_QROWS = 4
NA_KROWS = 12
NA_UNROLL = 4
LOG2E = 1.4426950408889634
LANE = 128
VMEM_LIMIT_BYTES = 58 * 1024 * 1024


def _cparams(*sem):
    return pltpu.CompilerParams(dimension_semantics=sem, vmem_limit_bytes=VMEM_LIMIT_BYTES)


def _pick(n, candidates):
    for c in candidates:
        if n % c == 0:
            return c
    raise ValueError(f"no tile in {candidates} divides {n}")


def _ada_kernel(c_ref, w_ref, b_ref, o_ref):
    s = jax.nn.silu(c_ref[...]).astype(BF16)
    o_ref[...] = jnp.dot(s, w_ref[...].astype(BF16), preferred_element_type=F32) + b_ref[...]


def _ada(cond8, mod_w, mod_b):
    depth, d, n6 = mod_w.shape
    tn = _pick(n6, (512, 256, 128))
    return pl.pallas_call(
        _ada_kernel,
        out_shape=jax.ShapeDtypeStruct((depth, 8, n6), F32),
        grid=(depth, n6 // tn),
        in_specs=[pl.BlockSpec((8, d), lambda l, j: (0, 0)),
                  pl.BlockSpec((None, d, tn), lambda l, j: (l, 0, j)),
                  pl.BlockSpec((None, 1, tn), lambda l, j: (l, 0, j))],
        out_specs=pl.BlockSpec((None, 8, tn), lambda l, j: (l, 0, j)),
        compiler_params=_cparams("parallel", "parallel"),
        name="ada_mod",
    )(cond8, mod_w, mod_b.reshape(depth, 1, n6))


def _seg_fn(tm, n_lat_rows, seq, n_batch):
    n_lat_blk = n_lat_rows // tm

    def seg(i):
        return jnp.where(i < n_lat_blk, (i * tm) // seq, n_batch)
    return seg


def _two_source_specs(block, n_first_blk, col=None):
    if col is None:
        return (pl.BlockSpec(block, lambda i: (jnp.minimum(i, n_first_blk - 1), 0)),
                pl.BlockSpec(block, lambda i: (jnp.maximum(i - n_first_blk, 0), 0)))
    return (pl.BlockSpec(block, lambda i, j: (jnp.minimum(i, n_first_blk - 1), jnp.where(i < n_first_blk, j, 0))),
            pl.BlockSpec(block, lambda i, j: (jnp.maximum(i - n_first_blk, 0), jnp.where(i < n_first_blk, 0, j))))


def _modulate2_kernel(x_ref, c_ref, g_ref, sh_ref, sc_ref, o_ref, *, n_first_blk):
    x = jnp.where(pl.program_id(0) < n_first_blk, x_ref[...], c_ref[...])
    y = x * lax.rsqrt(jnp.mean(x * x, axis=-1, keepdims=True) + EPS) * g_ref[...]
    o_ref[...] = (y * (1.0 + sc_ref[...]) + sh_ref[...]).astype(o_ref.dtype)


def _modulate2(x2d, c2d, gain, mods, shift_idx, scale_idx, seg, tm):
    d = x2d.shape[1]
    r = x2d.shape[0] + c2d.shape[0]
    n_first_blk = x2d.shape[0] // tm
    return pl.pallas_call(
        functools.partial(_modulate2_kernel, n_first_blk=n_first_blk),
        out_shape=jax.ShapeDtypeStruct((r, d), BF16),
        grid=(r // tm,),
        in_specs=[*_two_source_specs((tm, d), n_first_blk),
                  pl.BlockSpec((1, d), lambda i: (0, 0)),
                  pl.BlockSpec((None, None, 1, d), lambda i: (seg(i), shift_idx, 0, 0)),
                  pl.BlockSpec((None, None, 1, d), lambda i: (seg(i), scale_idx, 0, 0))],
        out_specs=pl.BlockSpec((tm, d), lambda i: (i, 0)),
        compiler_params=_cparams("parallel"),
        name="modulate",
    )(x2d, c2d, gain.reshape(1, d), mods, mods)


def _modulate_halo_kernel(x_ref, xp_ref, xn_ref, g_ref, sh_ref, sc_ref, o_ref):
    x = jnp.concatenate([x_ref[...], xp_ref[...], xn_ref[...]], axis=0)
    y = x * lax.rsqrt(jnp.mean(x * x, axis=-1, keepdims=True) + EPS) * g_ref[...]
    o_ref[...] = (y * (1.0 + sc_ref[...]) + sh_ref[...]).astype(o_ref.dtype)


def _modulate_halo(xs, gain, mods, shift_idx, scale_idx, seg, tm):
    r, d = xs.shape
    n_blk, per8 = r // tm, tm // 8
    return pl.pallas_call(
        _modulate_halo_kernel,
        out_shape=jax.ShapeDtypeStruct((n_blk, tm + 16, d), BF16),
        grid=(n_blk,),
        in_specs=[pl.BlockSpec((tm, d), lambda i: (i, 0)),
                  pl.BlockSpec((8, d), lambda i: (jnp.maximum(i * per8 - 1, 0), 0)),
                  pl.BlockSpec((8, d), lambda i: (jnp.minimum((i + 1) * per8, r // 8 - 1), 0)),
                  pl.BlockSpec((1, d), lambda i: (0, 0)),
                  pl.BlockSpec((None, None, 1, d), lambda i: (seg(i), shift_idx, 0, 0)),
                  pl.BlockSpec((None, None, 1, d), lambda i: (seg(i), scale_idx, 0, 0))],
        out_specs=pl.BlockSpec((None, tm + 16, d), lambda i: (i, 0, 0)),
        compiler_params=_cparams("parallel"),
        name="modulate_halo",
    )(xs, xs, xs, gain.reshape(1, d), mods, mods)


def _cast_kernel(x_ref, o_ref, *, n_valid):
    keep = pl.program_id(0) < n_valid
    o_ref[...] = jnp.where(keep, x_ref[...], 0.0).astype(o_ref.dtype)


def _cast_cols(w, layer, col0, n_cols, n_cols_pad, bw):
    d = w.shape[1]
    nv, off = n_cols // bw, col0 // bw
    assert n_cols % bw == 0 and col0 % bw == 0 and n_cols_pad % bw == 0
    return pl.pallas_call(
        functools.partial(_cast_kernel, n_valid=nv),
        out_shape=jax.ShapeDtypeStruct((n_cols_pad // bw, d, bw), BF16),
        grid=(n_cols_pad // bw,),
        in_specs=[pl.BlockSpec((None, d, bw), lambda i: (layer, 0, off + jnp.minimum(i, nv - 1)))],
        out_specs=pl.BlockSpec((None, d, bw), lambda i: (i, 0, 0)),
        compiler_params=_cparams("parallel"),
        name="cast_cols",
    )(w)


def _cast_rows(w, layer, n_rows_pad, br):
    n_rows, d = w.shape[1:]
    nv = n_rows // br
    assert n_rows % br == 0 and n_rows_pad % br == 0
    return pl.pallas_call(
        functools.partial(_cast_kernel, n_valid=nv),
        out_shape=jax.ShapeDtypeStruct((n_rows_pad, d), BF16),
        grid=(n_rows_pad // br,),
        in_specs=[pl.BlockSpec((None, br, d), lambda i: (layer, jnp.minimum(i, nv - 1), 0))],
        out_specs=pl.BlockSpec((br, d), lambda i: (i, 0)),
        compiler_params=_cparams("parallel"),
        name="cast_rows",
    )(w)


def _mm_kernel(a_ref, b_ref, o_ref):
    res = jnp.dot(a_ref[...], b_ref[...], preferred_element_type=F32).astype(o_ref.dtype)
    if len(o_ref.shape) == 3:
        w = o_ref.shape[2]
        for t in range(o_ref.shape[0]):
            o_ref[t] = res[:, t * w:(t + 1) * w]
    else:
        o_ref[...] = res


def _matmul(a, b, out_dtype, tm, tn, name, unit=None):
    m, k = a.shape
    n = b.shape[1]
    if unit is None:
        out_shape = jax.ShapeDtypeStruct((m, n), out_dtype)
        out_spec = pl.BlockSpec((tm, tn), lambda i, j: (i, j))
    else:
        out_shape = jax.ShapeDtypeStruct((n // unit, m, unit), out_dtype)
        out_spec = pl.BlockSpec((tn // unit, tm, unit), lambda i, j: (j, i, 0))
    return pl.pallas_call(
        _mm_kernel,
        out_shape=out_shape,
        grid=(m // tm, n // tn),
        in_specs=[pl.BlockSpec((tm, k), lambda i, j: (i, 0)),
                  pl.BlockSpec((k, tn), lambda i, j: (0, j))],
        out_specs=out_spec,
        compiler_params=_cparams("parallel", "parallel"),
        name=name,
    )(a, b)


def _rows_by_unit(ref):
    if len(ref.shape) == 2:
        return ref[...]
    return jnp.concatenate([ref[t] for t in range(ref.shape[0])], axis=1)


def _mm_res_kernel(*refs, n_a, n_first_blk):
    a_refs, b_refs = refs[:n_a], refs[n_a:2 * n_a]
    if n_first_blk is None:
        x_ref, gate_ref, o_ref = refs[2 * n_a:]
        x = x_ref[...]
    else:
        x_ref, c_ref, gate_ref, o_ref = refs[2 * n_a:]
        x = jnp.where(pl.program_id(0) < n_first_blk, x_ref[...], c_ref[...])
    acc = jnp.dot(_rows_by_unit(a_refs[0]), b_refs[0][...], preferred_element_type=F32)
    for a_ref, b_ref in zip(a_refs[1:], b_refs[1:]):
        acc = acc + jnp.dot(_rows_by_unit(a_ref), b_ref[...], preferred_element_type=F32)
    o_ref[...] = x + gate_ref[...] * acc


def _matmul_residual(a_list, b_list, xs, mods, gate_idx, seg, tm, tn, name):
    m = a_list[0].shape[-2]
    n = b_list[0].shape[1]

    def a_spec(a):
        if a.ndim == 2:
            return pl.BlockSpec((tm, a.shape[1]), lambda i, j: (i, 0))
        return pl.BlockSpec((a.shape[0], tm, a.shape[2]), lambda i, j: (0, i, 0))

    if isinstance(xs, tuple):
        n_first_blk = xs[0].shape[0] // tm
        x_specs = list(_two_source_specs((tm, tn), n_first_blk, col=True))
        x_args = list(xs)
    else:
        n_first_blk = None
        x_specs = [pl.BlockSpec((tm, tn), lambda i, j: (i, j))]
        x_args = [xs]
    in_specs = ([a_spec(a) for a in a_list]
                + [pl.BlockSpec((b.shape[0], tn), lambda i, j: (0, j)) for b in b_list]
                + x_specs
                + [pl.BlockSpec((None, None, 1, tn), lambda i, j: (seg(i), gate_idx, 0, j))])
    return pl.pallas_call(
        functools.partial(_mm_res_kernel, n_a=len(a_list), n_first_blk=n_first_blk),
        out_shape=jax.ShapeDtypeStruct((m, n), F32),
        grid=(m // tm, n // tn),
        in_specs=in_specs,
        out_specs=pl.BlockSpec((tm, tn), lambda i, j: (i, j)),
        compiler_params=_cparams("parallel", "parallel"),
        name=name,
    )(*a_list, *b_list, *x_args, mods)


def _rope(x, cos, sin):
    parts = []
    for j in range(x.shape[1] // LANE):
        sl = slice(j * LANE, (j + 1) * LANE)
        xs = x[:, sl]
        parts.append(xs * cos[:, sl] + pltpu.roll(xs, LANE // 2, axis=1) * sin[:, sl])
    return jnp.concatenate(parts, axis=1)


def _mlstm_kernel(gb_ref, q_ref, k_ref, v_ref, cos_ref, sin_ref, gcol_ref, grow_ref, *rest,
                  reverse, n_heads, scale):
    if reverse:
        o_ref, hf_ref, hg_ref, out_ref, ct_ref, m_ref = rest
    else:
        out_ref, ct_ref, m_ref = rest
    step = pl.program_id(1)
    d = 1 if reverse else 0
    L = q_ref.shape[0]
    dqk = q_ref.shape[1] // n_heads
    dv = v_ref.shape[1] // n_heads

    @pl.when(step == 0)
    def _():
        ct_ref[...] = jnp.zeros_like(ct_ref)
        m_ref[...] = jnp.zeros_like(m_ref)

    row = lax.broadcasted_iota(jnp.int32, (L, L), 0)
    col = lax.broadcasted_iota(jnp.int32, (L, L), 1)
    incl = (col >= row) if reverse else (col <= row)
    incl_t = (row >= col) if reverse else (row <= col)
    n_lat = pl.num_programs(1) - 1
    tab = jnp.where(step == 0, n_lat, (n_lat - step) if reverse else (step - 1))
    t0 = pl.multiple_of(tab * L, L)
    cos = cos_ref[pl.ds(t0, L), :]
    sin = sin_ref[pl.ds(t0, L), :]
    ones = jnp.ones((L, LANE), v_ref.dtype)

    for hh in range(n_heads):
        gi = (d * 2 + 0) * n_heads + hh
        gf = (d * 2 + 1) * n_heads + hh
        b_i = gb_ref[gi]
        b_f = gb_ref[gf]
        li_col = gcol_ref[:, gi:gi + 1] + b_i
        lf_col = jax.nn.log_sigmoid(gcol_ref[:, gf:gf + 1] + b_f)
        li_row = grow_ref[gi:gi + 1, :] + b_i
        lf_row = jax.nn.log_sigmoid(grow_ref[gf:gf + 1, :] + b_f)
        cum_col = jnp.sum(jnp.where(incl, lf_row, 0.0), axis=1, keepdims=True)
        cum_row = jnp.sum(jnp.where(incl_t, lf_col, 0.0), axis=0, keepdims=True)
        total = jnp.sum(lf_row, axis=1, keepdims=True)
        m_prev = m_ref[hh, 0:1, 0:1]

        dmat = jnp.where(incl, cum_col - cum_row + li_row, -jnp.inf)
        carried = cum_col + m_prev
        m_loc = jnp.maximum(carried, jnp.max(dmat, axis=1, keepdims=True))
        w = jnp.exp(dmat - m_loc)
        w_state = jnp.exp(carried - m_loc)

        qs = slice(hh * dqk, (hh + 1) * dqk)
        vs = slice(hh * dv, (hh + 1) * dv)
        qb = (_rope(q_ref[:, qs].astype(F32), cos, sin) * scale).astype(BF16)
        kb = _rope(k_ref[:, qs].astype(F32), cos, sin).astype(BF16)
        s_qk = lax.dot_general(qb, kb, (((1,), (1,)), ((), ())), preferred_element_type=F32)
        sc = (s_qk * w).astype(BF16)
        vext = jnp.concatenate([v_ref[:, vs], ones], axis=1)
        ct = ct_ref[hh]
        res = (jnp.dot(sc, vext, preferred_element_type=F32)
               + w_state * jnp.dot(qb, ct.astype(BF16), preferred_element_type=F32))
        den = res[:, dv:dv + 1]
        hval = res[:, :dv] / jnp.maximum(jnp.abs(den), jnp.exp(-m_loc))

        src = total - cum_col + li_col
        m_new = jnp.maximum(total + m_prev, jnp.max(src, axis=0, keepdims=True))
        decay = jnp.exp(total + m_prev - m_new)
        w_src = jnp.exp(src - m_new)
        upd = lax.dot_general(kb, (w_src * vext.astype(F32)).astype(BF16), (((0,), (0,)), ((), ())),
                              preferred_element_type=F32)
        ct_ref[hh] = decay * ct + upd
        m_ref[hh] = jnp.broadcast_to(m_new, m_ref.shape[1:])

        if reverse:
            hm = hf_ref[:, vs] + hval
            hn = hm * lax.rsqrt(jnp.mean(hm * hm, axis=-1, keepdims=True) + EPS) * hg_ref[:, vs]
            out_ref[:, vs] = (jax.nn.sigmoid(o_ref[:, vs].astype(F32)) * hn).astype(out_ref.dtype)
        else:
            out_ref[:, vs] = hval


def _mlstm_direction(p1, p2, grow, gate_b, cos_t, sin_t, head_g, hf, *, reverse, n_batch, seq,
                     n_heads, dqk, dv):
    r = p1.shape[0]
    L = MLSTM_CHUNK
    n_lat = seq // L
    lat_blocks = n_batch * n_lat
    qkw, mw = n_heads * dqk, n_heads * dv
    gate_tile = p2.shape[1] // LANE - 1

    def blk(b, s):
        lat = b * n_lat + ((n_lat - s) if reverse else (s - 1))
        return jnp.where(s == 0, lat_blocks + b, lat)

    once = pl.Buffered(1)
    in_specs = [
        pl.BlockSpec(memory_space=pltpu.SMEM),
        pl.BlockSpec((L, qkw), lambda b, s: (blk(b, s), 0)),
        pl.BlockSpec((L, qkw), lambda b, s: (blk(b, s), 1)),
        pl.BlockSpec((L, mw), lambda b, s: (blk(b, s), (2 * qkw) // mw)),
        pl.BlockSpec(cos_t.shape, lambda b, s: (0, 0), pipeline_mode=once),
        pl.BlockSpec(sin_t.shape, lambda b, s: (0, 0), pipeline_mode=once),
        pl.BlockSpec((L, LANE), lambda b, s: (blk(b, s), gate_tile)),
        pl.BlockSpec((grow.shape[0], L), lambda b, s: (0, blk(b, s))),
    ]
    args = [gate_b.reshape(-1), p1, p1, p1, cos_t, sin_t, p2, grow]
    if reverse:
        in_specs += [
            pl.BlockSpec((L, mw), lambda b, s: (blk(b, s), (2 * qkw) // mw + 1)),
            pl.BlockSpec((L, mw), lambda b, s: (blk(b, s), 0)),
            pl.BlockSpec((1, mw), lambda b, s: (0, 0)),
        ]
        args += [p1, hf, head_g.reshape(1, mw)]
    return pl.pallas_call(
        functools.partial(_mlstm_kernel, reverse=reverse, n_heads=n_heads, scale=dqk ** -0.5),
        out_shape=jax.ShapeDtypeStruct((r, mw), BF16 if reverse else F32),
        grid=(n_batch, n_lat + 1),
        in_specs=in_specs,
        out_specs=pl.BlockSpec((L, mw), lambda b, s: (blk(b, s), 0)),
        scratch_shapes=[pltpu.VMEM((n_heads, dqk, dv + LANE), F32), pltpu.VMEM((n_heads, 8, LANE), F32)],
        compiler_params=_cparams("parallel", "arbitrary"),
        name="mlstm_bwd" if reverse else "mlstm_fwd",
    )(*args)


def _rope_tables(seq, ctx_len, dqk):
    half = dqk // 2
    inv = ROPE_BASE ** (-jnp.arange(0, half, 2, dtype=F32) / half)
    t = jnp.arange(seq)

    def one(pos):
        ang = pos.astype(F32)[:, None] * inv
        c, s = jnp.cos(ang), jnp.sin(ang)
        return jnp.concatenate([c, c], axis=-1), jnp.concatenate([-s, s], axis=-1)

    c_r, s_r = one(t // GRID_W)
    c_c, s_c = one(t % GRID_W)
    cos_t = jnp.concatenate([c_r, c_c], axis=-1)
    sin_t = jnp.concatenate([s_r, s_c], axis=-1)
    cos_t = jnp.concatenate([cos_t, jnp.ones((ctx_len, dqk), F32)], axis=0)
    sin_t = jnp.concatenate([sin_t, jnp.zeros((ctx_len, dqk), F32)], axis=0)
    return cos_t, sin_t


def _s5_tables(a_re, a_im, log_dt, b_re, b_im, c_re, c_im):
    g_, p_, hsz = b_re.shape
    lc = S5_CHUNK
    bmat = lax.complex(b_re.astype(F32), b_im.astype(F32))
    cmat = lax.complex(c_re.astype(F32), c_im.astype(F32))
    tau = jnp.arange(lc + 1, dtype=F32)
    s_idx = jnp.arange(lc)
    m_tot = 0.0
    w_parts, v_parts, a_parts = [], [], []
    for d in range(2):
        lam = lax.complex(a_re[d].astype(F32), a_im[d].astype(F32))
        dt = jnp.exp(log_dt[d].astype(F32))[:, None]
        fac = (jnp.exp(lam * dt) - 1) / lam
        apow = jnp.exp(tau[:, None, None] * (lam * dt)[None])
        bf = bmat * fac[:, :, None]
        cb = cmat.transpose(0, 2, 1)[:, :, :, None] * bf[:, :, None, :]
        kt = jnp.real(jnp.einsum('tgp,gpij->tgij', apow[:lc], cb,
                                 precision=lax.Precision.HIGHEST))
        lag = (s_idx[None, :] - s_idx[:, None]) if d == 0 else (s_idx[:, None] - s_idx[None, :])
        ok = lag >= 0
        kk = kt[jnp.clip(lag, 0, lc - 1)]
        kk = jnp.where(ok[:, :, None, None, None], kk, 0.0)
        m_tot = m_tot + kk.transpose(2, 0, 4, 1, 3).reshape(g_, lc * hsz, lc * hsz)
        pw = apow[(lc - 1 - s_idx) if d == 0 else s_idx]
        wc = pw[:, :, :, None] * bf[None]
        wc = wc.transpose(1, 0, 3, 2).reshape(g_, lc * hsz, p_)
        pv = apow[(s_idx + 1) if d == 0 else (lc - s_idx)]
        vc = cmat[None] * pv[:, :, None, :]
        vc = vc.transpose(1, 3, 0, 2).reshape(g_, p_, lc * hsz)
        pad_w = jnp.zeros((g_, lc * hsz, LANE - p_), F32)
        pad_v = jnp.zeros((g_, LANE - p_, lc * hsz), F32)
        w_parts += [jnp.real(wc), pad_w, jnp.imag(wc), pad_w]
        v_parts += [jnp.real(vc), pad_v, -jnp.imag(vc), pad_v]
        a_chunk = apow[lc]
        pad_a = jnp.zeros((g_, LANE - p_), F32)
        a_parts += [jnp.concatenate([jnp.real(a_chunk), pad_a], axis=1),
                    jnp.concatenate([jnp.imag(a_chunk), pad_a], axis=1)]
    w_all = jnp.concatenate(w_parts, axis=2)
    v_all = jnp.concatenate(v_parts, axis=1)
    a_all = jnp.stack(a_parts + [jnp.zeros_like(a_parts[0])] * 4, axis=1)
    return m_tot.astype(BF16), w_all.astype(BF16), v_all.astype(BF16), a_all


def _s5_kernel(*refs, nb, n_lat, n_ctx, hsz):
    lc = S5_CHUNK
    x_refs = refs[:lc]
    m_ref, w_ref, v_ref, a_ref, o_ref = refs[lc:lc + 5]
    z_refs = refs[lc + 5:lc + 9]
    sp_refs = refs[lc + 9:lc + 13]
    k = pl.program_id(0) % (LANE // hsz)
    cw = lc * hsz
    ri = lax.broadcasted_iota(jnp.int32, (lc * LANE, cw), 0)
    ci = lax.broadcasted_iota(jnp.int32, (lc * LANE, cw), 1)
    sel = jnp.where(ri == (ci // hsz) * LANE + k * hsz + ci % hsz, 1.0, 0.0).astype(BF16)
    ri_t = lax.broadcasted_iota(jnp.int32, (cw, lc * LANE), 0)
    ci_t = lax.broadcasted_iota(jnp.int32, (cw, lc * LANE), 1)
    sel_t = jnp.where(ci_t == (ri_t // hsz) * LANE + k * hsz + ri_t % hsz, 1.0, 0.0).astype(BF16)

    x = jnp.concatenate([r[...] for r in x_refs], axis=1).astype(BF16)
    ub = jnp.dot(x, sel, preferred_element_type=F32).astype(BF16)
    z = jnp.dot(ub, w_ref[...], preferred_element_type=F32)
    for q in range(4):
        z_refs[q][...] = z[:, q * LANE:(q + 1) * LANE]
    n_chunks = n_lat + n_ctx

    def rows_of(chunk):
        if chunk < n_lat:
            return pl.ds(chunk, nb, stride=n_lat)
        return pl.ds(nb * n_lat + chunk - n_lat, nb, stride=n_ctx)

    a = a_ref[...]
    coef = [(a[2 * d:2 * d + 1, :], a[2 * d + 1:2 * d + 2, :]) for d in range(2)]
    state = [(jnp.zeros((nb, LANE), F32), jnp.zeros((nb, LANE), F32)) for _ in range(2)]
    for t in range(n_chunks):
        chunk = ((t + n_lat) % n_chunks, n_chunks - 1 - t)
        for d in range(2):
            rs = rows_of(chunk[d])
            a_r, a_i = coef[d]
            s_re, s_im = state[d]
            sp_refs[2 * d][rs, :] = s_re
            sp_refs[2 * d + 1][rs, :] = s_im
            z_re = z_refs[2 * d][rs, :]
            z_im = z_refs[2 * d + 1][rs, :]
            state[d] = (a_r * s_re - a_i * s_im + z_re, a_r * s_im + a_i * s_re + z_im)
    sp = jnp.concatenate([r[...] for r in sp_refs], axis=1)
    y = (jnp.dot(ub, m_ref[...], preferred_element_type=F32)
         + jnp.dot(sp.astype(BF16), v_ref[...], preferred_element_type=F32))
    y_hi = y.astype(BF16)
    y_lo = (y - y_hi.astype(F32)).astype(BF16)
    back = (jnp.dot(y_hi, sel_t, preferred_element_type=F32)
            + jnp.dot(y_lo, sel_t, preferred_element_type=F32))

    @pl.when(k == 0)
    def _():
        o_ref[...] = jnp.zeros_like(o_ref)

    for s in range(lc):
        o_ref[s] += back[:, s * LANE:(s + 1) * LANE]


def _s5_scan(p2v, tables, *, nb, n_lat, n_ctx, n_groups, hsz, row_w):
    m_tot, w_all, v_all, a_all = tables
    rows = p2v.shape[0]
    lc = S5_CHUNK
    cw = lc * hsz
    gpt = LANE // hsz
    tiles_per_tok = row_w // LANE

    def x_spec(s):
        return pl.BlockSpec((rows, LANE), lambda g: (0, s * tiles_per_tok + g // gpt))

    return pl.pallas_call(
        functools.partial(_s5_kernel, nb=nb, n_lat=n_lat, n_ctx=n_ctx, hsz=hsz),
        out_shape=jax.ShapeDtypeStruct((lc, rows, n_groups * hsz), F32),
        grid=(n_groups,),
        in_specs=[x_spec(s) for s in range(lc)] + [
            pl.BlockSpec((None, cw, cw), lambda g: (g, 0, 0)),
            pl.BlockSpec((None, cw, 4 * LANE), lambda g: (g, 0, 0)),
            pl.BlockSpec((None, 4 * LANE, cw), lambda g: (g, 0, 0)),
            pl.BlockSpec((None, 8, LANE), lambda g: (g, 0, 0))],
        out_specs=pl.BlockSpec((lc, rows, LANE), lambda g: (0, 0, g // gpt)),
        scratch_shapes=[pltpu.VMEM((rows, LANE), F32)] * 8,
        compiler_params=_cparams("arbitrary"),
        name="s5_scan",
    )(*([p2v] * lc), m_tot, w_all, v_all, a_all)


def _glu_kernel(y_ref, p_ref, d_ref, w_ref, b_ref, o_ref):
    width = y_ref.shape[1]
    g = jax.nn.gelu(y_ref[...] + d_ref[...] * p_ref[:, :width])
    z = jnp.dot(g.astype(BF16), w_ref[...], preferred_element_type=F32) + b_ref[...]
    o_ref[...] = (g * jax.nn.sigmoid(z)).astype(o_ref.dtype)


def _glu(y, p2, d_skip, glu_w, glu_b, tm):
    r, w = y.shape
    return pl.pallas_call(
        _glu_kernel,
        out_shape=jax.ShapeDtypeStruct((r, w), BF16),
        grid=(r // tm,),
        in_specs=[pl.BlockSpec((tm, w), lambda i: (i, 0)),
                  pl.BlockSpec((tm, p2.shape[1]), lambda i: (i, 0)),
                  pl.BlockSpec((1, w), lambda i: (0, 0)),
                  pl.BlockSpec((w, w), lambda i: (0, 0)),
                  pl.BlockSpec((1, w), lambda i: (0, 0))],
        out_specs=pl.BlockSpec((tm, w), lambda i: (i, 0)),
        compiler_params=_cparams("parallel"),
        name="s5_glu",
    )(y, p2, d_skip.astype(F32).reshape(1, w), glu_w.astype(BF16), glu_b.astype(F32).reshape(1, w))


def _na_variants(rows, kh):
    n_blk = rows // NA_QROWS
    out = []
    for i in (0, min(1, n_blk - 1), n_blk - 1):
        out.append((i, min(max(i * NA_QROWS - kh // 2, 0), rows - NA_KROWS)))
    return out


def _na_tables(rpb, rows):
    n_heads, nri, nci = rpb.shape
    win_h, win_w = (nri + 1) // 2, (nci + 1) // 2
    kh = min(win_h, rows)
    col = np.arange(GRID_W)
    col_start = np.clip(col - win_w // 2, 0, GRID_W - win_w)
    col_ok = (col[None, :] >= col_start[:, None]) & (col[None, :] < col_start[:, None] + win_w)
    ci = np.clip(col[None, :] - col[:, None], -(win_w - 1), win_w - 1) + (win_w - 1)
    onehot = (ci[None] == np.arange(nci)[:, None, None]).astype(np.float32)
    t1 = jnp.einsum('hrc,cqk->hrqk', rpb.astype(F32), onehot, precision=lax.Precision.HIGHEST)
    t1 = jnp.where(col_ok[None, None], t1, -jnp.inf)
    dummy = jnp.full((n_heads, 1, GRID_W, GRID_W), -jnp.inf, F32)
    t1 = jnp.concatenate([dummy, t1, dummy], axis=1)
    t2 = jnp.concatenate([t1[:, :nri + 1], t1[:, 1:nri + 2]], axis=-1)
    rmask = np.zeros((3, NA_QROWS, NA_KROWS // 2, 1, 2 * GRID_W), np.float32)
    for v, (i, ks) in enumerate(_na_variants(rows, kh)):
        for a in range(NA_QROWS):
            r_start = min(max(i * NA_QROWS + a - kh // 2, 0), rows - kh)
            for b in range(NA_KROWS):
                if not (r_start <= ks + b < r_start + kh):
                    rmask[v, a, b // 2, 0, (b % 2) * GRID_W:(b % 2 + 1) * GRID_W] = -np.inf
    return t2, jnp.asarray(rmask)


def _na_kernel(q_ref, k_ref, v_ref, kc_ref, vc_ref, t2_ref, rm_ref, o_ref, bias_ref, *, rows, kh, win_h,
               scale):
    n_blk = rows // NA_QROWS
    tq = NA_QROWS * GRID_W
    tk = NA_KROWS * GRID_W
    n_r = t2_ref.shape[0]

    @pl.when(pl.program_id(1) == 0)
    def _():
        for v, (i, ks) in enumerate(_na_variants(rows, kh)):
            for a in range(NA_QROWS):
                for pair in range(NA_KROWS // 2):
                    r = (ks + 2 * pair) - (i * NA_QROWS + a) + (win_h - 1)
                    tile = t2_ref[min(max(r + 1, 0), n_r - 1)] + rm_ref[v, a, pair]
                    bias_ref[v, a * GRID_W:(a + 1) * GRID_W,
                             pair * 2 * GRID_W:(pair + 1) * 2 * GRID_W] = tile * LOG2E

    kc = kc_ref[...]
    vc = vc_ref[...]
    nt = (((1,), (1,)), ((), ()))

    def body(i, carry):
        q0 = pl.multiple_of(i * tq, tq)
        ks = jnp.clip(i * NA_QROWS - kh // 2, 0, rows - NA_KROWS)
        k0 = pl.multiple_of(ks * GRID_W, 4 * GRID_W)
        variant = jnp.where(i == 0, 0, jnp.where(i == n_blk - 1, 2, 1))
        qb = q_ref[pl.ds(q0, tq), :]
        kb = k_ref[pl.ds(k0, tk), :]
        vb = v_ref[pl.ds(k0, tk), :]
        s_loc = lax.dot_general(qb, kb, nt, preferred_element_type=F32) * (scale * LOG2E) + bias_ref[variant]
        s_ctx = lax.dot_general(qb, kc, nt, preferred_element_type=F32) * (scale * LOG2E)
        m = jnp.maximum(jnp.max(s_loc, axis=1, keepdims=True), jnp.max(s_ctx, axis=1, keepdims=True))
        p_loc = jnp.exp2(s_loc - m)
        p_ctx = jnp.exp2(s_ctx - m)
        denom = jnp.sum(p_loc, axis=1, keepdims=True) + jnp.sum(p_ctx, axis=1, keepdims=True)
        acc = (jnp.dot(p_loc.astype(BF16), vb, preferred_element_type=F32)
               + jnp.dot(p_ctx.astype(BF16), vc, preferred_element_type=F32))
        o_ref[pl.ds(q0, tq), :] = (acc / denom).astype(o_ref.dtype)
        return carry

    lax.fori_loop(0, n_blk, body, 0, unroll=NA_UNROLL)


def _na_attention(qkv, t2, rmask, *, n_batch, seq, ctx_len, n_heads, hd, kh, win_h):
    rows = seq // GRID_W
    lat_ctx_blk = (n_batch * seq) // ctx_len
    tq, tk = NA_QROWS * GRID_W, NA_KROWS * GRID_W
    return pl.pallas_call(
        functools.partial(_na_kernel, rows=rows, kh=kh, win_h=win_h, scale=hd ** -0.5),
        out_shape=jax.ShapeDtypeStruct((n_heads, n_batch * seq, hd), BF16),
        grid=(n_heads, n_batch),
        in_specs=[pl.BlockSpec((None, seq, hd), lambda h, b: (h, b, 0)),
                  pl.BlockSpec((None, seq, hd), lambda h, b: (n_heads + h, b, 0)),
                  pl.BlockSpec((None, seq, hd), lambda h, b: (2 * n_heads + h, b, 0)),
                  pl.BlockSpec((None, ctx_len, hd), lambda h, b: (n_heads + h, lat_ctx_blk + b, 0)),
                  pl.BlockSpec((None, ctx_len, hd), lambda h, b: (2 * n_heads + h, lat_ctx_blk + b, 0)),
                  pl.BlockSpec((None,) + t2.shape[1:], lambda h, b: (h, 0, 0, 0)),
                  pl.BlockSpec(rmask.shape, lambda h, b: (0, 0, 0, 0, 0))],
        out_specs=pl.BlockSpec((None, seq, hd), lambda h, b: (h, b, 0)),
        scratch_shapes=[pltpu.VMEM((3, tq, tk), F32)],
        compiler_params=_cparams("arbitrary", "arbitrary"),
        name="na_attention",
    )(qkv, qkv, qkv, qkv, qkv, t2, rmask)


def _ffn_kernel(hx_ref, wg_ref, wa_ref, cw_ref, cb_ref, wd_ref, x_ref, gate_ref, ng_ref, *rest,
                tm, n_lat_rows, seq, ctx_len, post):
    if post == "modulate":
        nsh_ref, nsc_ref, o_ref, h_ref = rest
    else:
        (o_ref,) = rest
    i = pl.program_id(0)
    j = pl.program_id(1)

    @pl.when(j == 0)
    def _():
        o_ref[...] = jnp.zeros_like(o_ref)

    g_all = jnp.dot(hx_ref[...], _rows_by_unit(wg_ref), preferred_element_type=F32)
    a = jnp.dot(hx_ref[:tm, :], _rows_by_unit(wa_ref), preferred_element_type=F32)
    g = g_all[:tm]
    local = lax.broadcasted_iota(jnp.int32, (tm, 1), 0)
    grow = i * tm + local
    in_lat = grow < n_lat_rows
    pos = jnp.where(in_lat, grow % seq, (grow - n_lat_rows) % ctx_len)
    last = jnp.where(in_lat, seq - 1, ctx_len - 1)
    g_prev = jnp.where(local == 0, g_all[tm + 7:tm + 8, :], pltpu.roll(g, 1, axis=0))
    g_prev = jnp.where(pos == 0, 0.0, g_prev)
    g_next = jnp.where(local == tm - 1, g_all[tm + 8:tm + 9, :], pltpu.roll(g, tm - 1, axis=0))
    g_next = jnp.where(pos == last, 0.0, g_next)
    cw = cw_ref[...]
    gl = jax.nn.gelu(cw[0:1, :] * g_prev + cw[1:2, :] * g + cw[2:3, :] * g_next + cb_ref[...])
    o_ref[...] += jnp.dot((gl * a).astype(BF16), wd_ref[...], preferred_element_type=F32)

    @pl.when(j == pl.num_programs(1) - 1)
    def _():
        rows_per = 64

        def slab(c, carry):
            rs = pl.ds(pl.multiple_of(c * rows_per, rows_per), rows_per)
            xn = x_ref[rs, :] + gate_ref[...] * o_ref[rs, :]
            y = xn * lax.rsqrt(jnp.mean(xn * xn, axis=-1, keepdims=True) + EPS) * ng_ref[...]
            if post == "modulate":
                o_ref[rs, :] = xn
                h_ref[rs, :] = (y * (1.0 + nsc_ref[...]) + nsh_ref[...]).astype(h_ref.dtype)
            else:
                o_ref[rs, :] = y
            return carry

        lax.fori_loop(0, tm // rows_per, slab, 0)


def _conv_ffn(hx, xs, weights, mods, gate_idx, seg, next_gain, next_mods=None, *, tm, tf, n_lat_rows,
              seq, ctx_len):
    w_g, w_a, cw8, cb, w_down = weights
    n_blk, _, d = hx.shape
    r = n_blk * tm
    dffp = w_down.shape[0]
    nj = dffp // tf
    once = pl.Buffered(1)
    post = "norm" if next_mods is None else "modulate"
    per_tile = tf // w_g.shape[2]
    in_specs = [pl.BlockSpec((None, tm + 16, d), lambda i, j: (i, 0, 0), pipeline_mode=once),
                pl.BlockSpec((per_tile, d, w_g.shape[2]), lambda i, j: (j, 0, 0)),
                pl.BlockSpec((per_tile, d, w_a.shape[2]), lambda i, j: (j, 0, 0)),
                pl.BlockSpec((8, tf), lambda i, j: (0, j)),
                pl.BlockSpec((1, tf), lambda i, j: (0, j)),
                pl.BlockSpec((tf, d), lambda i, j: (j, 0)),
                pl.BlockSpec((tm, d), lambda i, j: (i, 0), pipeline_mode=once),
                pl.BlockSpec((None, None, 1, d), lambda i, j: (seg(i), gate_idx, 0, 0)),
                pl.BlockSpec((1, d), lambda i, j: (0, 0))]
    args = [hx, w_g, w_a, cw8, cb, w_down, xs, mods, next_gain.reshape(1, d)]
    out_spec = pl.BlockSpec((tm, d), lambda i, j: (i, 0), pipeline_mode=once)
    out_shape = jax.ShapeDtypeStruct((r, d), F32)
    if post == "modulate":
        in_specs += [pl.BlockSpec((None, None, 1, d), lambda i, j: (seg(i), 0, 0, 0)),
                     pl.BlockSpec((None, None, 1, d), lambda i, j: (seg(i), 1, 0, 0))]
        args += [next_mods, next_mods]
        out_spec = [out_spec, pl.BlockSpec((tm, d), lambda i, j: (i, 0), pipeline_mode=once)]
        out_shape = [out_shape, jax.ShapeDtypeStruct((r, d), BF16)]
    return pl.pallas_call(
        functools.partial(_ffn_kernel, tm=tm, n_lat_rows=n_lat_rows, seq=seq, ctx_len=ctx_len, post=post),
        out_shape=out_shape,
        grid=(n_blk, nj),
        in_specs=in_specs,
        out_specs=out_spec,
        compiler_params=_cparams("parallel", "arbitrary"),
        name="conv_ffn",
    )(*args)


def _pad_ffn_weights(w_up, conv_w, conv_b, w_down, layer, tf):
    dff = w_down.shape[1]
    dffp = dff + (-dff % tf)
    bw = _pick(dff, (256, 128))
    w_a = _cast_cols(w_up, layer, 0, dff, dffp, bw)
    w_g = _cast_cols(w_up, layer, dff, dff, dffp, bw)
    cw8 = jnp.pad(conv_w[layer].astype(F32), ((0, 8 - conv_w.shape[1]), (0, dffp - dff)))
    cb = jnp.pad(conv_b[layer].astype(F32), (0, dffp - dff)).reshape(1, dffp)
    w_down_p = _cast_rows(w_down, layer, dffp, bw)
    return w_g, w_a, cw8, cb, w_down_p


def kernel(x, c, ctx, c_ctx, mod_w, mod_b, norm_mix_g, norm_ffn_g, ab_w_in, mlstm_gate_b, mlstm_head_g, s5_a_re, s5_a_im, s5_log_dt, s5_b_re, s5_b_im, s5_c_re, s5_c_im, s5_d, s5_glu_w, s5_glu_b, ab_w_out, na_w_qkv, na_rpb, na_w_out, ffn_w_up, ffn_conv_w, ffn_conv_b, ffn_w_down, final_norm_g):
    n_batch, seq, d = x.shape
    ctx_len = ctx.shape[1]
    depth = mod_w.shape[0]
    assert depth == 2 and mod_w.shape[2] == 6 * d
    n_lat_rows = n_batch * seq
    n_ctx_rows = n_batch * ctx_len
    assert seq % MLSTM_CHUNK == 0 and ctx_len == MLSTM_CHUNK and seq % GRID_W == 0

    n_heads = mlstm_gate_b.shape[-1]
    s5w = s5_d.shape[-1]
    mw = d - s5w
    dv = mw // n_heads
    dqk = dv // 2
    qkw = n_heads * dqk
    n_groups, p_state, s5_group = s5_b_re.shape[1:]
    assert S5_CHUNK * s5_group == 2 * LANE and p_state <= LANE and dqk % (2 * LANE) == 0
    na_heads = na_rpb.shape[1]
    hd = d // na_heads
    rows = seq // GRID_W
    kh = min((na_rpb.shape[2] + 1) // 2, rows)
    assert rows % NA_QROWS == 0 and rows >= NA_KROWS and hd == LANE
    dff = ffn_w_down.shape[1]

    tm_big = _pick(math.gcd(n_lat_rows, n_ctx_rows), (1024, 512, 256))
    tm_ffn = _pick(math.gcd(n_lat_rows, n_ctx_rows), (512, 256))
    tf = 512
    seg_big = _seg_fn(tm_big, n_lat_rows, seq, n_batch)
    seg_ffn = _seg_fn(tm_ffn, n_lat_rows, seq, n_batch)
    seg_256 = _seg_fn(256, n_lat_rows, seq, n_batch)

    cond8 = jnp.concatenate([c, c_ctx[None], jnp.zeros((8 - n_batch - 1, d), F32)], axis=0)
    mods = _ada(cond8, mod_w, mod_b).reshape(depth, 8, 6, 1, d)

    x2d, c2d = x.reshape(n_lat_rows, d), ctx.reshape(n_ctx_rows, d)

    h = _modulate2(x2d, c2d, norm_mix_g[0], mods[0], 0, 1, seg_256, 256)
    w_in = ab_w_in[0]
    n_gate = 4 * n_heads
    w1 = w_in[:, :2 * qkw + 2 * mw].astype(BF16)
    w2 = jnp.concatenate([w_in[:, 2 * qkw + 2 * mw + n_gate:], w_in[:, 2 * qkw + 2 * mw:2 * qkw + 2 * mw + n_gate],
                          jnp.zeros((d, LANE - n_gate), F32)], axis=1).astype(BF16)
    p1 = _matmul(h, w1, BF16, tm_big, _pick(w1.shape[1], (512, 256, 128)), "ab_in_proj")
    p2 = _matmul(h, w2, F32, tm_big, w2.shape[1], "ab_in_proj_s5")
    grow = jnp.pad(p2[:, s5w:s5w + n_gate].T, ((0, -n_gate % 8), (0, 0)))

    cos_t, sin_t = _rope_tables(seq, ctx_len, dqk)
    common = dict(n_batch=n_batch, seq=seq, n_heads=n_heads, dqk=dqk, dv=dv)
    hf = _mlstm_direction(p1, p2, grow, mlstm_gate_b[0], cos_t, sin_t, None, None, reverse=False, **common)
    mix_a = _mlstm_direction(p1, p2, grow, mlstm_gate_b[0], cos_t, sin_t, mlstm_head_g[0], hf,
                             reverse=True, **common)

    n_lat_ch, n_ctx_ch = seq // S5_CHUNK, ctx_len // S5_CHUNK
    row_w = p2.shape[1]
    tables = _s5_tables(s5_a_re[0], s5_a_im[0], s5_log_dt[0], s5_b_re[0], s5_b_im[0], s5_c_re[0], s5_c_im[0])
    y16 = _s5_scan(p2.reshape(-1, S5_CHUNK * row_w), tables, nb=n_batch, n_lat=n_lat_ch, n_ctx=n_ctx_ch,
                   n_groups=n_groups, hsz=s5_group, row_w=row_w)
    y_rows = y16.transpose(1, 0, 2).reshape(-1, s5w)
    mix_b = _glu(y_rows, p2, s5_d[0], s5_glu_w[0], s5_glu_b[0], tm_ffn)

    w_out = ab_w_out[0].astype(BF16)
    xs = _matmul_residual([mix_a, mix_b], [w_out[:mw], w_out[mw:]], (x2d, c2d), mods[0], 2, seg_big,
                          tm_big, 512, "ab_out_proj")
    h = _modulate_halo(xs, norm_ffn_g[0], mods[0], 3, 4, seg_ffn, tm_ffn)
    ffn_kw = dict(tm=tm_ffn, tf=tf, n_lat_rows=n_lat_rows, seq=seq, ctx_len=ctx_len)
    xs, h = _conv_ffn(h, xs, _pad_ffn_weights(ffn_w_up, ffn_conv_w, ffn_conv_b, ffn_w_down, 0, tf),
                      mods[0], 5, seg_ffn, norm_mix_g[1], mods[1], **ffn_kw)

    qkv = _matmul(h, na_w_qkv[0].astype(BF16), BF16, tm_big, 512, "na_qkv_proj", unit=hd)
    t2, rmask = _na_tables(na_rpb[0], rows)
    o = _na_attention(qkv, t2, rmask, n_batch=n_batch, seq=seq, ctx_len=ctx_len, n_heads=na_heads, hd=hd,
                      kh=kh, win_h=(na_rpb.shape[2] + 1) // 2)
    xl = _matmul_residual([o], [na_w_out[0].astype(BF16)], xs, mods[1], 2, seg_big, tm_big, 512,
                          "na_out_proj")
    h = _modulate_halo(xl, norm_ffn_g[1], mods[1], 3, 4, seg_ffn, tm_ffn)
    out = _conv_ffn(h, xl, _pad_ffn_weights(ffn_w_up, ffn_conv_w, ffn_conv_b, ffn_w_down, 1, tf),
                    mods[1], 5, seg_ffn, final_norm_g, **ffn_kw)
    return out.reshape(n_batch, seq, d)
```

```python
import functools
import math

import jax
import jax.numpy as jnp
import numpy as np
from jax import lax
from jax.experimental import pallas as pl
from jax.experimental.pallas import tpu as pltpu

F32 = jnp.float32
BF16 = jnp.bfloat16

EPS = 1e-6
ROPE_BASE = 10000.0
GRID_W = 64
MLSTM_CHUNK = 256
S5_CHUNK = 16
NA_QROWS = 4
NA_KROWS = 12
NA_UNROLL = 8
LOG2E = 1.4426950408889634
LANE = 128
VMEM_LIMIT_BYTES = 58 * 1024 * 1024


def _cparams(*sem):
    return pltpu.CompilerParams(dimension_semantics=sem, vmem_limit_bytes=VMEM_LIMIT_BYTES)


def _pick(n, candidates):
    for c in candidates:
        if n % c == 0:
            return c
    raise ValueError(f"no tile in {candidates} divides {n}")


def _ada_kernel(c_ref, w_ref, b_ref, o_ref):
    s = jax.nn.silu(c_ref[...]).astype(BF16)
    o_ref[...] = jnp.dot(s, w_ref[...].astype(BF16), preferred_element_type=F32) + b_ref[...]


def _ada(cond8, mod_w, mod_b):
    depth, d, n6 = mod_w.shape
    tn = _pick(n6, (512, 256, 128))
    return pl.pallas_call(
        _ada_kernel,
        out_shape=jax.ShapeDtypeStruct((depth, 8, n6), F32),
        grid=(depth, n6 // tn),
        in_specs=[pl.BlockSpec((8, d), lambda l, j: (0, 0)),
                  pl.BlockSpec((None, d, tn), lambda l, j: (l, 0, j)),
                  pl.BlockSpec((None, 1, tn), lambda l, j: (l, 0, j))],
        out_specs=pl.BlockSpec((None, 8, tn), lambda l, j: (l, 0, j)),
        compiler_params=_cparams("parallel", "parallel"),
        name="ada_mod",
    )(cond8, mod_w, mod_b.reshape(depth, 1, n6))


def _seg_fn(tm, n_lat_rows, seq, n_batch):
    n_lat_blk = n_lat_rows // tm

    def seg(i):
        return jnp.where(i < n_lat_blk, (i * tm) // seq, n_batch)
    return seg


def _two_source_specs(block, n_first_blk, col=None):
    if col is None:
        return (pl.BlockSpec(block, lambda i: (jnp.minimum(i, n_first_blk - 1), 0)),
                pl.BlockSpec(block, lambda i: (jnp.maximum(i - n_first_blk, 0), 0)))
    return (pl.BlockSpec(block, lambda i, j: (jnp.minimum(i, n_first_blk - 1), jnp.where(i < n_first_blk, j, 0))),
            pl.BlockSpec(block, lambda i, j: (jnp.maximum(i - n_first_blk, 0), jnp.where(i < n_first_blk, 0, j))))


def _modulate2_kernel(x_ref, c_ref, g_ref, sh_ref, sc_ref, o_ref, *, n_first_blk):
    x = jnp.where(pl.program_id(0) < n_first_blk, x_ref[...], c_ref[...])
    y = x * lax.rsqrt(jnp.mean(x * x, axis=-1, keepdims=True) + EPS) * g_ref[...]
    o_ref[...] = (y * (1.0 + sc_ref[...]) + sh_ref[...]).astype(o_ref.dtype)


def _modulate2(x2d, c2d, gain, mods, shift_idx, scale_idx, seg, tm):
    d = x2d.shape[1]
    r = x2d.shape[0] + c2d.shape[0]
    n_first_blk = x2d.shape[0] // tm
    return pl.pallas_call(
        functools.partial(_modulate2_kernel, n_first_blk=n_first_blk),
        out_shape=jax.ShapeDtypeStruct((r, d), BF16),
        grid=(r // tm,),
        in_specs=[*_two_source_specs((tm, d), n_first_blk),
                  pl.BlockSpec((1, d), lambda i: (0, 0)),
                  pl.BlockSpec((None, None, 1, d), lambda i: (seg(i), shift_idx, 0, 0)),
                  pl.BlockSpec((None, None, 1, d), lambda i: (seg(i), scale_idx, 0, 0))],
        out_specs=pl.BlockSpec((tm, d), lambda i: (i, 0)),
        compiler_params=_cparams("parallel"),
        name="modulate",
    )(x2d, c2d, gain.reshape(1, d), mods, mods)


def _modulate_halo_kernel(x_ref, xp_ref, xn_ref, g_ref, sh_ref, sc_ref, o_ref):
    x = jnp.concatenate([x_ref[...], xp_ref[...], xn_ref[...]], axis=0)
    y = x * lax.rsqrt(jnp.mean(x * x, axis=-1, keepdims=True) + EPS) * g_ref[...]
    o_ref[...] = (y * (1.0 + sc_ref[...]) + sh_ref[...]).astype(o_ref.dtype)


def _modulate_halo(xs, gain, mods, shift_idx, scale_idx, seg, tm):
    r, d = xs.shape
    n_blk, per8 = r // tm, tm // 8
    return pl.pallas_call(
        _modulate_halo_kernel,
        out_shape=jax.ShapeDtypeStruct((n_blk, tm + 16, d), BF16),
        grid=(n_blk,),
        in_specs=[pl.BlockSpec((tm, d), lambda i: (i, 0)),
                  pl.BlockSpec((8, d), lambda i: (jnp.maximum(i * per8 - 1, 0), 0)),
                  pl.BlockSpec((8, d), lambda i: (jnp.minimum((i + 1) * per8, r // 8 - 1), 0)),
                  pl.BlockSpec((1, d), lambda i: (0, 0)),
                  pl.BlockSpec((None, None, 1, d), lambda i: (seg(i), shift_idx, 0, 0)),
                  pl.BlockSpec((None, None, 1, d), lambda i: (seg(i), scale_idx, 0, 0))],
        out_specs=pl.BlockSpec((None, tm + 16, d), lambda i: (i, 0, 0)),
        compiler_params=_cparams("parallel"),
        name="modulate_halo",
    )(xs, xs, xs, gain.reshape(1, d), mods, mods)


def _cast_kernel(x_ref, o_ref, *, n_valid):
    keep = pl.program_id(0) < n_valid
    o_ref[...] = jnp.where(keep, x_ref[...], 0.0).astype(o_ref.dtype)


def _cast_cols(w, layer, col0, n_cols, n_cols_pad, bw):
    d = w.shape[1]
    nv, off = n_cols // bw, col0 // bw
    assert n_cols % bw == 0 and col0 % bw == 0 and n_cols_pad % bw == 0
    return pl.pallas_call(
        functools.partial(_cast_kernel, n_valid=nv),
        out_shape=jax.ShapeDtypeStruct((n_cols_pad // bw, d, bw), BF16),
        grid=(n_cols_pad // bw,),
        in_specs=[pl.BlockSpec((None, d, bw), lambda i: (layer, 0, off + jnp.minimum(i, nv - 1)))],
        out_specs=pl.BlockSpec((None, d, bw), lambda i: (i, 0, 0)),
        compiler_params=_cparams("parallel"),
        name="cast_cols",
    )(w)


def _cast_rows(w, layer, n_rows_pad, br):
    n_rows, d = w.shape[1:]
    nv = n_rows // br
    assert n_rows % br == 0 and n_rows_pad % br == 0
    return pl.pallas_call(
        functools.partial(_cast_kernel, n_valid=nv),
        out_shape=jax.ShapeDtypeStruct((n_rows_pad, d), BF16),
        grid=(n_rows_pad // br,),
        in_specs=[pl.BlockSpec((None, br, d), lambda i: (layer, jnp.minimum(i, nv - 1), 0))],
        out_specs=pl.BlockSpec((br, d), lambda i: (i, 0)),
        compiler_params=_cparams("parallel"),
        name="cast_rows",
    )(w)


def _mm_kernel(a_ref, b_ref, o_ref):
    res = jnp.dot(a_ref[...], b_ref[...], preferred_element_type=F32).astype(o_ref.dtype)
    if len(o_ref.shape) == 3:
        w = o_ref.shape[2]
        for t in range(o_ref.shape[0]):
            o_ref[t] = res[:, t * w:(t + 1) * w]
    else:
        o_ref[...] = res


def _matmul(a, b, out_dtype, tm, tn, name, unit=None):
    m, k = a.shape
    n = b.shape[1]
    if unit is None:
        out_shape = jax.ShapeDtypeStruct((m, n), out_dtype)
        out_spec = pl.BlockSpec((tm, tn), lambda i, j: (i, j))
    else:
        out_shape = jax.ShapeDtypeStruct((n // unit, m, unit), out_dtype)
        out_spec = pl.BlockSpec((tn // unit, tm, unit), lambda i, j: (j, i, 0))
    return pl.pallas_call(
        _mm_kernel,
        out_shape=out_shape,
        grid=(m // tm, n // tn),
        in_specs=[pl.BlockSpec((tm, k), lambda i, j: (i, 0)),
                  pl.BlockSpec((k, tn), lambda i, j: (0, j))],
        out_specs=out_spec,
        compiler_params=_cparams("parallel", "parallel"),
        name=name,
    )(a, b)


def _rows_by_unit(ref):
    if len(ref.shape) == 2:
        return ref[...]
    return jnp.concatenate([ref[t] for t in range(ref.shape[0])], axis=1)


def _mm_res_kernel(*refs, n_a, n_first_blk):
    a_refs, b_refs = refs[:n_a], refs[n_a:2 * n_a]
    if n_first_blk is None:
        x_ref, gate_ref, o_ref = refs[2 * n_a:]
        x = x_ref[...]
    else:
        x_ref, c_ref, gate_ref, o_ref = refs[2 * n_a:]
        x = jnp.where(pl.program_id(0) < n_first_blk, x_ref[...], c_ref[...])
    acc = jnp.dot(_rows_by_unit(a_refs[0]), b_refs[0][...], preferred_element_type=F32)
    for a_ref, b_ref in zip(a_refs[1:], b_refs[1:]):
        acc = acc + jnp.dot(_rows_by_unit(a_ref), b_ref[...], preferred_element_type=F32)
    o_ref[...] = x + gate_ref[...] * acc


def _matmul_residual(a_list, b_list, xs, mods, gate_idx, seg, tm, tn, name):
    m = a_list[0].shape[-2]
    n = b_list[0].shape[1]

    def a_spec(a):
        if a.ndim == 2:
            return pl.BlockSpec((tm, a.shape[1]), lambda i, j: (i, 0))
        return pl.BlockSpec((a.shape[0], tm, a.shape[2]), lambda i, j: (0, i, 0))

    if isinstance(xs, tuple):
        n_first_blk = xs[0].shape[0] // tm
        x_specs = list(_two_source_specs((tm, tn), n_first_blk, col=True))
        x_args = list(xs)
    else:
        n_first_blk = None
        x_specs = [pl.BlockSpec((tm, tn), lambda i, j: (i, j))]
        x_args = [xs]
    in_specs = ([a_spec(a) for a in a_list]
                + [pl.BlockSpec((b.shape[0], tn), lambda i, j: (0, j)) for b in b_list]
                + x_specs
                + [pl.BlockSpec((None, None, 1, tn), lambda i, j: (seg(i), gate_idx, 0, j))])
    return pl.pallas_call(
        functools.partial(_mm_res_kernel, n_a=len(a_list), n_first_blk=n_first_blk),
        out_shape=jax.ShapeDtypeStruct((m, n), F32),
        grid=(m // tm, n // tn),
        in_specs=in_specs,
        out_specs=pl.BlockSpec((tm, tn), lambda i, j: (i, j)),
        compiler_params=_cparams("parallel", "parallel"),
        name=name,
    )(*a_list, *b_list, *x_args, mods)


def _rope(x, cos, sin):
    parts = []
    for j in range(x.shape[1] // LANE):
        sl = slice(j * LANE, (j + 1) * LANE)
        xs = x[:, sl]
        parts.append(xs * cos[:, sl] + pltpu.roll(xs, LANE // 2, axis=1) * sin[:, sl])
    return jnp.concatenate(parts, axis=1)


def _mlstm_kernel(gb_ref, q_ref, k_ref, v_ref, cos_ref, sin_ref, gcol_ref, grow_ref, *rest,
                  reverse, n_heads, scale):
    if reverse:
        o_ref, hf_ref, hg_ref, out_ref, ct_ref, m_ref = rest
    else:
        out_ref, ct_ref, m_ref = rest
    step = pl.program_id(1)
    d = 1 if reverse else 0
    L = q_ref.shape[0]
    dqk = q_ref.shape[1] // n_heads
    dv = v_ref.shape[1] // n_heads

    @pl.when(step == 0)
    def _():
        ct_ref[...] = jnp.zeros_like(ct_ref)
        m_ref[...] = jnp.zeros_like(m_ref)

    row = lax.broadcasted_iota(jnp.int32, (L, L), 0)
    col = lax.broadcasted_iota(jnp.int32, (L, L), 1)
    incl = (col >= row) if reverse else (col <= row)
    incl_t = (row >= col) if reverse else (row <= col)
    n_lat = pl.num_programs(1) - 1
    tab = jnp.where(step == 0, n_lat, (n_lat - step) if reverse else (step - 1))
    t0 = pl.multiple_of(tab * L, L)
    cos = cos_ref[pl.ds(t0, L), :]
    sin = sin_ref[pl.ds(t0, L), :]
    ones = jnp.ones((L, LANE), v_ref.dtype)

    for hh in range(n_heads):
        gi = (d * 2 + 0) * n_heads + hh
        gf = (d * 2 + 1) * n_heads + hh
        b_i = gb_ref[gi]
        b_f = gb_ref[gf]
        li_col = gcol_ref[:, gi:gi + 1] + b_i
        lf_col = jax.nn.log_sigmoid(gcol_ref[:, gf:gf + 1] + b_f)
        li_row = grow_ref[gi:gi + 1, :] + b_i
        lf_row = jax.nn.log_sigmoid(grow_ref[gf:gf + 1, :] + b_f)
        cum_col = jnp.sum(jnp.where(incl, lf_row, 0.0), axis=1, keepdims=True)
        cum_row = jnp.sum(jnp.where(incl_t, lf_col, 0.0), axis=0, keepdims=True)
        total = jnp.sum(lf_row, axis=1, keepdims=True)
        m_prev = m_ref[hh, 0:1, 0:1]

        dmat = jnp.where(incl, cum_col - cum_row + li_row, -jnp.inf)
        carried = cum_col + m_prev
        m_loc = jnp.maximum(carried, jnp.max(dmat, axis=1, keepdims=True))
        w = jnp.exp(dmat - m_loc)
        w_state = jnp.exp(carried - m_loc)

        qs = slice(hh * dqk, (hh + 1) * dqk)
        vs = slice(hh * dv, (hh + 1) * dv)
        qb = (_rope(q_ref[:, qs].astype(F32), cos, sin) * scale).astype(BF16)
        kb = _rope(k_ref[:, qs].astype(F32), cos, sin).astype(BF16)
        s_qk = lax.dot_general(qb, kb, (((1,), (1,)), ((), ())), preferred_element_type=F32)
        sc = (s_qk * w).astype(BF16)
        vext = jnp.concatenate([v_ref[:, vs], ones], axis=1)
        ct = ct_ref[hh]
        res = (jnp.dot(sc, vext, preferred_element_type=F32)
               + w_state * jnp.dot(qb, ct.astype(BF16), preferred_element_type=F32))
        den = res[:, dv:dv + 1]
        hval = res[:, :dv] / jnp.maximum(jnp.abs(den), jnp.exp(-m_loc))

        src = total - cum_col + li_col
        m_new = jnp.maximum(total + m_prev, jnp.max(src, axis=0, keepdims=True))
        decay = jnp.exp(total + m_prev - m_new)
        w_src = jnp.exp(src - m_new)
        upd = lax.dot_general(kb, (w_src * vext.astype(F32)).astype(BF16), (((0,), (0,)), ((), ())),
                              preferred_element_type=F32)
        ct_ref[hh] = decay * ct + upd
        m_ref[hh] = jnp.broadcast_to(m_new, m_ref.shape[1:])

        if reverse:
            hm = hf_ref[:, vs] + hval
            hn = hm * lax.rsqrt(jnp.mean(hm * hm, axis=-1, keepdims=True) + EPS) * hg_ref[:, vs]
            out_ref[:, vs] = (jax.nn.sigmoid(o_ref[:, vs].astype(F32)) * hn).astype(out_ref.dtype)
        else:
            out_ref[:, vs] = hval


def _mlstm_direction(p1, p2, grow, gate_b, cos_t, sin_t, head_g, hf, *, reverse, n_batch, seq,
                     n_heads, dqk, dv):
    r = p1.shape[0]
    L = MLSTM_CHUNK
    n_lat = seq // L
    lat_blocks = n_batch * n_lat
    qkw, mw = n_heads * dqk, n_heads * dv
    gate_tile = p2.shape[1] // LANE - 1

    def blk(b, s):
        lat = b * n_lat + ((n_lat - s) if reverse else (s - 1))
        return jnp.where(s == 0, lat_blocks + b, lat)

    once = pl.Buffered(1)
    in_specs = [
        pl.BlockSpec(memory_space=pltpu.SMEM),
        pl.BlockSpec((L, qkw), lambda b, s: (blk(b, s), 0)),
        pl.BlockSpec((L, qkw), lambda b, s: (blk(b, s), 1)),
        pl.BlockSpec((L, mw), lambda b, s: (blk(b, s), (2 * qkw) // mw)),
        pl.BlockSpec(cos_t.shape, lambda b, s: (0, 0), pipeline_mode=once),
        pl.BlockSpec(sin_t.shape, lambda b, s: (0, 0), pipeline_mode=once),
        pl.BlockSpec((L, LANE), lambda b, s: (blk(b, s), gate_tile)),
        pl.BlockSpec((grow.shape[0], L), lambda b, s: (0, blk(b, s))),
    ]
    args = [gate_b.reshape(-1), p1, p1, p1, cos_t, sin_t, p2, grow]
    if reverse:
        in_specs += [
            pl.BlockSpec((L, mw), lambda b, s: (blk(b, s), (2 * qkw) // mw + 1)),
            pl.BlockSpec((L, mw), lambda b, s: (blk(b, s), 0)),
            pl.BlockSpec((1, mw), lambda b, s: (0, 0)),
        ]
        args += [p1, hf, head_g.reshape(1, mw)]
    return pl.pallas_call(
        functools.partial(_mlstm_kernel, reverse=reverse, n_heads=n_heads, scale=dqk ** -0.5),
        out_shape=jax.ShapeDtypeStruct((r, mw), BF16 if reverse else F32),
        grid=(n_batch, n_lat + 1),
        in_specs=in_specs,
        out_specs=pl.BlockSpec((L, mw), lambda b, s: (blk(b, s), 0)),
        scratch_shapes=[pltpu.VMEM((n_heads, dqk, dv + LANE), F32), pltpu.VMEM((n_heads, 8, LANE), F32)],
        compiler_params=_cparams("parallel", "arbitrary"),
        name="mlstm_bwd" if reverse else "mlstm_fwd",
    )(*args)


def _rope_tables(seq, ctx_len, dqk):
    half = dqk // 2
    inv = ROPE_BASE ** (-jnp.arange(0, half, 2, dtype=F32) / half)
    t = jnp.arange(seq)

    def one(pos):
        ang = pos.astype(F32)[:, None] * inv
        c, s = jnp.cos(ang), jnp.sin(ang)
        return jnp.concatenate([c, c], axis=-1), jnp.concatenate([-s, s], axis=-1)

    c_r, s_r = one(t // GRID_W)
    c_c, s_c = one(t % GRID_W)
    cos_t = jnp.concatenate([c_r, c_c], axis=-1)
    sin_t = jnp.concatenate([s_r, s_c], axis=-1)
    cos_t = jnp.concatenate([cos_t, jnp.ones((ctx_len, dqk), F32)], axis=0)
    sin_t = jnp.concatenate([sin_t, jnp.zeros((ctx_len, dqk), F32)], axis=0)
    return cos_t, sin_t


def _s5_tables(a_re, a_im, log_dt, b_re, b_im, c_re, c_im):
    g_, p_, hsz = b_re.shape
    lc = S5_CHUNK
    bmat = lax.complex(b_re.astype(F32), b_im.astype(F32))
    cmat = lax.complex(c_re.astype(F32), c_im.astype(F32))
    tau = jnp.arange(lc + 1, dtype=F32)
    s_idx = jnp.arange(lc)
    m_tot = 0.0
    w_parts, v_parts, a_parts = [], [], []
    for d in range(2):
        lam = lax.complex(a_re[d].astype(F32), a_im[d].astype(F32))
        dt = jnp.exp(log_dt[d].astype(F32))[:, None]
        fac = (jnp.exp(lam * dt) - 1) / lam
        apow = jnp.exp(tau[:, None, None] * (lam * dt)[None])
        bf = bmat * fac[:, :, None]
        cb = cmat.transpose(0, 2, 1)[:, :, :, None] * bf[:, :, None, :]
        kt = jnp.real(jnp.einsum('tgp,gpij->tgij', apow[:lc], cb,
                                 precision=lax.Precision.HIGHEST))
        lag = (s_idx[None, :] - s_idx[:, None]) if d == 0 else (s_idx[:, None] - s_idx[None, :])
        ok = lag >= 0
        kk = kt[jnp.clip(lag, 0, lc - 1)]
        kk = jnp.where(ok[:, :, None, None, None], kk, 0.0)
        m_tot = m_tot + kk.transpose(2, 0, 4, 1, 3).reshape(g_, lc * hsz, lc * hsz)
        pw = apow[(lc - 1 - s_idx) if d == 0 else s_idx]
        wc = pw[:, :, :, None] * bf[None]
        wc = wc.transpose(1, 0, 3, 2).reshape(g_, lc * hsz, p_)
        pv = apow[(s_idx + 1) if d == 0 else (lc - s_idx)]
        vc = cmat[None] * pv[:, :, None, :]
        vc = vc.transpose(1, 3, 0, 2).reshape(g_, p_, lc * hsz)
        pad_w = jnp.zeros((g_, lc * hsz, LANE - p_), F32)
        pad_v = jnp.zeros((g_, LANE - p_, lc * hsz), F32)
        w_parts += [jnp.real(wc), pad_w, jnp.imag(wc), pad_w]
        v_parts += [jnp.real(vc), pad_v, -jnp.imag(vc), pad_v]
        a_chunk = apow[lc]
        pad_a = jnp.zeros((g_, LANE - p_), F32)
        a_parts += [jnp.concatenate([jnp.real(a_chunk), pad_a], axis=1),
                    jnp.concatenate([jnp.imag(a_chunk), pad_a], axis=1)]
    w_all = jnp.concatenate(w_parts, axis=2)
    v_all = jnp.concatenate(v_parts, axis=1)
    a_all = jnp.stack(a_parts + [jnp.zeros_like(a_parts[0])] * 4, axis=1)
    return m_tot.astype(BF16), w_all.astype(BF16), v_all.astype(BF16), a_all


def _s5_kernel(*refs, nb, n_lat, n_ctx, hsz):
    lc = S5_CHUNK
    x_refs = refs[:lc]
    m_ref, w_ref, v_ref, a_ref, o_ref = refs[lc:lc + 5]
    z_refs = refs[lc + 5:lc + 9]
    sp_refs = refs[lc + 9:lc + 13]
    k = pl.program_id(0) % (LANE // hsz)
    cw = lc * hsz
    ri = lax.broadcasted_iota(jnp.int32, (lc * LANE, cw), 0)
    ci = lax.broadcasted_iota(jnp.int32, (lc * LANE, cw), 1)
    sel = jnp.where(ri == (ci // hsz) * LANE + k * hsz + ci % hsz, 1.0, 0.0).astype(BF16)
    ri_t = lax.broadcasted_iota(jnp.int32, (cw, lc * LANE), 0)
    ci_t = lax.broadcasted_iota(jnp.int32, (cw, lc * LANE), 1)
    sel_t = jnp.where(ci_t == (ri_t // hsz) * LANE + k * hsz + ri_t % hsz, 1.0, 0.0).astype(BF16)

    x = jnp.concatenate([r[...] for r in x_refs], axis=1).astype(BF16)
    ub = jnp.dot(x, sel, preferred_element_type=F32).astype(BF16)
    z = jnp.dot(ub, w_ref[...], preferred_element_type=F32)
    for q in range(4):
        z_refs[q][...] = z[:, q * LANE:(q + 1) * LANE]
    n_chunks = n_lat + n_ctx

    def rows_of(chunk):
        if chunk < n_lat:
            return pl.ds(chunk, nb, stride=n_lat)
        return pl.ds(nb * n_lat + chunk - n_lat, nb, stride=n_ctx)

    a = a_ref[...]
    coef = [(a[2 * d:2 * d + 1, :], a[2 * d + 1:2 * d + 2, :]) for d in range(2)]
    state = [(jnp.zeros((nb, LANE), F32), jnp.zeros((nb, LANE), F32)) for _ in range(2)]
    for t in range(n_chunks):
        chunk = ((t + n_lat) % n_chunks, n_chunks - 1 - t)
        for d in range(2):
            rs = rows_of(chunk[d])
            a_r, a_i = coef[d]
            s_re, s_im = state[d]
            sp_refs[2 * d][rs, :] = s_re
            sp_refs[2 * d + 1][rs, :] = s_im
            z_re = z_refs[2 * d][rs, :]
            z_im = z_refs[2 * d + 1][rs, :]
            state[d] = (a_r * s_re - a_i * s_im + z_re, a_r * s_im + a_i * s_re + z_im)
    sp = jnp.concatenate([r[...] for r in sp_refs], axis=1)
    y = (jnp.dot(ub, m_ref[...], preferred_element_type=F32)
         + jnp.dot(sp.astype(BF16), v_ref[...], preferred_element_type=F32))
    y_hi = y.astype(BF16)
    y_lo = (y - y_hi.astype(F32)).astype(BF16)
    back = (jnp.dot(y_hi, sel_t, preferred_element_type=F32)
            + jnp.dot(y_lo, sel_t, preferred_element_type=F32))

    @pl.when(k == 0)
    def _():
        o_ref[...] = jnp.zeros_like(o_ref)

    for s in range(lc):
        o_ref[s] += back[:, s * LANE:(s + 1) * LANE]


def _s5_scan(p2v, tables, *, nb, n_lat, n_ctx, n_groups, hsz, row_w):
    m_tot, w_all, v_all, a_all = tables
    rows = p2v.shape[0]
    lc = S5_CHUNK
    cw = lc * hsz
    gpt = LANE // hsz
    tiles_per_tok = row_w // LANE

    def x_spec(s):
        return pl.BlockSpec((rows, LANE), lambda g: (0, s * tiles_per_tok + g // gpt))

    return pl.pallas_call(
        functools.partial(_s5_kernel, nb=nb, n_lat=n_lat, n_ctx=n_ctx, hsz=hsz),
        out_shape=jax.ShapeDtypeStruct((lc, rows, n_groups * hsz), F32),
        grid=(n_groups,),
        in_specs=[x_spec(s) for s in range(lc)] + [
            pl.BlockSpec((None, cw, cw), lambda g: (g, 0, 0)),
            pl.BlockSpec((None, cw, 4 * LANE), lambda g: (g, 0, 0)),
            pl.BlockSpec((None, 4 * LANE, cw), lambda g: (g, 0, 0)),
            pl.BlockSpec((None, 8, LANE), lambda g: (g, 0, 0))],
        out_specs=pl.BlockSpec((lc, rows, LANE), lambda g: (0, 0, g // gpt)),
        scratch_shapes=[pltpu.VMEM((rows, LANE), F32)] * 8,
        compiler_params=_cparams("arbitrary"),
        name="s5_scan",
    )(*([p2v] * lc), m_tot, w_all, v_all, a_all)


def _glu_kernel(y_ref, p_ref, d_ref, w_ref, b_ref, o_ref):
    width = y_ref.shape[1]
    g = jax.nn.gelu(y_ref[...] + d_ref[...] * p_ref[:, :width])
    z = jnp.dot(g.astype(BF16), w_ref[...], preferred_element_type=F32) + b_ref[...]
    o_ref[...] = (g * jax.nn.sigmoid(z)).astype(o_ref.dtype)


def _glu(y, p2, d_skip, glu_w, glu_b, tm):
    r, w = y.shape
    return pl.pallas_call(
        _glu_kernel,
        out_shape=jax.ShapeDtypeStruct((r, w), BF16),
        grid=(r // tm,),
        in_specs=[pl.BlockSpec((tm, w), lambda i: (i, 0)),
                  pl.BlockSpec((tm, p2.shape[1]), lambda i: (i, 0)),
                  pl.BlockSpec((1, w), lambda i: (0, 0)),
                  pl.BlockSpec((w, w), lambda i: (0, 0)),
                  pl.BlockSpec((1, w), lambda i: (0, 0))],
        out_specs=pl.BlockSpec((tm, w), lambda i: (i, 0)),
        compiler_params=_cparams("parallel"),
        name="s5_glu",
    )(y, p2, d_skip.astype(F32).reshape(1, w), glu_w.astype(BF16), glu_b.astype(F32).reshape(1, w))


def _na_variants(rows, kh):
    n_blk = rows // NA_QROWS
    out = []
    for i in (0, min(1, n_blk - 1), n_blk - 1):
        out.append((i, min(max(i * NA_QROWS - kh // 2, 0), rows - NA_KROWS)))
    return out


def _na_tables(rpb, rows):
    n_heads, nri, nci = rpb.shape
    win_h, win_w = (nri + 1) // 2, (nci + 1) // 2
    kh = min(win_h, rows)
    col = np.arange(GRID_W)
    col_start = np.clip(col - win_w // 2, 0, GRID_W - win_w)
    col_ok = (col[None, :] >= col_start[:, None]) & (col[None, :] < col_start[:, None] + win_w)
    ci = np.clip(col[None, :] - col[:, None], -(win_w - 1), win_w - 1) + (win_w - 1)
    onehot = (ci[None] == np.arange(nci)[:, None, None]).astype(np.float32)
    t1 = jnp.einsum('hrc,cqk->hrqk', rpb.astype(F32), onehot, precision=lax.Precision.HIGHEST)
    t1 = jnp.where(col_ok[None, None], t1, -jnp.inf)
    dummy = jnp.full((n_heads, 1, GRID_W, GRID_W), -jnp.inf, F32)
    t1 = jnp.concatenate([dummy, t1, dummy], axis=1)
    t2 = jnp.concatenate([t1[:, :nri + 1], t1[:, 1:nri + 2]], axis=-1)
    rmask = np.zeros((3, NA_QROWS, NA_KROWS // 2, 1, 2 * GRID_W), np.float32)
    for v, (i, ks) in enumerate(_na_variants(rows, kh)):
        for a in range(NA_QROWS):
            r_start = min(max(i * NA_QROWS + a - kh // 2, 0), rows - kh)
            for b in range(NA_KROWS):
                if not (r_start <= ks + b < r_start + kh):
                    rmask[v, a, b // 2, 0, (b % 2) * GRID_W:(b % 2 + 1) * GRID_W] = -np.inf
    return t2, jnp.asarray(rmask)


def _na_kernel(q_ref, k_ref, v_ref, kc_ref, vc_ref, t2_ref, rm_ref, o_ref, bias_ref, *, rows, kh, win_h,
               scale):
    n_blk = rows // NA_QROWS
    tq = NA_QROWS * GRID_W
    tk = NA_KROWS * GRID_W
    n_r = t2_ref.shape[0]

    @pl.when(pl.program_id(1) == 0)
    def _():
        for v, (i, ks) in enumerate(_na_variants(rows, kh)):
            for a in range(NA_QROWS):
                for pair in range(NA_KROWS // 2):
                    r = (ks + 2 * pair) - (i * NA_QROWS + a) + (win_h - 1)
                    tile = t2_ref[min(max(r + 1, 0), n_r - 1)] + rm_ref[v, a, pair]
                    bias_ref[v, a * GRID_W:(a + 1) * GRID_W,
                             pair * 2 * GRID_W:(pair + 1) * 2 * GRID_W] = tile * LOG2E

    kc = kc_ref[...]
    vc = vc_ref[...]
    nt = (((1,), (1,)), ((), ()))

    def body(i, carry):
        q0 = pl.multiple_of(i * tq, tq)
        ks = jnp.clip(i * NA_QROWS - kh // 2, 0, rows - NA_KROWS)
        k0 = pl.multiple_of(ks * GRID_W, 4 * GRID_W)
        variant = jnp.where(i == 0, 0, jnp.where(i == n_blk - 1, 2, 1))
        qb = q_ref[pl.ds(q0, tq), :]
        kb = k_ref[pl.ds(k0, tk), :]
        vb = v_ref[pl.ds(k0, tk), :]
        s_loc = lax.dot_general(qb, kb, nt, preferred_element_type=F32) * (scale * LOG2E) + bias_ref[variant]
        s_ctx = lax.dot_general(qb, kc, nt, preferred_element_type=F32) * (scale * LOG2E)
        m = jnp.maximum(jnp.max(s_loc, axis=1, keepdims=True), jnp.max(s_ctx, axis=1, keepdims=True))
        p_loc = jnp.exp2(s_loc - m)
        p_ctx = jnp.exp2(s_ctx - m)
        denom = jnp.sum(p_loc, axis=1, keepdims=True) + jnp.sum(p_ctx, axis=1, keepdims=True)
        acc = (jnp.dot(p_loc.astype(BF16), vb, preferred_element_type=F32)
               + jnp.dot(p_ctx.astype(BF16), vc, preferred_element_type=F32))
        o_ref[pl.ds(q0, tq), :] = (acc / denom).astype(o_ref.dtype)
        return carry

    lax.fori_loop(0, n_blk, body, 0, unroll=NA_UNROLL)


def _na_attention(qkv, t2, rmask, *, n_batch, seq, ctx_len, n_heads, hd, kh, win_h):
    rows = seq // GRID_W
    lat_ctx_blk = (n_batch * seq) // ctx_len
    tq, tk = NA_QROWS * GRID_W, NA_KROWS * GRID_W
    return pl.pallas_call(
        functools.partial(_na_kernel, rows=rows, kh=kh, win_h=win_h, scale=hd ** -0.5),
        out_shape=jax.ShapeDtypeStruct((n_heads, n_batch * seq, hd), BF16),
        grid=(n_heads, n_batch),
        in_specs=[pl.BlockSpec((None, seq, hd), lambda h, b: (h, b, 0)),
                  pl.BlockSpec((None, seq, hd), lambda h, b: (n_heads + h, b, 0)),
                  pl.BlockSpec((None, seq, hd), lambda h, b: (2 * n_heads + h, b, 0)),
                  pl.BlockSpec((None, ctx_len, hd), lambda h, b: (n_heads + h, lat_ctx_blk + b, 0)),
                  pl.BlockSpec((None, ctx_len, hd), lambda h, b: (2 * n_heads + h, lat_ctx_blk + b, 0)),
                  pl.BlockSpec((None,) + t2.shape[1:], lambda h, b: (h, 0, 0, 0)),
                  pl.BlockSpec(rmask.shape, lambda h, b: (0, 0, 0, 0, 0))],
        out_specs=pl.BlockSpec((None, seq, hd), lambda h, b: (h, b, 0)),
        scratch_shapes=[pltpu.VMEM((3, tq, tk), F32)],
        compiler_params=_cparams("arbitrary", "arbitrary"),
        name="na_attention",
    )(qkv, qkv, qkv, qkv, qkv, t2, rmask)


def _ffn_kernel(hx_ref, wg_ref, wa_ref, cw_ref, cb_ref, wd_ref, x_ref, gate_ref, ng_ref, *rest,
                tm, n_lat_rows, seq, ctx_len, post):
    if post == "modulate":
        nsh_ref, nsc_ref, o_ref, h_ref = rest
    else:
        (o_ref,) = rest
    i = pl.program_id(0)
    j = pl.program_id(1)

    @pl.when(j == 0)
    def _():
        o_ref[...] = jnp.zeros_like(o_ref)

    g_all = jnp.dot(hx_ref[...], _rows_by_unit(wg_ref), preferred_element_type=F32)
    a = jnp.dot(hx_ref[:tm, :], _rows_by_unit(wa_ref), preferred_element_type=F32)
    g = g_all[:tm]
    local = lax.broadcasted_iota(jnp.int32, (tm, 1), 0)
    grow = i * tm + local
    in_lat = grow < n_lat_rows
    pos = jnp.where(in_lat, grow % seq, (grow - n_lat_rows) % ctx_len)
    last = jnp.where(in_lat, seq - 1, ctx_len - 1)
    g_prev = jnp.where(local == 0, g_all[tm + 7:tm + 8, :], pltpu.roll(g, 1, axis=0))
    g_prev = jnp.where(pos == 0, 0.0, g_prev)
    g_next = jnp.where(local == tm - 1, g_all[tm + 8:tm + 9, :], pltpu.roll(g, tm - 1, axis=0))
    g_next = jnp.where(pos == last, 0.0, g_next)
    cw = cw_ref[...]
    gl = jax.nn.gelu(cw[0:1, :] * g_prev + cw[1:2, :] * g + cw[2:3, :] * g_next + cb_ref[...])
    o_ref[...] += jnp.dot((gl * a).astype(BF16), wd_ref[...], preferred_element_type=F32)

    @pl.when(j == pl.num_programs(1) - 1)
    def _():
        rows_per = 64

        def slab(c, carry):
            rs = pl.ds(pl.multiple_of(c * rows_per, rows_per), rows_per)
            xn = x_ref[rs, :] + gate_ref[...] * o_ref[rs, :]
            y = xn * lax.rsqrt(jnp.mean(xn * xn, axis=-1, keepdims=True) + EPS) * ng_ref[...]
            if post == "modulate":
                o_ref[rs, :] = xn
                h_ref[rs, :] = (y * (1.0 + nsc_ref[...]) + nsh_ref[...]).astype(h_ref.dtype)
            else:
                o_ref[rs, :] = y
            return carry

        lax.fori_loop(0, tm // rows_per, slab, 0)


def _conv_ffn(hx, xs, weights, mods, gate_idx, seg, next_gain, next_mods=None, *, tm, tf, n_lat_rows,
              seq, ctx_len):
    w_g, w_a, cw8, cb, w_down = weights
    n_blk, _, d = hx.shape
    r = n_blk * tm
    dffp = w_down.shape[0]
    nj = dffp // tf
    once = pl.Buffered(1)
    post = "norm" if next_mods is None else "modulate"
    per_tile = tf // w_g.shape[2]
    in_specs = [pl.BlockSpec((None, tm + 16, d), lambda i, j: (i, 0, 0), pipeline_mode=once),
                pl.BlockSpec((per_tile, d, w_g.shape[2]), lambda i, j: (j, 0, 0)),
                pl.BlockSpec((per_tile, d, w_a.shape[2]), lambda i, j: (j, 0, 0)),
                pl.BlockSpec((8, tf), lambda i, j: (0, j)),
                pl.BlockSpec((1, tf), lambda i, j: (0, j)),
                pl.BlockSpec((tf, d), lambda i, j: (j, 0)),
                pl.BlockSpec((tm, d), lambda i, j: (i, 0), pipeline_mode=once),
                pl.BlockSpec((None, None, 1, d), lambda i, j: (seg(i), gate_idx, 0, 0)),
                pl.BlockSpec((1, d), lambda i, j: (0, 0))]
    args = [hx, w_g, w_a, cw8, cb, w_down, xs, mods, next_gain.reshape(1, d)]
    out_spec = pl.BlockSpec((tm, d), lambda i, j: (i, 0), pipeline_mode=once)
    out_shape = jax.ShapeDtypeStruct((r, d), F32)
    if post == "modulate":
        in_specs += [pl.BlockSpec((None, None, 1, d), lambda i, j: (seg(i), 0, 0, 0)),
                     pl.BlockSpec((None, None, 1, d), lambda i, j: (seg(i), 1, 0, 0))]
        args += [next_mods, next_mods]
        out_spec = [out_spec, pl.BlockSpec((tm, d), lambda i, j: (i, 0), pipeline_mode=once)]
        out_shape = [out_shape, jax.ShapeDtypeStruct((r, d), BF16)]
    return pl.pallas_call(
        functools.partial(_ffn_kernel, tm=tm, n_lat_rows=n_lat_rows, seq=seq, ctx_len=ctx_len, post=post),
        out_shape=out_shape,
        grid=(n_blk, nj),
        in_specs=in_specs,
        out_specs=out_spec,
        compiler_params=_cparams("parallel", "arbitrary"),
        name="conv_ffn",
    )(*args)


def _pad_ffn_weights(w_up, conv_w, conv_b, w_down, layer, tf):
    dff = w_down.shape[1]
    dffp = dff + (-dff % tf)
    bw = _pick(dff, (256, 128))
    w_a = _cast_cols(w_up, layer, 0, dff, dffp, bw)
    w_g = _cast_cols(w_up, layer, dff, dff, dffp, bw)
    cw8 = jnp.pad(conv_w[layer].astype(F32), ((0, 8 - conv_w.shape[1]), (0, dffp - dff)))
    cb = jnp.pad(conv_b[layer].astype(F32), (0, dffp - dff)).reshape(1, dffp)
    w_down_p = _cast_rows(w_down, layer, dffp, bw)
    return w_g, w_a, cw8, cb, w_down_p


def kernel(x, c, ctx, c_ctx, mod_w, mod_b, norm_mix_g, norm_ffn_g, ab_w_in, mlstm_gate_b, mlstm_head_g, s5_a_re, s5_a_im, s5_log_dt, s5_b_re, s5_b_im, s5_c_re, s5_c_im, s5_d, s5_glu_w, s5_glu_b, ab_w_out, na_w_qkv, na_rpb, na_w_out, ffn_w_up, ffn_conv_w, ffn_conv_b, ffn_w_down, final_norm_g):
    n_batch, seq, d = x.shape
    ctx_len = ctx.shape[1]
    depth = mod_w.shape[0]
    assert depth == 2 and mod_w.shape[2] == 6 * d
    n_lat_rows = n_batch * seq
    n_ctx_rows = n_batch * ctx_len
    assert seq % MLSTM_CHUNK == 0 and ctx_len == MLSTM_CHUNK and seq % GRID_W == 0

    n_heads = mlstm_gate_b.shape[-1]
    s5w = s5_d.shape[-1]
    mw = d - s5w
    dv = mw // n_heads
    dqk = dv // 2
    qkw = n_heads * dqk
    n_groups, p_state, s5_group = s5_b_re.shape[1:]
    assert S5_CHUNK * s5_group == 2 * LANE and p_state <= LANE and dqk % (2 * LANE) == 0
    na_heads = na_rpb.shape[1]
    hd = d // na_heads
    rows = seq // GRID_W
    kh = min((na_rpb.shape[2] + 1) // 2, rows)
    assert rows % NA_QROWS == 0 and rows >= NA_KROWS and hd == LANE
    dff = ffn_w_down.shape[1]

    tm_big = _pick(math.gcd(n_lat_rows, n_ctx_rows), (1024, 512, 256))
    tm_ffn = _pick(math.gcd(n_lat_rows, n_ctx_rows), (512, 256))
    tf = 512
    seg_big = _seg_fn(tm_big, n_lat_rows, seq, n_batch)
    seg_ffn = _seg_fn(tm_ffn, n_lat_rows, seq, n_batch)
    seg_256 = _seg_fn(256, n_lat_rows, seq, n_batch)

    cond8 = jnp.concatenate([c, c_ctx[None], jnp.zeros((8 - n_batch - 1, d), F32)], axis=0)
    mods = _ada(cond8, mod_w, mod_b).reshape(depth, 8, 6, 1, d)

    x2d, c2d = x.reshape(n_lat_rows, d), ctx.reshape(n_ctx_rows, d)

    h = _modulate2(x2d, c2d, norm_mix_g[0], mods[0], 0, 1, seg_256, 256)
    w_in = ab_w_in[0]
    n_gate = 4 * n_heads
    w1 = w_in[:, :2 * qkw + 2 * mw].astype(BF16)
    w2 = jnp.concatenate([w_in[:, 2 * qkw + 2 * mw + n_gate:], w_in[:, 2 * qkw + 2 * mw:2 * qkw + 2 * mw + n_gate],
                          jnp.zeros((d, LANE - n_gate), F32)], axis=1).astype(BF16)
    p1 = _matmul(h, w1, BF16, tm_big, _pick(w1.shape[1], (512, 256, 128)), "ab_in_proj")
    p2 = _matmul(h, w2, F32, tm_big, w2.shape[1], "ab_in_proj_s5")
    grow = jnp.pad(p2[:, s5w:s5w + n_gate].T, ((0, -n_gate % 8), (0, 0)))

    cos_t, sin_t = _rope_tables(seq, ctx_len, dqk)
    common = dict(n_batch=n_batch, seq=seq, n_heads=n_heads, dqk=dqk, dv=dv)
    hf = _mlstm_direction(p1, p2, grow, mlstm_gate_b[0], cos_t, sin_t, None, None, reverse=False, **common)
    mix_a = _mlstm_direction(p1, p2, grow, mlstm_gate_b[0], cos_t, sin_t, mlstm_head_g[0], hf,
                             reverse=True, **common)

    n_lat_ch, n_ctx_ch = seq // S5_CHUNK, ctx_len // S5_CHUNK
    row_w = p2.shape[1]
    tables = _s5_tables(s5_a_re[0], s5_a_im[0], s5_log_dt[0], s5_b_re[0], s5_b_im[0], s5_c_re[0], s5_c_im[0])
    y16 = _s5_scan(p2.reshape(-1, S5_CHUNK * row_w), tables, nb=n_batch, n_lat=n_lat_ch, n_ctx=n_ctx_ch,
                   n_groups=n_groups, hsz=s5_group, row_w=row_w)
    y_rows = y16.transpose(1, 0, 2).reshape(-1, s5w)
    mix_b = _glu(y_rows, p2, s5_d[0], s5_glu_w[0], s5_glu_b[0], tm_ffn)

    w_out = ab_w_out[0].astype(BF16)
    xs = _matmul_residual([mix_a, mix_b], [w_out[:mw], w_out[mw:]], (x2d, c2d), mods[0], 2, seg_big,
                          tm_big, 512, "ab_out_proj")
    h = _modulate_halo(xs, norm_ffn_g[0], mods[0], 3, 4, seg_ffn, tm_ffn)
    ffn_kw = dict(tm=tm_ffn, tf=tf, n_lat_rows=n_lat_rows, seq=seq, ctx_len=ctx_len)
    xs, h = _conv_ffn(h, xs, _pad_ffn_weights(ffn_w_up, ffn_conv_w, ffn_conv_b, ffn_w_down, 0, tf),
                      mods[0], 5, seg_ffn, norm_mix_g[1], mods[1], **ffn_kw)

    qkv = _matmul(h, na_w_qkv[0].astype(BF16), BF16, tm_big, 512, "na_qkv_proj", unit=hd)
    t2, rmask = _na_tables(na_rpb[0], rows)
    o = _na_attention(qkv, t2, rmask, n_batch=n_batch, seq=seq, ctx_len=ctx_len, n_heads=na_heads, hd=hd,
                      kh=kh, win_h=(na_rpb.shape[2] + 1) // 2)
    xl = _matmul_residual([o], [na_w_out[0].astype(BF16)], xs, mods[1], 2, seg_big, tm_big, 512,
                          "na_out_proj")
    h = _modulate_halo(xl, norm_ffn_g[1], mods[1], 3, 4, seg_ffn, tm_ffn)
    out = _conv_ffn(h, xl, _pad_ffn_weights(ffn_w_up, ffn_conv_w, ffn_conv_b, ffn_w_down, 1, tf),
                    mods[1], 5, seg_ffn, final_norm_g, **ffn_kw)
    return out.reshape(n_batch, seq, d)
```

```python
import functools
import math

import jax
import jax.numpy as jnp
import numpy as np
from jax import lax
from jax.experimental import pallas as pl
from jax.experimental.pallas import tpu as pltpu

F32 = jnp.float32
BF16 = jnp.bfloat16

EPS = 1e-6
ROPE_BASE = 10000.0
GRID_W = 64
MLSTM_CHUNK = 256
S5_CHUNK = 16
NA_QROWS = 4
NA_KROWS = 12
NA_UNROLL = 8
LOG2E = 1.4426950408889634
LANE = 128
VMEM_LIMIT_BYTES = 58 * 1024 * 1024


def _cparams(*sem):
    return pltpu.CompilerParams(dimension_semantics=sem, vmem_limit_bytes=VMEM_LIMIT_BYTES)


def _pick(n, candidates):
    for c in candidates:
        if n % c == 0:
            return c
    raise ValueError(f"no tile in {candidates} divides {n}")


def _ada_kernel(c_ref, w_ref, b_ref, o_ref):
    s = jax.nn.silu(c_ref[...]).astype(BF16)
    o_ref[...] = jnp.dot(s, w_ref[...].astype(BF16), preferred_element_type=F32) + b_ref[...]


def _ada(cond8, mod_w, mod_b):
    depth, d, n6 = mod_w.shape
    tn = _pick(n6, (512, 256, 128))
    return pl.pallas_call(
        _ada_kernel,
        out_shape=jax.ShapeDtypeStruct((depth, 8, n6), F32),
        grid=(depth, n6 // tn),
        in_specs=[pl.BlockSpec((8, d), lambda l, j: (0, 0)),
                  pl.BlockSpec((None, d, tn), lambda l, j: (l, 0, j)),
                  pl.BlockSpec((None, 1, tn), lambda l, j: (l, 0, j))],
        out_specs=pl.BlockSpec((None, 8, tn), lambda l, j: (l, 0, j)),
        compiler_params=_cparams("parallel", "parallel"),
        name="ada_mod",
    )(cond8, mod_w, mod_b.reshape(depth, 1, n6))


def _seg_fn(tm, n_lat_rows, seq, n_batch):
    n_lat_blk = n_lat_rows // tm

    def seg(i):
        return jnp.where(i < n_lat_blk, (i * tm) // seq, n_batch)
    return seg


def _two_source_specs(block, n_first_blk, col=None):
    if col is None:
        return (pl.BlockSpec(block, lambda i: (jnp.minimum(i, n_first_blk - 1), 0)),
                pl.BlockSpec(block, lambda i: (jnp.maximum(i - n_first_blk, 0), 0)))
    return (pl.BlockSpec(block, lambda i, j: (jnp.minimum(i, n_first_blk - 1), jnp.where(i < n_first_blk, j, 0))),
            pl.BlockSpec(block, lambda i, j: (jnp.maximum(i - n_first_blk, 0), jnp.where(i < n_first_blk, 0, j))))


def _modulate2_kernel(x_ref, c_ref, g_ref, sh_ref, sc_ref, o_ref, *, n_first_blk):
    x = jnp.where(pl.program_id(0) < n_first_blk, x_ref[...], c_ref[...])
    y = x * lax.rsqrt(jnp.mean(x * x, axis=-1, keepdims=True) + EPS) * g_ref[...]
    o_ref[...] = (y * (1.0 + sc_ref[...]) + sh_ref[...]).astype(o_ref.dtype)


def _modulate2(x2d, c2d, gain, mods, shift_idx, scale_idx, seg, tm):
    d = x2d.shape[1]
    r = x2d.shape[0] + c2d.shape[0]
    n_first_blk = x2d.shape[0] // tm
    return pl.pallas_call(
        functools.partial(_modulate2_kernel, n_first_blk=n_first_blk),
        out_shape=jax.ShapeDtypeStruct((r, d), BF16),
        grid=(r // tm,),
        in_specs=[*_two_source_specs((tm, d), n_first_blk),
                  pl.BlockSpec((1, d), lambda i: (0, 0)),
                  pl.BlockSpec((None, None, 1, d), lambda i: (seg(i), shift_idx, 0, 0)),
                  pl.BlockSpec((None, None, 1, d), lambda i: (seg(i), scale_idx, 0, 0))],
        out_specs=pl.BlockSpec((tm, d), lambda i: (i, 0)),
        compiler_params=_cparams("parallel"),
        name="modulate",
    )(x2d, c2d, gain.reshape(1, d), mods, mods)


def _modulate_halo_kernel(x_ref, xp_ref, xn_ref, g_ref, sh_ref, sc_ref, o_ref):
    x = jnp.concatenate([x_ref[...], xp_ref[...], xn_ref[...]], axis=0)
    y = x * lax.rsqrt(jnp.mean(x * x, axis=-1, keepdims=True) + EPS) * g_ref[...]
    o_ref[...] = (y * (1.0 + sc_ref[...]) + sh_ref[...]).astype(o_ref.dtype)


def _modulate_halo(xs, gain, mods, shift_idx, scale_idx, seg, tm):
    r, d = xs.shape
    n_blk, per8 = r // tm, tm // 8
    return pl.pallas_call(
        _modulate_halo_kernel,
        out_shape=jax.ShapeDtypeStruct((n_blk, tm + 16, d), BF16),
        grid=(n_blk,),
        in_specs=[pl.BlockSpec((tm, d), lambda i: (i, 0)),
                  pl.BlockSpec((8, d), lambda i: (jnp.maximum(i * per8 - 1, 0), 0)),
                  pl.BlockSpec((8, d), lambda i: (jnp.minimum((i + 1) * per8, r // 8 - 1), 0)),
                  pl.BlockSpec((1, d), lambda i: (0, 0)),
                  pl.BlockSpec((None, None, 1, d), lambda i: (seg(i), shift_idx, 0, 0)),
                  pl.BlockSpec((None, None, 1, d), lambda i: (seg(i), scale_idx, 0, 0))],
        out_specs=pl.BlockSpec((None, tm + 16, d), lambda i: (i, 0, 0)),
        compiler_params=_cparams("parallel"),
        name="modulate_halo",
    )(xs, xs, xs, gain.reshape(1, d), mods, mods)


def _cast_kernel(x_ref, o_ref, *, n_valid):
    keep = pl.program_id(0) < n_valid
    o_ref[...] = jnp.where(keep, x_ref[...], 0.0).astype(o_ref.dtype)


def _cast_cols(w, layer, col0, n_cols, n_cols_pad, bw):
    d = w.shape[1]
    nv, off = n_cols // bw, col0 // bw
    assert n_cols % bw == 0 and col0 % bw == 0 and n_cols_pad % bw == 0
    return pl.pallas_call(
        functools.partial(_cast_kernel, n_valid=nv),
        out_shape=jax.ShapeDtypeStruct((n_cols_pad // bw, d, bw), BF16),
        grid=(n_cols_pad // bw,),
        in_specs=[pl.BlockSpec((None, d, bw), lambda i: (layer, 0, off + jnp.minimum(i, nv - 1)))],
        out_specs=pl.BlockSpec((None, d, bw), lambda i: (i, 0, 0)),
        compiler_params=_cparams("parallel"),
        name="cast_cols",
    )(w)


def _cast_rows(w, layer, n_rows_pad, br):
    n_rows, d = w.shape[1:]
    nv = n_rows // br
    assert n_rows % br == 0 and n_rows_pad % br == 0
    return pl.pallas_call(
        functools.partial(_cast_kernel, n_valid=nv),
        out_shape=jax.ShapeDtypeStruct((n_rows_pad, d), BF16),
        grid=(n_rows_pad // br,),
        in_specs=[pl.BlockSpec((None, br, d), lambda i: (layer, jnp.minimum(i, nv - 1), 0))],
        out_specs=pl.BlockSpec((br, d), lambda i: (i, 0)),
        compiler_params=_cparams("parallel"),
        name="cast_rows",
    )(w)


def _mm_kernel(a_ref, b_ref, o_ref):
    res = jnp.dot(a_ref[...], b_ref[...], preferred_element_type=F32).astype(o_ref.dtype)
    if len(o_ref.shape) == 3:
        w = o_ref.shape[2]
        for t in range(o_ref.shape[0]):
            o_ref[t] = res[:, t * w:(t + 1) * w]
    else:
        o_ref[...] = res


def _matmul(a, b, out_dtype, tm, tn, name, unit=None):
    m, k = a.shape
    n = b.shape[1]
    if unit is None:
        out_shape = jax.ShapeDtypeStruct((m, n), out_dtype)
        out_spec = pl.BlockSpec((tm, tn), lambda i, j: (i, j))
    else:
        out_shape = jax.ShapeDtypeStruct((n // unit, m, unit), out_dtype)
        out_spec = pl.BlockSpec((tn // unit, tm, unit), lambda i, j: (j, i, 0))
    return pl.pallas_call(
        _mm_kernel,
        out_shape=out_shape,
        grid=(m // tm, n // tn),
        in_specs=[pl.BlockSpec((tm, k), lambda i, j: (i, 0)),
                  pl.BlockSpec((k, tn), lambda i, j: (0, j))],
        out_specs=out_spec,
        compiler_params=_cparams("parallel", "parallel"),
        name=name,
    )(a, b)


def _rows_by_unit(ref):
    if len(ref.shape) == 2:
        return ref[...]
    return jnp.concatenate([ref[t] for t in range(ref.shape[0])], axis=1)


def _mm_res_kernel(*refs, n_a, n_first_blk):
    a_refs, b_refs = refs[:n_a], refs[n_a:2 * n_a]
    if n_first_blk is None:
        x_ref, gate_ref, o_ref = refs[2 * n_a:]
        x = x_ref[...]
    else:
        x_ref, c_ref, gate_ref, o_ref = refs[2 * n_a:]
        x = jnp.where(pl.program_id(0) < n_first_blk, x_ref[...], c_ref[...])
    acc = jnp.dot(_rows_by_unit(a_refs[0]), b_refs[0][...], preferred_element_type=F32)
    for a_ref, b_ref in zip(a_refs[1:], b_refs[1:]):
        acc = acc + jnp.dot(_rows_by_unit(a_ref), b_ref[...], preferred_element_type=F32)
    o_ref[...] = x + gate_ref[...] * acc


def _matmul_residual(a_list, b_list, xs, mods, gate_idx, seg, tm, tn, name):
    m = a_list[0].shape[-2]
    n = b_list[0].shape[1]

    def a_spec(a):
        if a.ndim == 2:
            return pl.BlockSpec((tm, a.shape[1]), lambda i, j: (i, 0))
        return pl.BlockSpec((a.shape[0], tm, a.shape[2]), lambda i, j: (0, i, 0))

    if isinstance(xs, tuple):
        n_first_blk = xs[0].shape[0] // tm
        x_specs = list(_two_source_specs((tm, tn), n_first_blk, col=True))
        x_args = list(xs)
    else:
        n_first_blk = None
        x_specs = [pl.BlockSpec((tm, tn), lambda i, j: (i, j))]
        x_args = [xs]
    in_specs = ([a_spec(a) for a in a_list]
                + [pl.BlockSpec((b.shape[0], tn), lambda i, j: (0, j)) for b in b_list]
                + x_specs
                + [pl.BlockSpec((None, None, 1, tn), lambda i, j: (seg(i), gate_idx, 0, j))])
    return pl.pallas_call(
        functools.partial(_mm_res_kernel, n_a=len(a_list), n_first_blk=n_first_blk),
        out_shape=jax.ShapeDtypeStruct((m, n), F32),
        grid=(m // tm, n // tn),
        in_specs=in_specs,
        out_specs=pl.BlockSpec((tm, tn), lambda i, j: (i, j)),
        compiler_params=_cparams("parallel", "parallel"),
        name=name,
    )(*a_list, *b_list, *x_args, mods)


def _rope(x, cos, sin):
    parts = []
    for j in range(x.shape[1] // LANE):
        sl = slice(j * LANE, (j + 1) * LANE)
        xs = x[:, sl]
        parts.append(xs * cos[:, sl] + pltpu.roll(xs, LANE // 2, axis=1) * sin[:, sl])
    return jnp.concatenate(parts, axis=1)


def _mlstm_kernel(gb_ref, q_ref, k_ref, v_ref, cos_ref, sin_ref, gcol_ref, grow_ref, *rest,
                  reverse, n_heads, scale):
    if reverse:
        o_ref, hf_ref, hg_ref, out_ref, ct_ref, m_ref = rest
    else:
        out_ref, ct_ref, m_ref = rest
    step = pl.program_id(1)
    d = 1 if reverse else 0
    L = q_ref.shape[0]
    dqk = q_ref.shape[1] // n_heads
    dv = v_ref.shape[1] // n_heads

    @pl.when(step == 0)
    def _():
        ct_ref[...] = jnp.zeros_like(ct_ref)
        m_ref[...] = jnp.zeros_like(m_ref)

    row = lax.broadcasted_iota(jnp.int32, (L, L), 0)
    col = lax.broadcasted_iota(jnp.int32, (L, L), 1)
    incl = (col >= row) if reverse else (col <= row)
    incl_t = (row >= col) if reverse else (row <= col)
    n_lat = pl.num_programs(1) - 1
    tab = jnp.where(step == 0, n_lat, (n_lat - step) if reverse else (step - 1))
    t0 = pl.multiple_of(tab * L, L)
    cos = cos_ref[pl.ds(t0, L), :]
    sin = sin_ref[pl.ds(t0, L), :]
    ones = jnp.ones((L, LANE), v_ref.dtype)

    for hh in range(n_heads):
        gi = (d * 2 + 0) * n_heads + hh
        gf = (d * 2 + 1) * n_heads + hh
        b_i = gb_ref[gi]
        b_f = gb_ref[gf]
        li_col = gcol_ref[:, gi:gi + 1] + b_i
        lf_col = jax.nn.log_sigmoid(gcol_ref[:, gf:gf + 1] + b_f)
        li_row = grow_ref[gi:gi + 1, :] + b_i
        lf_row = jax.nn.log_sigmoid(grow_ref[gf:gf + 1, :] + b_f)
        cum_col = jnp.sum(jnp.where(incl, lf_row, 0.0), axis=1, keepdims=True)
        cum_row = jnp.sum(jnp.where(incl_t, lf_col, 0.0), axis=0, keepdims=True)
        total = jnp.sum(lf_row, axis=1, keepdims=True)
        m_prev = m_ref[hh, 0:1, 0:1]

        dmat = jnp.where(incl, cum_col - cum_row + li_row, -jnp.inf)
        carried = cum_col + m_prev
        m_loc = jnp.maximum(carried, jnp.max(dmat, axis=1, keepdims=True))
        w = jnp.exp(dmat - m_loc)
        w_state = jnp.exp(carried - m_loc)

        qs = slice(hh * dqk, (hh + 1) * dqk)
        vs = slice(hh * dv, (hh + 1) * dv)
        qb = (_rope(q_ref[:, qs].astype(F32), cos, sin) * scale).astype(BF16)
        kb = _rope(k_ref[:, qs].astype(F32), cos, sin).astype(BF16)
        s_qk = lax.dot_general(qb, kb, (((1,), (1,)), ((), ())), preferred_element_type=F32)
        sc = (s_qk * w).astype(BF16)
        vext = jnp.concatenate([v_ref[:, vs], ones], axis=1)
        ct = ct_ref[hh]
        res = (jnp.dot(sc, vext, preferred_element_type=F32)
               + w_state * jnp.dot(qb, ct.astype(BF16), preferred_element_type=F32))
        den = res[:, dv:dv + 1]
        hval = res[:, :dv] / jnp.maximum(jnp.abs(den), jnp.exp(-m_loc))

        src = total - cum_col + li_col
        m_new = jnp.maximum(total + m_prev, jnp.max(src, axis=0, keepdims=True))
        decay = jnp.exp(total + m_prev - m_new)
        w_src = jnp.exp(src - m_new)
        upd = lax.dot_general(kb, (w_src * vext.astype(F32)).astype(BF16), (((0,), (0,)), ((), ())),
                              preferred_element_type=F32)
        ct_ref[hh] = decay * ct + upd
        m_ref[hh] = jnp.broadcast_to(m_new, m_ref.shape[1:])

        if reverse:
            hm = hf_ref[:, vs] + hval
            hn = hm * lax.rsqrt(jnp.mean(hm * hm, axis=-1, keepdims=True) + EPS) * hg_ref[:, vs]
            out_ref[:, vs] = (jax.nn.sigmoid(o_ref[:, vs].astype(F32)) * hn).astype(out_ref.dtype)
        else:
            out_ref[:, vs] = hval


def _mlstm_direction(p1, p2, grow, gate_b, cos_t, sin_t, head_g, hf, *, reverse, n_batch, seq,
                     n_heads, dqk, dv):
    r = p1.shape[0]
    L = MLSTM_CHUNK
    n_lat = seq // L
    lat_blocks = n_batch * n_lat
    qkw, mw = n_heads * dqk, n_heads * dv
    gate_tile = p2.shape[1] // LANE - 1

    def blk(b, s):
        lat = b * n_lat + ((n_lat - s) if reverse else (s - 1))
        return jnp.where(s == 0, lat_blocks + b, lat)

    once = pl.Buffered(1)
    in_specs = [
        pl.BlockSpec(memory_space=pltpu.SMEM),
        pl.BlockSpec((L, qkw), lambda b, s: (blk(b, s), 0)),
        pl.BlockSpec((L, qkw), lambda b, s: (blk(b, s), 1)),
        pl.BlockSpec((L, mw), lambda b, s: (blk(b, s), (2 * qkw) // mw)),
        pl.BlockSpec(cos_t.shape, lambda b, s: (0, 0), pipeline_mode=once),
        pl.BlockSpec(sin_t.shape, lambda b, s: (0, 0), pipeline_mode=once),
        pl.BlockSpec((L, LANE), lambda b, s: (blk(b, s), gate_tile)),
        pl.BlockSpec((grow.shape[0], L), lambda b, s: (0, blk(b, s))),
    ]
    args = [gate_b.reshape(-1), p1, p1, p1, cos_t, sin_t, p2, grow]
    if reverse:
        in_specs += [
            pl.BlockSpec((L, mw), lambda b, s: (blk(b, s), (2 * qkw) // mw + 1)),
            pl.BlockSpec((L, mw), lambda b, s: (blk(b, s), 0)),
            pl.BlockSpec((1, mw), lambda b, s: (0, 0)),
        ]
        args += [p1, hf, head_g.reshape(1, mw)]
    return pl.pallas_call(
        functools.partial(_mlstm_kernel, reverse=reverse, n_heads=n_heads, scale=dqk ** -0.5),
        out_shape=jax.ShapeDtypeStruct((r, mw), BF16 if reverse else F32),
        grid=(n_batch, n_lat + 1),
        in_specs=in_specs,
        out_specs=pl.BlockSpec((L, mw), lambda b, s: (blk(b, s), 0)),
        scratch_shapes=[pltpu.VMEM((n_heads, dqk, dv + LANE), F32), pltpu.VMEM((n_heads, 8, LANE), F32)],
        compiler_params=_cparams("parallel", "arbitrary"),
        name="mlstm_bwd" if reverse else "mlstm_fwd",
    )(*args)


def _rope_tables(seq, ctx_len, dqk):
    half = dqk // 2
    inv = ROPE_BASE ** (-jnp.arange(0, half, 2, dtype=F32) / half)
    t = jnp.arange(seq)

    def one(pos):
        ang = pos.astype(F32)[:, None] * inv
        c, s = jnp.cos(ang), jnp.sin(ang)
        return jnp.concatenate([c, c], axis=-1), jnp.concatenate([-s, s], axis=-1)

    c_r, s_r = one(t // GRID_W)
    c_c, s_c = one(t % GRID_W)
    cos_t = jnp.concatenate([c_r, c_c], axis=-1)
    sin_t = jnp.concatenate([s_r, s_c], axis=-1)
    cos_t = jnp.concatenate([cos_t, jnp.ones((ctx_len, dqk), F32)], axis=0)
    sin_t = jnp.concatenate([sin_t, jnp.zeros((ctx_len, dqk), F32)], axis=0)
    return cos_t, sin_t


def _s5_tables(a_re, a_im, log_dt, b_re, b_im, c_re, c_im):
    g_, p_, hsz = b_re.shape
    lc = S5_CHUNK
    bmat = lax.complex(b_re.astype(F32), b_im.astype(F32))
    cmat = lax.complex(c_re.astype(F32), c_im.astype(F32))
    tau = jnp.arange(lc + 1, dtype=F32)
    s_idx = jnp.arange(lc)
    m_tot = 0.0
    w_parts, v_parts, a_parts = [], [], []
    for d in range(2):
        lam = lax.complex(a_re[d].astype(F32), a_im[d].astype(F32))
        dt = jnp.exp(log_dt[d].astype(F32))[:, None]
        fac = (jnp.exp(lam * dt) - 1) / lam
        apow = jnp.exp(tau[:, None, None] * (lam * dt)[None])
        bf = bmat * fac[:, :, None]
        cb = cmat.transpose(0, 2, 1)[:, :, :, None] * bf[:, :, None, :]
        kt = jnp.real(jnp.einsum('tgp,gpij->tgij', apow[:lc], cb,
                                 precision=lax.Precision.HIGHEST))
        lag = (s_idx[None, :] - s_idx[:, None]) if d == 0 else (s_idx[:, None] - s_idx[None, :])
        ok = lag >= 0
        kk = kt[jnp.clip(lag, 0, lc - 1)]
        kk = jnp.where(ok[:, :, None, None, None], kk, 0.0)
        m_tot = m_tot + kk.transpose(2, 0, 4, 1, 3).reshape(g_, lc * hsz, lc * hsz)
        pw = apow[(lc - 1 - s_idx) if d == 0 else s_idx]
        wc = pw[:, :, :, None] * bf[None]
        wc = wc.transpose(1, 0, 3, 2).reshape(g_, lc * hsz, p_)
        pv = apow[(s_idx + 1) if d == 0 else (lc - s_idx)]
        vc = cmat[None] * pv[:, :, None, :]
        vc = vc.transpose(1, 3, 0, 2).reshape(g_, p_, lc * hsz)
        pad_w = jnp.zeros((g_, lc * hsz, LANE - p_), F32)
        pad_v = jnp.zeros((g_, LANE - p_, lc * hsz), F32)
        w_parts += [jnp.real(wc), pad_w, jnp.imag(wc), pad_w]
        v_parts += [jnp.real(vc), pad_v, -jnp.imag(vc), pad_v]
        a_chunk = apow[lc]
        pad_a = jnp.zeros((g_, LANE - p_), F32)
        a_parts += [jnp.concatenate([jnp.real(a_chunk), pad_a], axis=1),
                    jnp.concatenate([jnp.imag(a_chunk), pad_a], axis=1)]
    w_all = jnp.concatenate(w_parts, axis=2)
    v_all = jnp.concatenate(v_parts, axis=1)
    a_all = jnp.stack(a_parts + [jnp.zeros_like(a_parts[0])] * 4, axis=1)
    return m_tot.astype(BF16), w_all.astype(BF16), v_all.astype(BF16), a_all


def _s5_kernel(*refs, nb, n_lat, n_ctx, hsz):
    lc = S5_CHUNK
    x_refs = refs[:lc]
    m_ref, w_ref, v_ref, a_ref, o_ref = refs[lc:lc + 5]
    z_refs = refs[lc + 5:lc + 9]
    sp_refs = refs[lc + 9:lc + 13]
    k = pl.program_id(0) % (LANE // hsz)
    cw = lc * hsz
    ri = lax.broadcasted_iota(jnp.int32, (lc * LANE, cw), 0)
    ci = lax.broadcasted_iota(jnp.int32, (lc * LANE, cw), 1)
    sel = jnp.where(ri == (ci // hsz) * LANE + k * hsz + ci % hsz, 1.0, 0.0).astype(BF16)
    ri_t = lax.broadcasted_iota(jnp.int32, (cw, lc * LANE), 0)
    ci_t = lax.broadcasted_iota(jnp.int32, (cw, lc * LANE), 1)
    sel_t = jnp.where(ci_t == (ri_t // hsz) * LANE + k * hsz + ri_t % hsz, 1.0, 0.0).astype(BF16)

    x = jnp.concatenate([r[...] for r in x_refs], axis=1).astype(BF16)
    ub = jnp.dot(x, sel, preferred_element_type=F32).astype(BF16)
    z = jnp.dot(ub, w_ref[...], preferred_element_type=F32)
    for q in range(4):
        z_refs[q][...] = z[:, q * LANE:(q + 1) * LANE]
    n_chunks = n_lat + n_ctx

    def rows_of(chunk):
        if chunk < n_lat:
            return pl.ds(chunk, nb, stride=n_lat)
        return pl.ds(nb * n_lat + chunk - n_lat, nb, stride=n_ctx)

    a = a_ref[...]
    coef = [(a[2 * d:2 * d + 1, :], a[2 * d + 1:2 * d + 2, :]) for d in range(2)]
    state = [(jnp.zeros((nb, LANE), F32), jnp.zeros((nb, LANE), F32)) for _ in range(2)]
    for t in range(n_chunks):
        chunk = ((t + n_lat) % n_chunks, n_chunks - 1 - t)
        for d in range(2):
            rs = rows_of(chunk[d])
            a_r, a_i = coef[d]
            s_re, s_im = state[d]
            sp_refs[2 * d][rs, :] = s_re
            sp_refs[2 * d + 1][rs, :] = s_im
            z_re = z_refs[2 * d][rs, :]
            z_im = z_refs[2 * d + 1][rs, :]
            state[d] = (a_r * s_re - a_i * s_im + z_re, a_r * s_im + a_i * s_re + z_im)
    sp = jnp.concatenate([r[...] for r in sp_refs], axis=1)
    y = (jnp.dot(ub, m_ref[...], preferred_element_type=F32)
         + jnp.dot(sp.astype(BF16), v_ref[...], preferred_element_type=F32))
    y_hi = y.astype(BF16)
    y_lo = (y - y_hi.astype(F32)).astype(BF16)
    back = (jnp.dot(y_hi, sel_t, preferred_element_type=F32)
            + jnp.dot(y_lo, sel_t, preferred_element_type=F32))

    @pl.when(k == 0)
    def _():
        o_ref[...] = jnp.zeros_like(o_ref)

    for s in range(lc):
        o_ref[s] += back[:, s * LANE:(s + 1) * LANE]


def _s5_scan(p2v, tables, *, nb, n_lat, n_ctx, n_groups, hsz, row_w):
    m_tot, w_all, v_all, a_all = tables
    rows = p2v.shape[0]
    lc = S5_CHUNK
    cw = lc * hsz
    gpt = LANE // hsz
    tiles_per_tok = row_w // LANE

    def x_spec(s):
        return pl.BlockSpec((rows, LANE), lambda g: (0, s * tiles_per_tok + g // gpt))

    return pl.pallas_call(
        functools.partial(_s5_kernel, nb=nb, n_lat=n_lat, n_ctx=n_ctx, hsz=hsz),
        out_shape=jax.ShapeDtypeStruct((lc, rows, n_groups * hsz), F32),
        grid=(n_groups,),
        in_specs=[x_spec(s) for s in range(lc)] + [
            pl.BlockSpec((None, cw, cw), lambda g: (g, 0, 0)),
            pl.BlockSpec((None, cw, 4 * LANE), lambda g: (g, 0, 0)),
            pl.BlockSpec((None, 4 * LANE, cw), lambda g: (g, 0, 0)),
            pl.BlockSpec((None, 8, LANE), lambda g: (g, 0, 0))],
        out_specs=pl.BlockSpec((lc, rows, LANE), lambda g: (0, 0, g // gpt)),
        scratch_shapes=[pltpu.VMEM((rows, LANE), F32)] * 8,
        compiler_params=_cparams("arbitrary"),
        name="s5_scan",
    )(*([p2v] * lc), m_tot, w_all, v_all, a_all)


def _glu_kernel(y_ref, p_ref, d_ref, w_ref, b_ref, o_ref):
    width = y_ref.shape[1]
    g = jax.nn.gelu(y_ref[...] + d_ref[...] * p_ref[:, :width])
    z = jnp.dot(g.astype(BF16), w_ref[...], preferred_element_type=F32) + b_ref[...]
    o_ref[...] = (g * jax.nn.sigmoid(z)).astype(o_ref.dtype)


def _glu(y, p2, d_skip, glu_w, glu_b, tm):
    r, w = y.shape
    return pl.pallas_call(
        _glu_kernel,
        out_shape=jax.ShapeDtypeStruct((r, w), BF16),
        grid=(r // tm,),
        in_specs=[pl.BlockSpec((tm, w), lambda i: (i, 0)),
                  pl.BlockSpec((tm, p2.shape[1]), lambda i: (i, 0)),
                  pl.BlockSpec((1, w), lambda i: (0, 0)),
                  pl.BlockSpec((w, w), lambda i: (0, 0)),
                  pl.BlockSpec((1, w), lambda i: (0, 0))],
        out_specs=pl.BlockSpec((tm, w), lambda i: (i, 0)),
        compiler_params=_cparams("parallel"),
        name="s5_glu",
    )(y, p2, d_skip.astype(F32).reshape(1, w), glu_w.astype(BF16), glu_b.astype(F32).reshape(1, w))


def _na_variants(rows, kh):
    n_blk = rows // NA_QROWS
    out = []
    for i in (0, min(1, n_blk - 1), n_blk - 1):
        out.append((i, min(max(i * NA_QROWS - kh // 2, 0), rows - NA_KROWS)))
    return out


def _na_tables(rpb, rows):
    n_heads, nri, nci = rpb.shape
    win_h, win_w = (nri + 1) // 2, (nci + 1) // 2
    kh = min(win_h, rows)
    col = np.arange(GRID_W)
    col_start = np.clip(col - win_w // 2, 0, GRID_W - win_w)
    col_ok = (col[None, :] >= col_start[:, None]) & (col[None, :] < col_start[:, None] + win_w)
    ci = np.clip(col[None, :] - col[:, None], -(win_w - 1), win_w - 1) + (win_w - 1)
    onehot = (ci[None] == np.arange(nci)[:, None, None]).astype(np.float32)
    t1 = jnp.einsum('hrc,cqk->hrqk', rpb.astype(F32), onehot, precision=lax.Precision.HIGHEST)
    t1 = jnp.where(col_ok[None, None], t1, -jnp.inf)
    dummy = jnp.full((n_heads, 1, GRID_W, GRID_W), -jnp.inf, F32)
    t1 = jnp.concatenate([dummy, t1, dummy], axis=1)
    t2 = jnp.concatenate([t1[:, :nri + 1], t1[:, 1:nri + 2]], axis=-1)
    rmask = np.zeros((3, NA_QROWS, NA_KROWS // 2, 1, 2 * GRID_W), np.float32)
    for v, (i, ks) in enumerate(_na_variants(rows, kh)):
        for a in range(NA_QROWS):
            r_start = min(max(i * NA_QROWS + a - kh // 2, 0), rows - kh)
            for b in range(NA_KROWS):
                if not (r_start <= ks + b < r_start + kh):
                    rmask[v, a, b // 2, 0, (b % 2) * GRID_W:(b % 2 + 1) * GRID_W] = -np.inf
    return t2, jnp.asarray(rmask)


def _na_kernel(q_ref, k_ref, v_ref, kc_ref, vc_ref, t2_ref, rm_ref, o_ref, bias_ref, *, rows, kh, win_h,
               scale):
    n_blk = rows // NA_QROWS
    tq = NA_QROWS * GRID_W
    tk = NA_KROWS * GRID_W
    n_r = t2_ref.shape[0]

    @pl.when(pl.program_id(1) == 0)
    def _():
        for v, (i, ks) in enumerate(_na_variants(rows, kh)):
            for a in range(NA_QROWS):
                for pair in range(NA_KROWS // 2):
                    r = (ks + 2 * pair) - (i * NA_QROWS + a) + (win_h - 1)
                    tile = t2_ref[min(max(r + 1, 0), n_r - 1)] + rm_ref[v, a, pair]
                    bias_ref[v, a * GRID_W:(a + 1) * GRID_W,
                             pair * 2 * GRID_W:(pair + 1) * 2 * GRID_W] = tile * LOG2E

    kc = kc_ref[...]
    vc = vc_ref[...]
    nt = (((1,), (1,)), ((), ()))

    def body(i, carry):
        q0 = pl.multiple_of(i * tq, tq)
        ks = jnp.clip(i * NA_QROWS - kh // 2, 0, rows - NA_KROWS)
        k0 = pl.multiple_of(ks * GRID_W, 4 * GRID_W)
        variant = jnp.where(i == 0, 0, jnp.where(i == n_blk - 1, 2, 1))
        qb = q_ref[pl.ds(q0, tq), :]
        kb = k_ref[pl.ds(k0, tk), :]
        vb = v_ref[pl.ds(k0, tk), :]
        s_loc = lax.dot_general(qb, kb, nt, preferred_element_type=F32) * (scale * LOG2E) + bias_ref[variant]
        s_ctx = lax.dot_general(qb, kc, nt, preferred_element_type=F32) * (scale * LOG2E)
        m = jnp.maximum(jnp.max(s_loc, axis=1, keepdims=True), jnp.max(s_ctx, axis=1, keepdims=True))
        p_loc = jnp.exp2(s_loc - m)
        p_ctx = jnp.exp2(s_ctx - m)
        denom = jnp.sum(p_loc, axis=1, keepdims=True) + jnp.sum(p_ctx, axis=1, keepdims=True)
        acc = (jnp.dot(p_loc.astype(BF16), vb, preferred_element_type=F32)
               + jnp.dot(p_ctx.astype(BF16), vc, preferred_element_type=F32))
        o_ref[pl.ds(q0, tq), :] = (acc / denom).astype(o_ref.dtype)
        return carry

    lax.fori_loop(0, n_blk, body, 0, unroll=NA_UNROLL)


def _na_attention(qkv, t2, rmask, *, n_batch, seq, ctx_len, n_heads, hd, kh, win_h):
    rows = seq // GRID_W
    lat_ctx_blk = (n_batch * seq) // ctx_len
    tq, tk = NA_QROWS * GRID_W, NA_KROWS * GRID_W
    return pl.pallas_call(
        functools.partial(_na_kernel, rows=rows, kh=kh, win_h=win_h, scale=hd ** -0.5),
        out_shape=jax.ShapeDtypeStruct((n_heads, n_batch * seq, hd), BF16),
        grid=(n_heads, n_batch),
        in_specs=[pl.BlockSpec((None, seq, hd), lambda h, b: (h, b, 0)),
                  pl.BlockSpec((None, seq, hd), lambda h, b: (n_heads + h, b, 0)),
                  pl.BlockSpec((None, seq, hd), lambda h, b: (2 * n_heads + h, b, 0)),
                  pl.BlockSpec((None, ctx_len, hd), lambda h, b: (n_heads + h, lat_ctx_blk + b, 0)),
                  pl.BlockSpec((None, ctx_len, hd), lambda h, b: (2 * n_heads + h, lat_ctx_blk + b, 0)),
                  pl.BlockSpec((None,) + t2.shape[1:], lambda h, b: (h, 0, 0, 0)),
                  pl.BlockSpec(rmask.shape, lambda h, b: (0, 0, 0, 0, 0))],
        out_specs=pl.BlockSpec((None, seq, hd), lambda h, b: (h, b, 0)),
        scratch_shapes=[pltpu.VMEM((3, tq, tk), F32)],
        compiler_params=_cparams("arbitrary", "arbitrary"),
        name="na_attention",
    )(qkv, qkv, qkv, qkv, qkv, t2, rmask)


def _ffn_kernel(hx_ref, wg_ref, wa_ref, cw_ref, cb_ref, wd_ref, x_ref, gate_ref, ng_ref, *rest,
                tm, n_lat_rows, seq, ctx_len, post):
    if post == "modulate":
        nsh_ref, nsc_ref, o_ref, h_ref = rest
    else:
        (o_ref,) = rest
    i = pl.program_id(0)
    j = pl.program_id(1)

    @pl.when(j == 0)
    def _():
        o_ref[...] = jnp.zeros_like(o_ref)

    g_all = jnp.dot(hx_ref[...], _rows_by_unit(wg_ref), preferred_element_type=F32)
    a = jnp.dot(hx_ref[:tm, :], _rows_by_unit(wa_ref), preferred_element_type=F32)
    g = g_all[:tm]
    local = lax.broadcasted_iota(jnp.int32, (tm, 1), 0)
    grow = i * tm + local
    in_lat = grow < n_lat_rows
    pos = jnp.where(in_lat, grow % seq, (grow - n_lat_rows) % ctx_len)
    last = jnp.where(in_lat, seq - 1, ctx_len - 1)
    g_prev = jnp.where(local == 0, g_all[tm + 7:tm + 8, :], pltpu.roll(g, 1, axis=0))
    g_prev = jnp.where(pos == 0, 0.0, g_prev)
    g_next = jnp.where(local == tm - 1, g_all[tm + 8:tm + 9, :], pltpu.roll(g, tm - 1, axis=0))
    g_next = jnp.where(pos == last, 0.0, g_next)
    cw = cw_ref[...]
    gl = jax.nn.gelu(cw[0:1, :] * g_prev + cw[1:2, :] * g + cw[2:3, :] * g_next + cb_ref[...])
    o_ref[...] += jnp.dot((gl * a).astype(BF16), wd_ref[...], preferred_element_type=F32)

    @pl.when(j == pl.num_programs(1) - 1)
    def _():
        rows_per = 64

        def slab(c, carry):
            rs = pl.ds(pl.multiple_of(c * rows_per, rows_per), rows_per)
            xn = x_ref[rs, :] + gate_ref[...] * o_ref[rs, :]
            y = xn * lax.rsqrt(jnp.mean(xn * xn, axis=-1, keepdims=True) + EPS) * ng_ref[...]
            if post == "modulate":
                o_ref[rs, :] = xn
                h_ref[rs, :] = (y * (1.0 + nsc_ref[...]) + nsh_ref[...]).astype(h_ref.dtype)
            else:
                o_ref[rs, :] = y
            return carry

        lax.fori_loop(0, tm // rows_per, slab, 0)


def _conv_ffn(hx, xs, weights, mods, gate_idx, seg, next_gain, next_mods=None, *, tm, tf, n_lat_rows,
              seq, ctx_len):
    w_g, w_a, cw8, cb, w_down = weights
    n_blk, _, d = hx.shape
    r = n_blk * tm
    dffp = w_down.shape[0]
    nj = dffp // tf
    once = pl.Buffered(1)
    post = "norm" if next_mods is None else "modulate"
    per_tile = tf // w_g.shape[2]
    in_specs = [pl.BlockSpec((None, tm + 16, d), lambda i, j: (i, 0, 0), pipeline_mode=once),
                pl.BlockSpec((per_tile, d, w_g.shape[2]), lambda i, j: (j, 0, 0)),
                pl.BlockSpec((per_tile, d, w_a.shape[2]), lambda i, j: (j, 0, 0)),
                pl.BlockSpec((8, tf), lambda i, j: (0, j)),
                pl.BlockSpec((1, tf), lambda i, j: (0, j)),
                pl.BlockSpec((tf, d), lambda i, j: (j, 0)),
                pl.BlockSpec((tm, d), lambda i, j: (i, 0), pipeline_mode=once),
                pl.BlockSpec((None, None, 1, d), lambda i, j: (seg(i), gate_idx, 0, 0)),
                pl.BlockSpec((1, d), lambda i, j: (0, 0))]
    args = [hx, w_g, w_a, cw8, cb, w_down, xs, mods, next_gain.reshape(1, d)]
    out_spec = pl.BlockSpec((tm, d), lambda i, j: (i, 0), pipeline_mode=once)
    out_shape = jax.ShapeDtypeStruct((r, d), F32)
    if post == "modulate":
        in_specs += [pl.BlockSpec((None, None, 1, d), lambda i, j: (seg(i), 0, 0, 0)),
                     pl.BlockSpec((None, None, 1, d), lambda i, j: (seg(i), 1, 0, 0))]
        args += [next_mods, next_mods]
        out_spec = [out_spec, pl.BlockSpec((tm, d), lambda i, j: (i, 0), pipeline_mode=once)]
        out_shape = [out_shape, jax.ShapeDtypeStruct((r, d), BF16)]
    return pl.pallas_call(
        functools.partial(_ffn_kernel, tm=tm, n_lat_rows=n_lat_rows, seq=seq, ctx_len=ctx_len, post=post),
        out_shape=out_shape,
        grid=(n_blk, nj),
        in_specs=in_specs,
        out_specs=out_spec,
        compiler_params=_cparams("parallel", "arbitrary"),
        name="conv_ffn",
    )(*args)


def _pad_ffn_weights(w_up, conv_w, conv_b, w_down, layer, tf):
    dff = w_down.shape[1]
    dffp = dff + (-dff % tf)
    bw = _pick(dff, (256, 128))
    w_a = _cast_cols(w_up, layer, 0, dff, dffp, bw)
    w_g = _cast_cols(w_up, layer, dff, dff, dffp, bw)
    cw8 = jnp.pad(conv_w[layer].astype(F32), ((0, 8 - conv_w.shape[1]), (0, dffp - dff)))
    cb = jnp.pad(conv_b[layer].astype(F32), (0, dffp - dff)).reshape(1, dffp)
    w_down_p = _cast_rows(w_down, layer, dffp, bw)
    return w_g, w_a, cw8, cb, w_down_p


def kernel(x, c, ctx, c_ctx, mod_w, mod_b, norm_mix_g, norm_ffn_g, ab_w_in, mlstm_gate_b, mlstm_head_g, s5_a_re, s5_a_im, s5_log_dt, s5_b_re, s5_b_im, s5_c_re, s5_c_im, s5_d, s5_glu_w, s5_glu_b, ab_w_out, na_w_qkv, na_rpb, na_w_out, ffn_w_up, ffn_conv_w, ffn_conv_b, ffn_w_down, final_norm_g):
    n_batch, seq, d = x.shape
    ctx_len = ctx.shape[1]
    depth = mod_w.shape[0]
    assert depth == 2 and mod_w.shape[2] == 6 * d
    n_lat_rows = n_batch * seq
    n_ctx_rows = n_batch * ctx_len
    assert seq % MLSTM_CHUNK == 0 and ctx_len == MLSTM_CHUNK and seq % GRID_W == 0

    n_heads = mlstm_gate_b.shape[-1]
    s5w = s5_d.shape[-1]
    mw = d - s5w
    dv = mw // n_heads
    dqk = dv // 2
    qkw = n_heads * dqk
    n_groups, p_state, s5_group = s5_b_re.shape[1:]
    assert S5_CHUNK * s5_group == 2 * LANE and p_state <= LANE and dqk % (2 * LANE) == 0
    na_heads = na_rpb.shape[1]
    hd = d // na_heads
    rows = seq // GRID_W
    kh = min((na_rpb.shape[2] + 1) // 2, rows)
    assert rows % NA_QROWS == 0 and rows >= NA_KROWS and hd == LANE
    dff = ffn_w_down.shape[1]

    tm_big = _pick(math.gcd(n_lat_rows, n_ctx_rows), (1024, 512, 256))
    tm_ffn = _pick(math.gcd(n_lat_rows, n_ctx_rows), (512, 256))
    tf = 512
    seg_big = _seg_fn(tm_big, n_lat_rows, seq, n_batch)
    seg_ffn = _seg_fn(tm_ffn, n_lat_rows, seq, n_batch)
    seg_256 = _seg_fn(256, n_lat_rows, seq, n_batch)

    cond8 = jnp.concatenate([c, c_ctx[None], jnp.zeros((8 - n_batch - 1, d), F32)], axis=0)
    mods = _ada(cond8, mod_w, mod_b).reshape(depth, 8, 6, 1, d)

    x2d, c2d = x.reshape(n_lat_rows, d), ctx.reshape(n_ctx_rows, d)

    h = _modulate2(x2d, c2d, norm_mix_g[0], mods[0], 0, 1, seg_256, 256)
    w_in = ab_w_in[0]
    n_gate = 4 * n_heads
    w1 = w_in[:, :2 * qkw + 2 * mw].astype(BF16)
    w2 = jnp.concatenate([w_in[:, 2 * qkw + 2 * mw + n_gate:], w_in[:, 2 * qkw + 2 * mw:2 * qkw + 2 * mw + n_gate],
                          jnp.zeros((d, LANE - n_gate), F32)], axis=1).astype(BF16)
    p1 = _matmul(h, w1, BF16, tm_big, _pick(w1.shape[1], (1024, 512, 256, 128)), "ab_in_proj")
    p2 = _matmul(h, w2, F32, tm_big, w2.shape[1], "ab_in_proj_s5")
    grow = jnp.pad(p2[:, s5w:s5w + n_gate].T, ((0, -n_gate % 8), (0, 0)))

    cos_t, sin_t = _rope_tables(seq, ctx_len, dqk)
    common = dict(n_batch=n_batch, seq=seq, n_heads=n_heads, dqk=dqk, dv=dv)
    hf = _mlstm_direction(p1, p2, grow, mlstm_gate_b[0], cos_t, sin_t, None, None, reverse=False, **common)
    mix_a = _mlstm_direction(p1, p2, grow, mlstm_gate_b[0], cos_t, sin_t, mlstm_head_g[0], hf,
                             reverse=True, **common)

    n_lat_ch, n_ctx_ch = seq // S5_CHUNK, ctx_len // S5_CHUNK
    row_w = p2.shape[1]
    tables = _s5_tables(s5_a_re[0], s5_a_im[0], s5_log_dt[0], s5_b_re[0], s5_b_im[0], s5_c_re[0], s5_c_im[0])
    y16 = _s5_scan(p2.reshape(-1, S5_CHUNK * row_w), tables, nb=n_batch, n_lat=n_lat_ch, n_ctx=n_ctx_ch,
                   n_groups=n_groups, hsz=s5_group, row_w=row_w)
    y_rows = y16.transpose(1, 0, 2).reshape(-1, s5w)
    mix_b = _glu(y_rows, p2, s5_d[0], s5_glu_w[0], s5_glu_b[0], tm_ffn)

    w_out = ab_w_out[0].astype(BF16)
    xs = _matmul_residual([mix_a, mix_b], [w_out[:mw], w_out[mw:]], (x2d, c2d), mods[0], 2, seg_big,
                          tm_big, 512, "ab_out_proj")
    h = _modulate_halo(xs, norm_ffn_g[0], mods[0], 3, 4, seg_ffn, tm_ffn)
    ffn_kw = dict(tm=tm_ffn, tf=tf, n_lat_rows=n_lat_rows, seq=seq, ctx_len=ctx_len)
    xs, h = _conv_ffn(h, xs, _pad_ffn_weights(ffn_w_up, ffn_conv_w, ffn_conv_b, ffn_w_down, 0, tf),
                      mods[0], 5, seg_ffn, norm_mix_g[1], mods[1], **ffn_kw)

    qkv = _matmul(h, na_w_qkv[0].astype(BF16), BF16, tm_big, _pick(3 * d, (1024, 512)), "na_qkv_proj", unit=hd)
    t2, rmask = _na_tables(na_rpb[0], rows)
    o = _na_attention(qkv, t2, rmask, n_batch=n_batch, seq=seq, ctx_len=ctx_len, n_heads=na_heads, hd=hd,
                      kh=kh, win_h=(na_rpb.shape[2] + 1) // 2)
    xl = _matmul_residual([o], [na_w_out[0].astype(BF16)], xs, mods[1], 2, seg_big, tm_big, 512,
                          "na_out_proj")
    h = _modulate_halo(xl, norm_ffn_g[1], mods[1], 3, 4, seg_ffn, tm_ffn)
    out = _conv_ffn(h, xl, _pad_ffn_weights(ffn_w_up, ffn_conv_w, ffn_conv_b, ffn_w_down, 1, tf),
                    mods[1], 5, seg_ffn, final_norm_g, **ffn_kw)
    return out.reshape(n_batch, seq, d)
```
